```python
import math
import jax, jax.numpy as jnp
from jax import lax
import numpy as np

D_MODEL = 2048
BATCH = 8
SEQ = 2048
DEPTH = 4

D_MIX = D_MODEL
MLA_HEADS = 8
MLA_NOPE = 128
MLA_ROPE = 64
MLA_V = 128
KV_RANK = 512
RET_HEADS = 4
RET_DK = 128
RET_DV = 128
HG_HEADS = 4
HG_DK = 128
HG_DV = 128
D_FF = 5632
Q_BLOCK = 128
RET_CHUNK = 128
HG_CHUNK = 64
ROPE_BASE = 10000.0
EPS = 1e-6
MASK_VALUE = -1e30
MIN_FORGET = 1e-20

MLA_OUT = MLA_HEADS * MLA_V
RET_OUT = RET_HEADS * RET_DV
HG_OUT = HG_HEADS * HG_DV
IN_SPLITS = (MLA_HEADS * (MLA_NOPE + MLA_ROPE), KV_RANK, MLA_ROPE,
             RET_HEADS * RET_DK, RET_HEADS * RET_DK, RET_OUT, RET_OUT,
             HG_HEADS * HG_DK, HG_HEADS * HG_DK, HG_OUT, HG_OUT)
D_IN = sum(IN_SPLITS)

kernel_name = "hymba_mla_retnet_hgrn2_macaron"


def rms_norm(x, w):
    xf = x.astype(jnp.float32)
    y = xf * lax.rsqrt(jnp.mean(xf * xf, axis=-1, keepdims=True) + EPS)
    return y.astype(x.dtype) * w


def rope_tables(n, dim):
    inv = ROPE_BASE ** (-jnp.arange(0, dim, 2, dtype=jnp.float32) / dim)
    ang = jnp.arange(n, dtype=jnp.float32)[:, None] * inv[None, :]
    return jnp.cos(ang), jnp.sin(ang)


def apply_rope(x, cos, sin):
    shape = (1, cos.shape[0]) + (1,) * (x.ndim - 3) + (cos.shape[1],)
    c = cos.reshape(shape).astype(x.dtype)
    s = sin.reshape(shape).astype(x.dtype)
    x1, x2 = jnp.split(x, 2, axis=-1)
    return jnp.concatenate([x1 * c - x2 * s, x2 * c + x1 * s], axis=-1)


def swiglu(x, w1, w3, w2):
    return (jax.nn.silu(x @ w1) * (x @ w3)) @ w2


def mla_attention(q_all, c_kv, k_rope, kv_norm_w, w_kv_b, out_norm_w, cos, sin):
    B, S, _ = q_all.shape
    q = q_all.reshape(B, S, MLA_HEADS, MLA_NOPE + MLA_ROPE)
    q_nope = q[..., :MLA_NOPE]
    q_rope = apply_rope(q[..., MLA_NOPE:], cos, sin)
    k_rope = apply_rope(k_rope, cos, sin)
    kv = (rms_norm(c_kv, kv_norm_w) @ w_kv_b).reshape(B, S, MLA_HEADS, MLA_NOPE + MLA_V)
    k_nope, v = kv[..., :MLA_NOPE], kv[..., MLA_NOPE:]
    scale = (MLA_NOPE + MLA_ROPE) ** -0.5
    outs = []
    for blk in range(S // Q_BLOCK):
        q0, q1 = blk * Q_BLOCK, (blk + 1) * Q_BLOCK
        s = (jnp.einsum('bqhd,bkhd->bhqk', q_nope[:, q0:q1], k_nope[:, :q1])
             + jnp.einsum('bqhr,bkr->bhqk', q_rope[:, q0:q1], k_rope[:, :q1]))
        s = s.astype(jnp.float32) * scale
        qpos = jnp.arange(q0, q1)
        kpos = jnp.arange(q1)
        s = jnp.where(kpos[None, :] <= qpos[:, None], s, MASK_VALUE)
        p = jax.nn.softmax(s, axis=-1).astype(v.dtype)
        outs.append(jnp.einsum('bhqk,bkhd->bqhd', p, v[:, :q1]))
    o = jnp.concatenate(outs, axis=1)
    o = rms_norm(o, out_norm_w.reshape(MLA_HEADS, MLA_V))
    return o.reshape(B, S, MLA_OUT)


def retention(q, k, v, g, gn_w, cos, sin):
    B, S, _ = q.shape
    dt = g.dtype
    q = apply_rope(q.reshape(B, S, RET_HEADS, RET_DK), cos, sin)
    k = apply_rope(k.reshape(B, S, RET_HEADS, RET_DK), cos, sin) * (RET_DK ** -0.5)
    v = v.reshape(B, S, RET_HEADS, RET_DV)
    log_gamma = jnp.log1p(-(2.0 ** (-5.0 - jnp.arange(RET_HEADS, dtype=jnp.float32))))
    idx = jnp.arange(RET_CHUNK, dtype=jnp.float32)
    rel = idx[:, None] - idx[None, :]
    decay = jnp.where(rel >= 0, jnp.exp(log_gamma[:, None, None] * jnp.maximum(rel, 0.0)), 0.0)
    xi = jnp.exp(log_gamma[None, :] * (idx[:, None] + 1.0))
    zeta = jnp.exp(log_gamma[:, None] * (RET_CHUNK - 1.0 - idx[None, :]))
    g_chunk = jnp.exp(log_gamma * RET_CHUNK)
    n = S // RET_CHUNK

    def to_chunks(t):
        return t.astype(jnp.float32).reshape(B, n, RET_CHUNK, RET_HEADS, -1).swapaxes(0, 1)

    def step(R, inp):
        qi, ki, vi = inp
        intra = jnp.einsum('bihd,bjhd->bhij', qi, ki) * decay[None]
        o = (jnp.einsum('bhij,bjhe->bihe', intra, vi)
             + jnp.einsum('bihd,bhde->bihe', qi, R) * xi[None, :, :, None])
        R = R * g_chunk[None, :, None, None] + jnp.einsum('bjhd,bjhe,hj->bhde', ki, vi, zeta)
        return R, o

    R0 = jnp.zeros((B, RET_HEADS, RET_DK, RET_DV), jnp.float32)
    _, o = lax.scan(step, R0, (to_chunks(q), to_chunks(k), to_chunks(v)))
    o = o.swapaxes(0, 1).reshape(B, S, RET_HEADS, RET_DV)
    mu = jnp.mean(o, axis=-1, keepdims=True)
    var = jnp.mean(jnp.square(o - mu), axis=-1, keepdims=True)
    o = ((o - mu) * lax.rsqrt(var + EPS)).astype(dt) * gn_w.reshape(RET_HEADS, RET_DV)
    return jax.nn.silu(g) * o.reshape(B, S, RET_OUT)


def hgrn2(q, f_logit, i_in, g, lb, norm_w):
    B, S, _ = q.shape
    dt = g.dtype
    shp = (B, S, HG_HEADS, HG_DK)
    q = q.reshape(shp).astype(jnp.float32)
    z = f_logit.reshape(shp).astype(jnp.float32)
    v = i_in.reshape(B, S, HG_HEADS, HG_DV).astype(jnp.float32)
    lb = lb.reshape(HG_HEADS, HG_DK).astype(jnp.float32)
    f = lb + (1.0 - lb) * jax.nn.sigmoid(z)
    log_f = jnp.log(jnp.maximum(f, MIN_FORGET))
    k = (1.0 - lb) * jax.nn.sigmoid(-z)
    n = S // HG_CHUNK
    causal = (jnp.arange(HG_CHUNK)[:, None] >= jnp.arange(HG_CHUNK)[None, :])[None, :, :, None, None]

    def to_chunks(t):
        return t.reshape(B, n, HG_CHUNK, HG_HEADS, -1).swapaxes(0, 1)

    def step(St, inp):
        qi, ki, vi, lfi = inp
        b = jnp.cumsum(lfi, axis=1)
        diff = b[:, :, None] - b[:, None, :]
        dec = jnp.where(causal, jnp.exp(jnp.where(causal, diff, 0.0)), 0.0)
        A = jnp.einsum('bihk,bjhk,bijhk->bhij', qi, ki, dec)
        o = (jnp.einsum('bhij,bjhv->bihv', A, vi)
             + jnp.einsum('bihk,bhkv->bihv', qi * jnp.exp(b), St))
        b_last = b[:, -1:]
        St = (jnp.exp(b_last[:, 0])[..., None] * St
              + jnp.einsum('bjhk,bjhv->bhkv', ki * jnp.exp(b_last - b), vi))
        return St, o

    S0 = jnp.zeros((B, HG_HEADS, HG_DK, HG_DV), jnp.float32)
    _, o = lax.scan(step, S0, (to_chunks(q), to_chunks(k), to_chunks(v), to_chunks(log_f)))
    o = o.swapaxes(0, 1).reshape(B, S, HG_HEADS, HG_DV)
    o = o * lax.rsqrt(jnp.mean(o * o, axis=-1, keepdims=True) + EPS)
    o = o.astype(dt) * norm_w.reshape(HG_HEADS, HG_DV)
    return jax.nn.silu(g) * o.reshape(B, S, HG_OUT)


def token_mixer(h, w_in, kv_norm_w, w_kv_b, mla_norm_w, ret_gn_w, lb, hg_norm_w, w_o, rope_mla, rope_ret):
    proj = h @ w_in
    cuts = [int(c) for c in np.cumsum(IN_SPLITS)[:-1]]
    (mq, ckv, krope, rq, rk, rv, rg, hq, hf, hi, hg) = jnp.split(proj, cuts, axis=-1)
    o_a = mla_attention(mq, ckv, krope, kv_norm_w, w_kv_b, mla_norm_w, *rope_mla)
    o_b = retention(rq, rk, rv, rg, ret_gn_w, *rope_ret)
    o_c = hgrn2(hq, hf, hi, hg, lb, hg_norm_w)
    return jnp.concatenate([o_a, o_b, o_c], axis=-1) @ w_o


def setup_inputs(seed: int = 0) -> dict:
    key = jax.random.key(seed)
    ks = jax.random.split(key, 20)
    nrm = lambda k, shape, fan: jax.random.normal(k, shape, jnp.float32) * (fan ** -0.5)
    gain = lambda k, shape: 1.0 + 0.02 * jax.random.normal(k, shape, jnp.float32)
    D = D_MODEL
    return {
        "x": jax.random.normal(ks[0], (BATCH, SEQ, D), jnp.float32),
        "ffn1_norm": gain(ks[1], (DEPTH, D)),
        "ffn1_w1": nrm(ks[2], (DEPTH, D, D_FF), D),
        "ffn1_w3": nrm(ks[3], (DEPTH, D, D_FF), D),
        "ffn1_w2": nrm(ks[4], (DEPTH, D_FF, D), D_FF),
        "mix_norm": gain(ks[5], (DEPTH, D)),
        "w_in": nrm(ks[6], (DEPTH, D, D_IN), D),
        "mla_kv_norm": gain(ks[7], (DEPTH, KV_RANK)),
        "mla_w_kv_b": nrm(ks[8], (DEPTH, KV_RANK, MLA_HEADS * (MLA_NOPE + MLA_V)), KV_RANK),
        "mla_out_norm": gain(ks[9], (DEPTH, MLA_OUT)),
        "ret_gn": gain(ks[10], (DEPTH, RET_OUT)),
        "hgrn_lb_logits": 0.5 * jax.random.normal(ks[11], (DEPTH, HG_HEADS * HG_DK), jnp.float32),
        "hgrn_out_norm": gain(ks[12], (DEPTH, HG_OUT)),
        "w_o": nrm(ks[13], (DEPTH, D_MIX, D), D_MIX),
        "ffn2_norm": gain(ks[14], (DEPTH, D)),
        "ffn2_w1": nrm(ks[15], (DEPTH, D, D_FF), D),
        "ffn2_w3": nrm(ks[16], (DEPTH, D, D_FF), D),
        "ffn2_w2": nrm(ks[17], (DEPTH, D_FF, D), D_FF),
        "final_norm": gain(ks[18], (D,)),
    }


def reference(x, ffn1_norm, ffn1_w1, ffn1_w3, ffn1_w2, mix_norm, w_in, mla_kv_norm, mla_w_kv_b,
              mla_out_norm, ret_gn, hgrn_lb_logits, hgrn_out_norm, w_o, ffn2_norm, ffn2_w1,
              ffn2_w3, ffn2_w2, final_norm):
    S = x.shape[1]
    rope_mla = rope_tables(S, MLA_ROPE)
    rope_ret = rope_tables(S, RET_DK)
    p = jax.nn.softmax(hgrn_lb_logits.astype(jnp.float32), axis=0)
    lbs = jnp.cumsum(p, axis=0) - p[0:1]
    h = x
    for l in range(DEPTH):
        h = h + 0.5 * swiglu(rms_norm(h, ffn1_norm[l]), ffn1_w1[l], ffn1_w3[l], ffn1_w2[l])
        h = h + token_mixer(rms_norm(h, mix_norm[l]), w_in[l], mla_kv_norm[l], mla_w_kv_b[l],
                            mla_out_norm[l], ret_gn[l], lbs[l], hgrn_out_norm[l], w_o[l],
                            rope_mla, rope_ret)
        h = h + 0.5 * swiglu(rms_norm(h, ffn2_norm[l]), ffn2_w1[l], ffn2_w3[l], ffn2_w2[l])
    return rms_norm(h, final_norm)
```

```python
import functools
import math

import jax
import jax.numpy as jnp
from jax import lax
from jax.experimental import pallas as pl
from jax.experimental.pallas import tpu as pltpu

F32 = jnp.float32
BF16 = jnp.bfloat16

D_MODEL = 2048
DEPTH = 4
MLA_HEADS = 8
MLA_NOPE = 128
MLA_ROPE = 64
MLA_V = 128
KV_RANK = 512
RET_HEADS = 4
RET_D = 128
HG_HEADS = 4
HG_D = 128
D_FF = 5632
ROPE_BASE = 10000.0
EPS = 1e-6
MASK_VALUE = -1e30
MIN_FORGET = 1e-20

LANES = 128
SUBLANES = 8
VMEM_LIMIT = 56 * 1024 * 1024

COL_QN = 0
COL_QR = 1024
COL_CKV = 1536
COL_RET = 2048
COL_HG = 4096
COL_KR = 6144
D_IN_PAD = 6400
IN_TN = 1280

FFN_TM = 512
FFN_TF = 512
PROJ_TM = 1024
KV_TM = 512
ATT_T = 512
RET_C = 256
HG_C = 64
OUT_TM = 512


def _cparams(sem):
    return pltpu.CompilerParams(dimension_semantics=sem, vmem_limit_bytes=VMEM_LIMIT)


def _rms(x, w):
    return (x * lax.rsqrt(jnp.mean(x * x, axis=-1, keepdims=True) + EPS)) * w


def _sigmoid(x):
    return 1.0 / (1.0 + jnp.exp(-x))


def _dot(a, b):
    return jnp.dot(a, b, preferred_element_type=F32)


def _dot_nt(a, b):
    return lax.dot_general(a, b, (((1,), (1,)), ((), ())), preferred_element_type=F32)


def _dot_tn(a, b):
    return lax.dot_general(a, b, (((0,), (0,)), ((), ())), preferred_element_type=F32)


def _ffn_kernel(x_ref, nw_ref, w1_ref, w3_ref, w2_ref, *rest, n_f, final):
    if final:
        fw_ref, o_ref, n_ref = rest
    else:
        o_ref, n_ref = rest
    f = pl.program_id(1)

    @pl.when(f == 0)
    def _():
        x = x_ref[...]
        n_ref[...] = _rms(x, nw_ref[...]).astype(BF16)
        o_ref[...] = x

    n = n_ref[...]
    h1 = _dot(n, w1_ref[...])
    h3 = _dot(n, w3_ref[...])
    g = (h1 * _sigmoid(h1) * h3 * 0.5).astype(BF16)
    o_ref[...] += _dot(g, w2_ref[...])

    if final:
        @pl.when(f == n_f - 1)
        def _():
            o_ref[...] = _rms(o_ref[...], fw_ref[...])


def _ffn(h, nw, w1, w3, w2, final_w=None):
    t = h.shape[0]
    n_f = D_FF // FFN_TF
    final = final_w is not None
    in_specs = [
        pl.BlockSpec((FFN_TM, D_MODEL), lambda i, f: (i, 0)),
        pl.BlockSpec((1, D_MODEL), lambda i, f: (0, 0)),
        pl.BlockSpec((D_MODEL, FFN_TF), lambda i, f: (0, f)),
        pl.BlockSpec((D_MODEL, FFN_TF), lambda i, f: (0, f)),
        pl.BlockSpec((FFN_TF, D_MODEL), lambda i, f: (f, 0)),
    ]
    args = [h, nw, w1, w3, w2]
    if final:
        in_specs.append(pl.BlockSpec((1, D_MODEL), lambda i, f: (0, 0)))
        args.append(final_w)
    return pl.pallas_call(
        functools.partial(_ffn_kernel, n_f=n_f, final=final),
        grid=(t // FFN_TM, n_f),
        in_specs=in_specs,
        out_specs=pl.BlockSpec((FFN_TM, D_MODEL), lambda i, f: (i, 0)),
        out_shape=jax.ShapeDtypeStruct((t, D_MODEL), F32),
        scratch_shapes=[pltpu.VMEM((FFN_TM, D_MODEL), BF16)],
        compiler_params=_cparams(("parallel", "arbitrary")),
        name="ffn_final" if final else "ffn",
    )(*args)


def _inproj_kernel(x_ref, nw_ref, w_ref, o_ref, n_ref):
    @pl.when(pl.program_id(1) == 0)
    def _():
        n_ref[...] = _rms(x_ref[...], nw_ref[...]).astype(BF16)

    o_ref[...] = _dot(n_ref[...], w_ref[...])


def _inproj(h, nw, w):
    t = h.shape[0]
    return pl.pallas_call(
        _inproj_kernel,
        grid=(t // PROJ_TM, D_IN_PAD // IN_TN),
        in_specs=[
            pl.BlockSpec((PROJ_TM, D_MODEL), lambda i, j: (i, 0)),
            pl.BlockSpec((1, D_MODEL), lambda i, j: (0, 0)),
            pl.BlockSpec((D_MODEL, IN_TN), lambda i, j: (0, j)),
        ],
        out_specs=pl.BlockSpec((PROJ_TM, IN_TN), lambda i, j: (i, j)),
        out_shape=jax.ShapeDtypeStruct((t, D_IN_PAD), F32),
        scratch_shapes=[pltpu.VMEM((PROJ_TM, D_MODEL), BF16)],
        compiler_params=_cparams(("parallel", "arbitrary")),
        name="inproj",
    )(h, nw, w)


def _rope_pair(p, c4, sa, sb):
    return p * c4 + pltpu.roll(p, 96, 1) * sa + pltpu.roll(p, 32, 1) * sb


def _rope_full(x, c, s):
    return x * c + pltpu.roll(x, 64, 1) * s


def _kv_kernel(ckv_ref, kr_ref, nw_ref, w_ref, c4_ref, sa_ref, sb_ref, kcat_ref, v_ref):
    n = _rms(ckv_ref[...], nw_ref[...]).astype(BF16)
    kv = _dot(n, w_ref[...])
    kr = _rope_pair(kr_ref[...], c4_ref[...], sa_ref[...], sb_ref[...])
    kr_lo = kr.astype(BF16)
    kr_hi = pltpu.roll(kr, 64, 1).astype(BF16)
    for h in range(MLA_HEADS):
        kcat_ref[:, h * 256:h * 256 + 128] = kv[:, h * 128:(h + 1) * 128].astype(BF16)
        kcat_ref[:, h * 256 + 128:(h + 1) * 256] = kr_lo if h % 2 == 0 else kr_hi
    v_ref[...] = kv[:, MLA_HEADS * MLA_NOPE:].astype(BF16)


def _kv_prep(proj, nw, w, c4, sa, sb, seq):
    t = proj.shape[0]
    ns = seq // KV_TM
    tab = pl.BlockSpec((KV_TM, LANES), lambda i: (i % ns, 0))
    return pl.pallas_call(
        _kv_kernel,
        grid=(t // KV_TM,),
        in_specs=[
            pl.BlockSpec((KV_TM, KV_RANK), lambda i: (i, COL_CKV // KV_RANK)),
            pl.BlockSpec((KV_TM, LANES), lambda i: (i, COL_KR // LANES)),
            pl.BlockSpec((1, KV_RANK), lambda i: (0, 0)),
            pl.BlockSpec((KV_RANK, 2048), lambda i: (0, 0)),
            tab, tab, tab,
        ],
        out_specs=[
            pl.BlockSpec((KV_TM, 2048), lambda i: (i, 0)),
            pl.BlockSpec((KV_TM, 1024), lambda i: (i, 0)),
        ],
        out_shape=[
            jax.ShapeDtypeStruct((t, 2048), BF16),
            jax.ShapeDtypeStruct((t, 1024), BF16),
        ],
        compiler_params=_cparams(("parallel",)),
        name="kv_prep",
    )(proj, proj, nw, w, c4, sa, sb)


def _attn_kernel(qn_ref, qr_ref, c4_ref, sa_ref, sb_ref, k_ref, v_ref, nw_ref, o_ref):
    i = pl.program_id(1)
    t = ATT_T
    scale = (MLA_NOPE + MLA_ROPE) ** -0.5
    row = lax.broadcasted_iota(jnp.int32, (t, t), 0)
    col = lax.broadcasted_iota(jnp.int32, (t, t), 1)
    causal = col <= row

    for h in range(MLA_HEADS):
        if h % 2 == 0:
            pair = qr_ref[:, (h // 2) * LANES:(h // 2 + 1) * LANES]
            roped = _rope_pair(pair, c4_ref[...], sa_ref[...], sb_ref[...]) * scale
        qn = qn_ref[:, h * LANES:(h + 1) * LANES] * scale
        q = jnp.concatenate([qn, roped], axis=-1).astype(BF16)

        def tile(j, carry, masked, h=h, q=q):
            m, l, acc = carry
            start = pl.multiple_of(j * t, t)
            k = k_ref[pl.ds(start, t), h * 256:(h + 1) * 256]
            v = v_ref[pl.ds(start, t), h * LANES:(h + 1) * LANES]
            s = _dot_nt(q, k)
            if masked:
                s = jnp.where(causal, s, MASK_VALUE)
            m_new = jnp.maximum(m, jnp.max(s, axis=-1, keepdims=True))
            p = jnp.exp(s - m_new)
            alpha = jnp.exp(m - m_new)
            l = alpha * l + jnp.sum(p, axis=-1, keepdims=True)
            acc = alpha * acc + _dot(p.astype(BF16), v)
            return m_new, l, acc

        init = (jnp.full((t, 1), MASK_VALUE, F32), jnp.zeros((t, 1), F32), jnp.zeros((t, LANES), F32))
        carry = lax.fori_loop(0, i, functools.partial(tile, masked=False), init)
        m, l, acc = tile(i, carry, True)
        o = acc / l
        o = _rms(o, nw_ref[:, h * LANES:(h + 1) * LANES])
        o_ref[:, h * LANES:(h + 1) * LANES] = o.astype(BF16)


def _attention(proj, c4, sa, sb, kcat, v, nw, batch, seq):
    t = proj.shape[0]
    nq = seq // ATT_T
    tab = pl.BlockSpec((ATT_T, LANES), lambda b, i: (i, 0))
    return pl.pallas_call(
        _attn_kernel,
        grid=(batch, nq),
        in_specs=[
            pl.BlockSpec((ATT_T, 1024), lambda b, i: (b * nq + i, COL_QN // 1024)),
            pl.BlockSpec((ATT_T, 512), lambda b, i: (b * nq + i, COL_QR // 512)),
            tab, tab, tab,
            pl.BlockSpec((seq, 2048), lambda b, i: (b, 0)),
            pl.BlockSpec((seq, 1024), lambda b, i: (b, 0)),
            pl.BlockSpec((1, 1024), lambda b, i: (0, 0)),
        ],
        out_specs=pl.BlockSpec((ATT_T, 1024), lambda b, i: (b * nq + i, 0)),
        out_shape=jax.ShapeDtypeStruct((t, 1024), BF16),
        compiler_params=_cparams(("parallel", "arbitrary")),
        name="mla_attention",
    )(proj, proj, c4, sa, sb, kcat, v, nw)


def _ret_kernel(x_ref, c_ref, s_ref, gw_ref, o_ref, state_ref):
    c = RET_C

    @pl.when(pl.program_id(1) == 0)
    def _():
        state_ref[...] = jnp.zeros_like(state_ref)

    row = lax.broadcasted_iota(jnp.int32, (c, c), 0)
    col = lax.broadcasted_iota(jnp.int32, (c, c), 1)
    rel = (row - col).astype(F32)
    idx = lax.broadcasted_iota(jnp.int32, (c, 1), 0).astype(F32)
    cosv = c_ref[...]
    sinv = s_ref[...]
    for h in range(RET_HEADS):
        lg = math.log1p(-(2.0 ** (-5.0 - h)))
        sl = slice(h * RET_D, (h + 1) * RET_D)
        q = _rope_full(x_ref[:, sl], cosv, sinv)
        k = _rope_full(x_ref[:, 512 + h * RET_D:512 + (h + 1) * RET_D], cosv, sinv) * (RET_D ** -0.5)
        v = x_ref[:, 1024 + h * RET_D:1024 + (h + 1) * RET_D]
        g = x_ref[:, 1536 + h * RET_D:1536 + (h + 1) * RET_D]
        decay = jnp.where(rel >= 0, jnp.exp(lg * jnp.maximum(rel, 0.0)), 0.0)
        xi = jnp.exp(lg * (idx + 1.0))
        zeta = jnp.exp(lg * (c - 1.0 - idx))
        qb = q.astype(BF16)
        vb = v.astype(BF16)
        state = state_ref[h]
        intra = _dot_nt(qb, k.astype(BF16)) * decay
        o = _dot(intra.astype(BF16), vb) + _dot(qb, state.astype(BF16)) * xi
        state_ref[h] = state * math.exp(lg * c) + _dot_tn((k * zeta).astype(BF16), vb)
        mu = jnp.mean(o, axis=-1, keepdims=True)
        d = o - mu
        var = jnp.mean(d * d, axis=-1, keepdims=True)
        o = d * lax.rsqrt(var + EPS) * gw_ref[:, sl]
        o_ref[:, sl] = (g * _sigmoid(g) * o).astype(BF16)


def _retention(proj, cosf, sinf, gw, batch, seq):
    t = proj.shape[0]
    nc = seq // RET_C
    tab = pl.BlockSpec((RET_C, LANES), lambda b, c: (c, 0))
    return pl.pallas_call(
        _ret_kernel,
        grid=(batch, nc),
        in_specs=[
            pl.BlockSpec((RET_C, 2048), lambda b, c: (b * nc + c, COL_RET // 2048)),
            tab, tab,
            pl.BlockSpec((1, 512), lambda b, c: (0, 0)),
        ],
        out_specs=pl.BlockSpec((RET_C, 512), lambda b, c: (b * nc + c, 0)),
        out_shape=jax.ShapeDtypeStruct((t, 512), BF16),
        scratch_shapes=[pltpu.VMEM((RET_HEADS, RET_D, RET_D), F32)],
        compiler_params=_cparams(("parallel", "arbitrary")),
        name="retention",
    )(proj, cosf, sinf, gw)


def _hgrn_kernel(x_ref, lbl_ref, nw_ref, o_ref, state_ref, *, layer):
    c = HG_C

    @pl.when(pl.program_id(1) == 0)
    def _():
        state_ref[...] = jnp.zeros_like(state_ref)

    logits = lbl_ref[...]
    e = jnp.exp(logits - jnp.max(logits, axis=0, keepdims=True))
    p = e / jnp.sum(e, axis=0, keepdims=True)
    lb_all = jnp.zeros((1, HG_HEADS * HG_D), F32)
    for m in range(layer + 1):
        lb_all = lb_all + p[m:m + 1, :]
    lb_all = lb_all - p[0:1, :]

    rowv = lax.broadcasted_iota(jnp.int32, (c, LANES), 0)
    row = lax.broadcasted_iota(jnp.int32, (c, c), 0)
    col = lax.broadcasted_iota(jnp.int32, (c, c), 1)

    for h in range(HG_HEADS):
        sl = slice(h * HG_D, (h + 1) * HG_D)
        q = x_ref[:, sl]
        z = x_ref[:, 512 + h * HG_D:512 + (h + 1) * HG_D]
        v = x_ref[:, 1024 + h * HG_D:1024 + (h + 1) * HG_D]
        g = x_ref[:, 1536 + h * HG_D:1536 + (h + 1) * HG_D]
        lb = lb_all[:, sl]

        ez = jnp.exp(-jnp.abs(z))
        r = 1.0 / (1.0 + ez)
        pos = z >= 0
        sig_p = jnp.where(pos, r, ez * r)
        sig_n = jnp.where(pos, ez * r, r)
        f = lb + (1.0 - lb) * sig_p
        lf = jnp.log(jnp.maximum(f, MIN_FORGET))
        kk = (1.0 - lb) * sig_n

        b = lf
        sh = 1
        while sh < c:
            b = b + jnp.where(rowv >= sh, pltpu.roll(b, sh, 0), 0.0)
            sh *= 2

        a = jnp.zeros((c, c), F32)
        for dlt in range(SUBLANES):
            if dlt == 0:
                a_d = jnp.sum(q * kk, axis=-1, keepdims=True)
                a = a + jnp.where(col == row, a_d, 0.0)
            else:
                ok = (rowv & (SUBLANES - 1)) >= dlt
                diff = jnp.where(ok, b - pltpu.roll(b, dlt, 0), 0.0)
                a_d = jnp.sum(q * pltpu.roll(kk, dlt, 0) * jnp.exp(diff), axis=-1, keepdims=True)
                a = a + jnp.where((col == row - dlt) & ((row & (SUBLANES - 1)) >= dlt), a_d, 0.0)
        m = SUBLANES
        while m < c:
            parts = []
            for blk in range(c // (2 * m)):
                lo = blk * 2 * m
                ref = b[lo + m - 1:lo + m, :]
                parts.append(b[lo:lo + 2 * m, :] - ref)
            d = parts[0] if len(parts) == 1 else jnp.concatenate(parts, axis=0)
            second = (rowv & (2 * m - 1)) >= m
            efac = jnp.exp(jnp.where(second, d, -d))
            ql = jnp.where(second, q * efac, 0.0).astype(BF16)
            kl = jnp.where(second, 0.0, kk * efac).astype(BF16)
            al = _dot_nt(ql, kl)
            if 2 * m < c:
                sft = (2 * m).bit_length() - 1
                al = jnp.where((row >> sft) == (col >> sft), al, 0.0)
            a = a + al
            m *= 2

        vb = v.astype(BF16)
        state = state_ref[h]
        b_last = b[c - 1:c, :]
        o = _dot(a.astype(BF16), vb) + _dot_nt((q * jnp.exp(b)).astype(BF16), state.astype(BF16))
        upd = _dot_tn(vb, (kk * jnp.exp(b_last - b)).astype(BF16))
        state_ref[h] = jnp.exp(b_last) * state + upd

        o = o * lax.rsqrt(jnp.mean(o * o, axis=-1, keepdims=True) + EPS)
        o = o * nw_ref[:, sl]
        o_ref[:, sl] = (g * _sigmoid(g) * o).astype(BF16)


def _hgrn(proj, lb_logits, nw, layer, batch, seq):
    t = proj.shape[0]
    nc = seq // HG_C
    return pl.pallas_call(
        functools.partial(_hgrn_kernel, layer=layer),
        grid=(batch, nc),
        in_specs=[
            pl.BlockSpec((HG_C, 2048), lambda b, c: (b * nc + c, COL_HG // 2048)),
            pl.BlockSpec((DEPTH, 512), lambda b, c: (0, 0)),
            pl.BlockSpec((1, 512), lambda b, c: (0, 0)),
        ],
        out_specs=pl.BlockSpec((HG_C, 512), lambda b, c: (b * nc + c, 0)),
        out_shape=jax.ShapeDtypeStruct((t, 512), BF16),
        scratch_shapes=[pltpu.VMEM((HG_HEADS, HG_D, HG_D), F32)],
        compiler_params=_cparams(("parallel", "arbitrary")),
        name="hgrn2",
    )(proj, lb_logits, nw)


def _outproj_kernel(h_ref, oa_ref, ob_ref, oc_ref, w_ref, o_ref):
    acc = _dot(oa_ref[...], w_ref[0:1024, :])
    acc += _dot(ob_ref[...], w_ref[1024:1536, :])
    acc += _dot(oc_ref[...], w_ref[1536:2048, :])
    o_ref[...] = h_ref[...] + acc


def _outproj(h, oa, ob, oc, w):
    t = h.shape[0]
    return pl.pallas_call(
        _outproj_kernel,
        grid=(t // OUT_TM,),
        in_specs=[
            pl.BlockSpec((OUT_TM, D_MODEL), lambda i: (i, 0)),
            pl.BlockSpec((OUT_TM, 1024), lambda i: (i, 0)),
            pl.BlockSpec((OUT_TM, 512), lambda i: (i, 0)),
            pl.BlockSpec((OUT_TM, 512), lambda i: (i, 0)),
            pl.BlockSpec((D_MODEL, D_MODEL), lambda i: (0, 0)),
        ],
        out_specs=pl.BlockSpec((OUT_TM, D_MODEL), lambda i: (i, 0)),
        out_shape=jax.ShapeDtypeStruct((t, D_MODEL), F32),
        compiler_params=_cparams(("parallel",)),
        name="outproj",
    )(h, oa, ob, oc, w)


def _rope_tables(seq):
    inv64 = ROPE_BASE ** (-jnp.arange(0, MLA_ROPE, 2, dtype=F32) / MLA_ROPE)
    ang64 = jnp.arange(seq, dtype=F32)[:, None] * inv64[None, :]
    c, s = jnp.cos(ang64), jnp.sin(ang64)
    z = jnp.zeros_like(s)
    c4 = jnp.concatenate([c, c, c, c], axis=-1)
    sa = jnp.concatenate([-s, z, -s, z], axis=-1)
    sb = jnp.concatenate([z, s, z, s], axis=-1)
    inv128 = ROPE_BASE ** (-jnp.arange(0, RET_D, 2, dtype=F32) / RET_D)
    ang128 = jnp.arange(seq, dtype=F32)[:, None] * inv128[None, :]
    cf, sf = jnp.cos(ang128), jnp.sin(ang128)
    return c4, sa, sb, jnp.concatenate([cf, cf], axis=-1), jnp.concatenate([-sf, sf], axis=-1)


def _prep_w_in(w_in):
    d = w_in.shape[1]
    nq = MLA_HEADS * (MLA_NOPE + MLA_ROPE)
    q = w_in[:, :, :nq].reshape(DEPTH, d, MLA_HEADS, MLA_NOPE + MLA_ROPE)
    qn = q[..., :MLA_NOPE].reshape(DEPTH, d, MLA_HEADS * MLA_NOPE)
    qr = q[..., MLA_NOPE:].reshape(DEPTH, d, MLA_HEADS * MLA_ROPE)
    ckv = w_in[:, :, nq:nq + KV_RANK]
    kr = w_in[:, :, nq + KV_RANK:nq + KV_RANK + MLA_ROPE]
    rest = w_in[:, :, nq + KV_RANK + MLA_ROPE:]
    pad = jnp.zeros((DEPTH, d, D_IN_PAD - COL_KR - MLA_ROPE), w_in.dtype)
    return jnp.concatenate([qn, qr, ckv, rest, kr, pad], axis=-1).astype(BF16)


def _prep_w_kv(w):
    w = w.reshape(DEPTH, KV_RANK, MLA_HEADS, MLA_NOPE + MLA_V)
    kn = w[..., :MLA_NOPE].reshape(DEPTH, KV_RANK, MLA_HEADS * MLA_NOPE)
    v = w[..., MLA_NOPE:].reshape(DEPTH, KV_RANK, MLA_HEADS * MLA_V)
    return jnp.concatenate([kn, v], axis=-1).astype(BF16)


def kernel(x, ffn1_norm, ffn1_w1, ffn1_w3, ffn1_w2, mix_norm, w_in, mla_kv_norm, mla_w_kv_b,
           mla_out_norm, ret_gn, hgrn_lb_logits, hgrn_out_norm, w_o, ffn2_norm, ffn2_w1,
           ffn2_w3, ffn2_w2, final_norm):
    batch, seq, d = x.shape
    assert d == D_MODEL and seq % ATT_T == 0 and seq % RET_C == 0 and seq % KV_TM == 0
    t = batch * seq
    assert t % PROJ_TM == 0 and t % FFN_TM == 0
    c4, sa, sb, cosf, sinf = _rope_tables(seq)
    w_in_p = _prep_w_in(w_in)
    w_kv_p = _prep_w_kv(mla_w_kv_b)
    row = lambda a: a.reshape(1, -1)

    h = x.reshape(t, d)
    for l in range(DEPTH):
        h = _ffn(h, row(ffn1_norm[l]), ffn1_w1[l].astype(BF16), ffn1_w3[l].astype(BF16),
                 ffn1_w2[l].astype(BF16))
        proj = _inproj(h, row(mix_norm[l]), w_in_p[l])
        kcat, v = _kv_prep(proj, row(mla_kv_norm[l]), w_kv_p[l], c4, sa, sb, seq)
        oa = _attention(proj, c4, sa, sb, kcat, v, row(mla_out_norm[l]), batch, seq)
        ob = _retention(proj, cosf, sinf, row(ret_gn[l]), batch, seq)
        oc = _hgrn(proj, hgrn_lb_logits, row(hgrn_out_norm[l]), l, batch, seq)
        h = _outproj(h, oa, ob, oc, w_o[l].astype(BF16))
        h = _ffn(h, row(ffn2_norm[l]), ffn2_w1[l].astype(BF16), ffn2_w3[l].astype(BF16),
                 ffn2_w2[l].astype(BF16), final_w=row(final_norm) if l == DEPTH - 1 else None)
    return h.reshape(batch, seq, d)
```

```python
import functools
import math

import jax
import jax.numpy as jnp
from jax import lax
from jax.experimental import pallas as pl
from jax.experimental.pallas import tpu as pltpu

F32 = jnp.float32
BF16 = jnp.bfloat16

D_MODEL = 2048
DEPTH = 4
MLA_HEADS = 8
MLA_NOPE = 128
MLA_ROPE = 64
MLA_V = 128
KV_RANK = 512
RET_HEADS = 4
RET_D = 128
HG_HEADS = 4
HG_D = 128
D_FF = 5632
ROPE_BASE = 10000.0
EPS = 1e-6
MASK_VALUE = -1e30
MIN_FORGET = 1e-20

LANES = 128
SUBLANES = 8
VMEM_LIMIT = 56 * 1024 * 1024

COL_QN = 0
COL_QR = 1024
COL_CKV = 1536
COL_RET = 2048
COL_HG = 4096
COL_KR = 6144
D_IN_PAD = 6400
IN_TN = 1280

FFN_TM = 1024
FFN_TF = 256
PROJ_TM = 1024
ATT_TQ = 256
ATT_TK = 512
ATT_ONES = 16
KV_TM = ATT_TK
RET_C = 256
HG_C = 64
OUT_TM = 512


def _cparams(sem):
    return pltpu.CompilerParams(dimension_semantics=sem, vmem_limit_bytes=VMEM_LIMIT)


def _rms(x, w):
    return (x * lax.rsqrt(jnp.mean(x * x, axis=-1, keepdims=True) + EPS)) * w


def _sigmoid(x):
    return 1.0 / (1.0 + jnp.exp(-x))


def _dot(a, b):
    return jnp.dot(a, b, preferred_element_type=F32)


def _dot_nt(a, b):
    return lax.dot_general(a, b, (((1,), (1,)), ((), ())), preferred_element_type=F32)


def _dot_tn(a, b):
    return lax.dot_general(a, b, (((0,), (0,)), ((), ())), preferred_element_type=F32)


def _ffn_kernel(x_ref, nw_ref, w1_ref, w3_ref, w2_ref, *rest, n_f, final):
    if final:
        fw_ref, o_ref, n_ref = rest
    else:
        o_ref, n_ref = rest
    f = pl.program_id(1)

    @pl.when(f == 0)
    def _():
        x = x_ref[...]
        n_ref[...] = _rms(x, nw_ref[...]).astype(BF16)
        o_ref[...] = x

    n = n_ref[...]
    h1 = _dot(n, w1_ref[...].astype(BF16))
    h3 = _dot(n, w3_ref[...].astype(BF16))
    g = (h1 * _sigmoid(h1) * h3 * 0.5).astype(BF16)
    o_ref[...] += _dot(g, w2_ref[...].astype(BF16))

    if final:
        @pl.when(f == n_f - 1)
        def _():
            o_ref[...] = _rms(o_ref[...], fw_ref[...])


def _ffn(h, nw, w1, w3, w2, layer, final_w=None):
    t = h.shape[0]
    n_f = D_FF // FFN_TF
    final = final_w is not None
    in_specs = [
        pl.BlockSpec((FFN_TM, D_MODEL), lambda i, f: (i, 0), pipeline_mode=pl.Buffered(1)),
        pl.BlockSpec((1, D_MODEL), lambda i, f: (0, 0)),
        pl.BlockSpec((None, D_MODEL, FFN_TF), lambda i, f: (layer, 0, f)),
        pl.BlockSpec((None, D_MODEL, FFN_TF), lambda i, f: (layer, 0, f)),
        pl.BlockSpec((None, FFN_TF, D_MODEL), lambda i, f: (layer, f, 0)),
    ]
    args = [h, nw, w1, w3, w2]
    if final:
        in_specs.append(pl.BlockSpec((1, D_MODEL), lambda i, f: (0, 0)))
        args.append(final_w)
    return pl.pallas_call(
        functools.partial(_ffn_kernel, n_f=n_f, final=final),
        grid=(t // FFN_TM, n_f),
        in_specs=in_specs,
        out_specs=pl.BlockSpec((FFN_TM, D_MODEL), lambda i, f: (i, 0)),
        out_shape=jax.ShapeDtypeStruct((t, D_MODEL), F32),
        scratch_shapes=[pltpu.VMEM((FFN_TM, D_MODEL), BF16)],
        compiler_params=_cparams(("parallel", "arbitrary")),
        name="ffn_final" if final else "ffn",
    )(*args)


def _inproj_kernel(x_ref, nw_ref, w_ref, o_ref, n_ref):
    @pl.when(pl.program_id(1) == 0)
    def _():
        n_ref[...] = _rms(x_ref[...], nw_ref[...]).astype(BF16)

    o_ref[...] = _dot(n_ref[...], w_ref[...])


def _inproj(h, nw, w, layer):
    t = h.shape[0]
    return pl.pallas_call(
        _inproj_kernel,
        grid=(t // PROJ_TM, D_IN_PAD // IN_TN),
        in_specs=[
            pl.BlockSpec((PROJ_TM, D_MODEL), lambda i, j: (i, 0)),
            pl.BlockSpec((1, D_MODEL), lambda i, j: (0, 0)),
            pl.BlockSpec((None, D_MODEL, IN_TN), lambda i, j: (layer, 0, j)),
        ],
        out_specs=pl.BlockSpec((PROJ_TM, IN_TN), lambda i, j: (i, j)),
        out_shape=jax.ShapeDtypeStruct((t, D_IN_PAD), F32),
        scratch_shapes=[pltpu.VMEM((PROJ_TM, D_MODEL), BF16)],
        compiler_params=_cparams(("parallel", "arbitrary")),
        name="inproj",
    )(h, nw, w)


def _rope_pair(p, c4, sa, sb):
    return p * c4 + pltpu.roll(p, 96, 1) * sa + pltpu.roll(p, 32, 1) * sb


def _rope_full(x, c, s):
    return x * c + pltpu.roll(x, 64, 1) * s


def _kv_kernel(ckv_ref, kr_ref, nw_ref, w_ref, c4_ref, sa_ref, sb_ref, kcat_ref, vt_ref):
    n = _rms(ckv_ref[...], nw_ref[...]).astype(BF16)
    kv = _dot(n, w_ref[...])
    kr = _rope_pair(kr_ref[...], c4_ref[...], sa_ref[...], sb_ref[...])
    kr_lo = kr.astype(BF16)
    kr_hi = pltpu.roll(kr, 64, 1).astype(BF16)
    for h in range(MLA_HEADS):
        kcat_ref[:, h * 256:h * 256 + 128] = kv[:, h * 128:(h + 1) * 128].astype(BF16)
        kcat_ref[:, h * 256 + 128:(h + 1) * 256] = kr_lo if h % 2 == 0 else kr_hi
    vt_ref[...] = kv[:, MLA_HEADS * MLA_NOPE:].T.astype(BF16)


def _kv_prep(proj, nw, w, c4, sa, sb, seq, layer):
    t = proj.shape[0]
    ns = seq // KV_TM
    tab = pl.BlockSpec((KV_TM, LANES), lambda i: (i % ns, 0))
    return pl.pallas_call(
        _kv_kernel,
        grid=(t // KV_TM,),
        in_specs=[
            pl.BlockSpec((KV_TM, KV_RANK), lambda i: (i, COL_CKV // KV_RANK)),
            pl.BlockSpec((KV_TM, LANES), lambda i: (i, COL_KR // LANES)),
            pl.BlockSpec((1, KV_RANK), lambda i: (0, 0)),
            pl.BlockSpec((None, KV_RANK, 2048), lambda i: (layer, 0, 0)),
            tab, tab, tab,
        ],
        out_specs=[
            pl.BlockSpec((KV_TM, 2048), lambda i: (i, 0)),
            pl.BlockSpec((None, 1024, KV_TM), lambda i: (i, 0, 0)),
        ],
        out_shape=[
            jax.ShapeDtypeStruct((t, 2048), BF16),
            jax.ShapeDtypeStruct((t // KV_TM, 1024, KV_TM), BF16),
        ],
        compiler_params=_cparams(("parallel",)),
        name="kv_prep",
    )(proj, proj, nw, w, c4, sa, sb)


def _attn_kernel(qn_ref, qr_ref, c4_ref, sa_ref, sb_ref, k_ref, vt_ref, nw_ref, o_ref,
                 qt_s, m_s, acc_s):
    i = pl.program_id(1)
    tq, tk = ATT_TQ, ATT_TK
    scale = (MLA_NOPE + MLA_ROPE) ** -0.5

    for h in range(MLA_HEADS):
        if h % 2 == 0:
            pair = qr_ref[:, (h // 2) * LANES:(h // 2 + 1) * LANES]
            roped = _rope_pair(pair, c4_ref[...], sa_ref[...], sb_ref[...]) * scale
        qn = qn_ref[:, h * LANES:(h + 1) * LANES] * scale
        qt_s[h] = jnp.concatenate([qn, roped], axis=-1).T.astype(BF16)
    m_s[...] = jnp.full(m_s.shape, MASK_VALUE, F32)
    acc_s[...] = jnp.zeros(acc_s.shape, F32)

    def tile(j, masked):
        ones = jnp.ones((ATT_ONES, tk), BF16)

        def scores(h):
            return _dot(k_ref[j, :, h * 256:(h + 1) * 256], qt_s[h])

        st_next = scores(0)
        for h in range(MLA_HEADS):
            st = st_next
            if h + 1 < MLA_HEADS:
                st_next = scores(h + 1)
            if masked:
                key = j * tk + lax.broadcasted_iota(jnp.int32, (tk, tq), 0)
                qry = i * tq + lax.broadcasted_iota(jnp.int32, (tk, tq), 1)
                st = jnp.where(key <= qry, st, MASK_VALUE)
            m_old = m_s[h]
            m_new = jnp.maximum(m_old, jnp.max(st, axis=0, keepdims=True))
            p = jnp.exp(st - m_new).astype(BF16)
            alpha = jnp.exp(m_old - m_new)
            vext = jnp.concatenate([vt_ref[j, h * LANES:(h + 1) * LANES, :], ones], axis=0)
            acc_s[h] = alpha * acc_s[h] + _dot(vext, p)
            m_s[h] = m_new

    def body(j, carry):
        tile(j, False)
        return carry

    n_full = (i * tq) // tk
    lax.fori_loop(0, n_full, body, 0)
    for d in range(max(1, tq // tk)):
        tile(n_full + d, True)

    for h in range(MLA_HEADS):
        acc = acc_s[h]
        ot = acc[:MLA_V, :] / acc[MLA_V:MLA_V + 1, :]
        ot = ot * lax.rsqrt(jnp.mean(ot * ot, axis=0, keepdims=True) + EPS)
        ot = ot * nw_ref[h * LANES:(h + 1) * LANES, :]
        o_ref[:, h * LANES:(h + 1) * LANES] = ot.T.astype(BF16)


def _attention(proj, c4, sa, sb, kcat, vt, nw_col, batch, seq):
    t = proj.shape[0]
    nq = seq // ATT_TQ
    nk = seq // ATT_TK
    tab = pl.BlockSpec((ATT_TQ, LANES), lambda b, i: (i, 0))
    return pl.pallas_call(
        _attn_kernel,
        grid=(batch, nq),
        in_specs=[
            pl.BlockSpec((ATT_TQ, 1024), lambda b, i: (b * nq + i, COL_QN // 1024)),
            pl.BlockSpec((ATT_TQ, 512), lambda b, i: (b * nq + i, COL_QR // 512)),
            tab, tab, tab,
            pl.BlockSpec((nk, ATT_TK, 2048), lambda b, i: (b, 0, 0)),
            pl.BlockSpec((nk, 1024, ATT_TK), lambda b, i: (b, 0, 0)),
            pl.BlockSpec((1024, 1), lambda b, i: (0, 0)),
        ],
        out_specs=pl.BlockSpec((ATT_TQ, 1024), lambda b, i: (b * nq + i, 0)),
        out_shape=jax.ShapeDtypeStruct((t, 1024), BF16),
        scratch_shapes=[
            pltpu.VMEM((MLA_HEADS, 256, ATT_TQ), BF16),
            pltpu.VMEM((MLA_HEADS, 1, ATT_TQ), F32),
            pltpu.VMEM((MLA_HEADS, MLA_V + ATT_ONES, ATT_TQ), F32),
        ],
        compiler_params=_cparams(("parallel", "arbitrary")),
        name="mla_attention",
    )(proj, proj, c4, sa, sb, kcat.reshape(t // ATT_TK, ATT_TK, 2048), vt, nw_col)


def _ret_kernel(x_ref, c_ref, s_ref, gw_ref, o_ref, state_ref):
    c = RET_C

    @pl.when(pl.program_id(1) == 0)
    def _():
        state_ref[...] = jnp.zeros_like(state_ref)

    row = lax.broadcasted_iota(jnp.int32, (c, c), 0)
    col = lax.broadcasted_iota(jnp.int32, (c, c), 1)
    rel = (row - col).astype(F32)
    idx = lax.broadcasted_iota(jnp.int32, (c, 1), 0).astype(F32)
    cosv = c_ref[...]
    sinv = s_ref[...]
    for h in range(RET_HEADS):
        lg = math.log1p(-(2.0 ** (-5.0 - h)))
        sl = slice(h * RET_D, (h + 1) * RET_D)
        q = _rope_full(x_ref[:, sl], cosv, sinv)
        k = _rope_full(x_ref[:, 512 + h * RET_D:512 + (h + 1) * RET_D], cosv, sinv) * (RET_D ** -0.5)
        v = x_ref[:, 1024 + h * RET_D:1024 + (h + 1) * RET_D]
        g = x_ref[:, 1536 + h * RET_D:1536 + (h + 1) * RET_D]
        decay = jnp.where(rel >= 0, jnp.exp(lg * jnp.maximum(rel, 0.0)), 0.0)
        xi = jnp.exp(lg * (idx + 1.0))
        zeta = jnp.exp(lg * (c - 1.0 - idx))
        qb = q.astype(BF16)
        vb = v.astype(BF16)
        state = state_ref[h]
        intra = _dot_nt(qb, k.astype(BF16)) * decay
        o = _dot(intra.astype(BF16), vb) + _dot(qb, state.astype(BF16)) * xi
        state_ref[h] = state * math.exp(lg * c) + _dot_tn((k * zeta).astype(BF16), vb)
        mu = jnp.mean(o, axis=-1, keepdims=True)
        d = o - mu
        var = jnp.mean(d * d, axis=-1, keepdims=True)
        o = d * lax.rsqrt(var + EPS) * gw_ref[:, sl]
        o_ref[:, sl] = (g * _sigmoid(g) * o).astype(BF16)


def _retention(proj, cosf, sinf, gw, batch, seq):
    t = proj.shape[0]
    nc = seq // RET_C
    tab = pl.BlockSpec((RET_C, LANES), lambda b, c: (c, 0))
    return pl.pallas_call(
        _ret_kernel,
        grid=(batch, nc),
        in_specs=[
            pl.BlockSpec((RET_C, 2048), lambda b, c: (b * nc + c, COL_RET // 2048)),
            tab, tab,
            pl.BlockSpec((1, 512), lambda b, c: (0, 0)),
        ],
        out_specs=pl.BlockSpec((RET_C, 512), lambda b, c: (b * nc + c, 0)),
        out_shape=jax.ShapeDtypeStruct((t, 512), BF16),
        scratch_shapes=[pltpu.VMEM((RET_HEADS, RET_D, RET_D), F32)],
        compiler_params=_cparams(("parallel", "arbitrary")),
        name="retention",
    )(proj, cosf, sinf, gw)


def _hgrn_kernel(x_ref, lbl_ref, nw_ref, o_ref, state_ref, *, layer):
    c = HG_C

    @pl.when(pl.program_id(1) == 0)
    def _():
        state_ref[...] = jnp.zeros_like(state_ref)

    logits = lbl_ref[...]
    e = jnp.exp(logits - jnp.max(logits, axis=0, keepdims=True))
    p = e / jnp.sum(e, axis=0, keepdims=True)
    lb_all = jnp.zeros((1, HG_HEADS * HG_D), F32)
    for m in range(layer + 1):
        lb_all = lb_all + p[m:m + 1, :]
    lb_all = lb_all - p[0:1, :]

    rowv = lax.broadcasted_iota(jnp.int32, (c, LANES), 0)
    row = lax.broadcasted_iota(jnp.int32, (c, c), 0)
    col = lax.broadcasted_iota(jnp.int32, (c, c), 1)

    for h in range(HG_HEADS):
        sl = slice(h * HG_D, (h + 1) * HG_D)
        q = x_ref[:, sl]
        z = x_ref[:, 512 + h * HG_D:512 + (h + 1) * HG_D]
        v = x_ref[:, 1024 + h * HG_D:1024 + (h + 1) * HG_D]
        g = x_ref[:, 1536 + h * HG_D:1536 + (h + 1) * HG_D]
        lb = lb_all[:, sl]

        ez = jnp.exp(-jnp.abs(z))
        r = 1.0 / (1.0 + ez)
        pos = z >= 0
        sig_p = jnp.where(pos, r, ez * r)
        sig_n = jnp.where(pos, ez * r, r)
        f = lb + (1.0 - lb) * sig_p
        lf = jnp.log(jnp.maximum(f, MIN_FORGET))
        kk = (1.0 - lb) * sig_n

        b = lf
        sh = 1
        while sh < c:
            b = b + jnp.where(rowv >= sh, pltpu.roll(b, sh, 0), 0.0)
            sh *= 2

        a = jnp.zeros((c, c), F32)
        for dlt in range(SUBLANES):
            if dlt == 0:
                a_d = jnp.sum(q * kk, axis=-1, keepdims=True)
                a = a + jnp.where(col == row, a_d, 0.0)
            else:
                ok = (rowv & (SUBLANES - 1)) >= dlt
                diff = jnp.where(ok, b - pltpu.roll(b, dlt, 0), 0.0)
                a_d = jnp.sum(q * pltpu.roll(kk, dlt, 0) * jnp.exp(diff), axis=-1, keepdims=True)
                a = a + jnp.where((col == row - dlt) & ((row & (SUBLANES - 1)) >= dlt), a_d, 0.0)
        m = SUBLANES
        while m < c:
            parts = []
            for blk in range(c // (2 * m)):
                lo = blk * 2 * m
                ref = b[lo + m - 1:lo + m, :]
                parts.append(b[lo:lo + 2 * m, :] - ref)
            d = parts[0] if len(parts) == 1 else jnp.concatenate(parts, axis=0)
            second = (rowv & (2 * m - 1)) >= m
            efac = jnp.exp(jnp.where(second, d, -d))
            ql = jnp.where(second, q * efac, 0.0).astype(BF16)
            kl = jnp.where(second, 0.0, kk * efac).astype(BF16)
            al = _dot_nt(ql, kl)
            if 2 * m < c:
                sft = (2 * m).bit_length() - 1
                al = jnp.where((row >> sft) == (col >> sft), al, 0.0)
            a = a + al
            m *= 2

        vb = v.astype(BF16)
        state = state_ref[h]
        b_last = b[c - 1:c, :]
        o = _dot(a.astype(BF16), vb) + _dot_nt((q * jnp.exp(b)).astype(BF16), state.astype(BF16))
        upd = _dot_tn(vb, (kk * jnp.exp(b_last - b)).astype(BF16))
        state_ref[h] = jnp.exp(b_last) * state + upd

        o = o * lax.rsqrt(jnp.mean(o * o, axis=-1, keepdims=True) + EPS)
        o = o * nw_ref[:, sl]
        o_ref[:, sl] = (g * _sigmoid(g) * o).astype(BF16)


def _hgrn(proj, lb_logits, nw, layer, batch, seq):
    t = proj.shape[0]
    nc = seq // HG_C
    return pl.pallas_call(
        functools.partial(_hgrn_kernel, layer=layer),
        grid=(batch, nc),
        in_specs=[
            pl.BlockSpec((HG_C, 2048), lambda b, c: (b * nc + c, COL_HG // 2048)),
            pl.BlockSpec((DEPTH, 512), lambda b, c: (0, 0)),
            pl.BlockSpec((1, 512), lambda b, c: (0, 0)),
        ],
        out_specs=pl.BlockSpec((HG_C, 512), lambda b, c: (b * nc + c, 0)),
        out_shape=jax.ShapeDtypeStruct((t, 512), BF16),
        scratch_shapes=[pltpu.VMEM((HG_HEADS, HG_D, HG_D), F32)],
        compiler_params=_cparams(("parallel", "arbitrary")),
        name="hgrn2",
    )(proj, lb_logits, nw)


def _outproj_kernel(h_ref, oa_ref, ob_ref, oc_ref, w_ref, o_ref):
    acc = _dot(oa_ref[...], w_ref[0:1024, :])
    acc += _dot(ob_ref[...], w_ref[1024:1536, :])
    acc += _dot(oc_ref[...], w_ref[1536:2048, :])
    o_ref[...] = h_ref[...] + acc


def _outproj(h, oa, ob, oc, w, layer):
    t = h.shape[0]
    return pl.pallas_call(
        _outproj_kernel,
        grid=(t // OUT_TM,),
        in_specs=[
            pl.BlockSpec((OUT_TM, D_MODEL), lambda i: (i, 0)),
            pl.BlockSpec((OUT_TM, 1024), lambda i: (i, 0)),
            pl.BlockSpec((OUT_TM, 512), lambda i: (i, 0)),
            pl.BlockSpec((OUT_TM, 512), lambda i: (i, 0)),
            pl.BlockSpec((None, D_MODEL, D_MODEL), lambda i: (layer, 0, 0)),
        ],
        out_specs=pl.BlockSpec((OUT_TM, D_MODEL), lambda i: (i, 0)),
        out_shape=jax.ShapeDtypeStruct((t, D_MODEL), F32),
        compiler_params=_cparams(("parallel",)),
        name="outproj",
    )(h, oa, ob, oc, w)


def _rope_tables(seq):
    inv64 = ROPE_BASE ** (-jnp.arange(0, MLA_ROPE, 2, dtype=F32) / MLA_ROPE)
    ang64 = jnp.arange(seq, dtype=F32)[:, None] * inv64[None, :]
    c, s = jnp.cos(ang64), jnp.sin(ang64)
    z = jnp.zeros_like(s)
    c4 = jnp.concatenate([c, c, c, c], axis=-1)
    sa = jnp.concatenate([-s, z, -s, z], axis=-1)
    sb = jnp.concatenate([z, s, z, s], axis=-1)
    inv128 = ROPE_BASE ** (-jnp.arange(0, RET_D, 2, dtype=F32) / RET_D)
    ang128 = jnp.arange(seq, dtype=F32)[:, None] * inv128[None, :]
    cf, sf = jnp.cos(ang128), jnp.sin(ang128)
    return c4, sa, sb, jnp.concatenate([cf, cf], axis=-1), jnp.concatenate([-sf, sf], axis=-1)


def _prep_w_in(w_in):
    d = w_in.shape[1]
    w_in = w_in.astype(BF16)
    nq = MLA_HEADS * (MLA_NOPE + MLA_ROPE)
    q = w_in[:, :, :nq].reshape(DEPTH, d, MLA_HEADS, MLA_NOPE + MLA_ROPE)
    qn = q[..., :MLA_NOPE].reshape(DEPTH, d, MLA_HEADS * MLA_NOPE)
    qr = q[..., MLA_NOPE:].reshape(DEPTH, d, MLA_HEADS * MLA_ROPE)
    ckv = w_in[:, :, nq:nq + KV_RANK]
    kr = w_in[:, :, nq + KV_RANK:nq + KV_RANK + MLA_ROPE]
    rest = w_in[:, :, nq + KV_RANK + MLA_ROPE:]
    pad = jnp.zeros((DEPTH, d, D_IN_PAD - COL_KR - MLA_ROPE), w_in.dtype)
    return jnp.concatenate([qn, qr, ckv, rest, kr, pad], axis=-1)


def _prep_w_kv(w):
    w = w.reshape(DEPTH, KV_RANK, MLA_HEADS, MLA_NOPE + MLA_V)
    kn = w[..., :MLA_NOPE].reshape(DEPTH, KV_RANK, MLA_HEADS * MLA_NOPE)
    v = w[..., MLA_NOPE:].reshape(DEPTH, KV_RANK, MLA_HEADS * MLA_V)
    return jnp.concatenate([kn, v], axis=-1).astype(BF16)


def kernel(x, ffn1_norm, ffn1_w1, ffn1_w3, ffn1_w2, mix_norm, w_in, mla_kv_norm, mla_w_kv_b,
           mla_out_norm, ret_gn, hgrn_lb_logits, hgrn_out_norm, w_o, ffn2_norm, ffn2_w1,
           ffn2_w3, ffn2_w2, final_norm):
    batch, seq, d = x.shape
    assert d == D_MODEL and seq % ATT_TQ == 0 and seq % ATT_TK == 0 and seq % RET_C == 0
    assert ATT_TK % ATT_TQ == 0 or ATT_TQ % ATT_TK == 0
    t = batch * seq
    assert t % PROJ_TM == 0 and t % FFN_TM == 0
    c4, sa, sb, cosf, sinf = _rope_tables(seq)
    w_in_p = _prep_w_in(w_in)
    w_kv_p = _prep_w_kv(mla_w_kv_b)
    row = lambda a: a.reshape(1, -1)

    w_o_b = w_o.astype(BF16)
    h = x.reshape(t, d)
    for l in range(DEPTH):
        h = _ffn(h, row(ffn1_norm[l]), ffn1_w1, ffn1_w3, ffn1_w2, l)
        proj = _inproj(h, row(mix_norm[l]), w_in_p, l)
        kcat, vt = _kv_prep(proj, row(mla_kv_norm[l]), w_kv_p, c4, sa, sb, seq, l)
        oa = _attention(proj, c4, sa, sb, kcat, vt, mla_out_norm[l].reshape(-1, 1), batch, seq)
        ob = _retention(proj, cosf, sinf, row(ret_gn[l]), batch, seq)
        oc = _hgrn(proj, hgrn_lb_logits, row(hgrn_out_norm[l]), l, batch, seq)
        h = _outproj(h, oa, ob, oc, w_o_b, l)
        h = _ffn(h, row(ffn2_norm[l]), ffn2_w1, ffn2_w3, ffn2_w2, l,
                 final_w=row(final_norm) if l == DEPTH - 1 else None)
    return h.reshape(batch, seq, d)
```

```python
import functools
import math

import jax
import jax.numpy as jnp
from jax import lax
from jax.experimental import pallas as pl
from jax.experimental.pallas import tpu as pltpu

F32 = jnp.float32
BF16 = jnp.bfloat16

D_MODEL = 2048
DEPTH = 4
MLA_HEADS = 8
MLA_NOPE = 128
MLA_ROPE = 64
MLA_V = 128
KV_RANK = 512
RET_HEADS = 4
RET_D = 128
HG_HEADS = 4
HG_D = 128
D_FF = 5632
ROPE_BASE = 10000.0
EPS = 1e-6
MASK_VALUE = -1e30
MIN_FORGET = 1e-20

LANES = 128
SUBLANES = 8
VMEM_LIMIT = 56 * 1024 * 1024

COL_QN = 0
COL_QR = 1024
COL_CKV = 1536
COL_RET = 2048
COL_HG = 4096
COL_KR = 6144
D_IN_PAD = 6400
IN_TN = 1280

FFN_TM = 1024
FFN_TF = 512
PROJ_TM = 1024
ATT_TQ = 256
ATT_TK = 512
ATT_ONES = 16
KV_TM = ATT_TK
RET_C = 256
HG_C = 128
HG_NEAR = 4
OUT_TM = 512


def _cparams(sem):
    return pltpu.CompilerParams(dimension_semantics=sem, vmem_limit_bytes=VMEM_LIMIT)


def _rms(x, w):
    return (x * lax.rsqrt(jnp.mean(x * x, axis=-1, keepdims=True) + EPS)) * w


def _sigmoid(x):
    return 1.0 / (1.0 + jnp.exp(-x))


def _dot(a, b):
    return jnp.dot(a, b, preferred_element_type=F32)


def _dot_nt(a, b):
    return lax.dot_general(a, b, (((1,), (1,)), ((), ())), preferred_element_type=F32)


def _dot_tn(a, b):
    return lax.dot_general(a, b, (((0,), (0,)), ((), ())), preferred_element_type=F32)


def _ffn_kernel(x_ref, nw_ref, w1_ref, w3_ref, w2_ref, *rest, n_f, final):
    if final:
        fw_ref, o_ref, n_ref = rest
    else:
        o_ref, n_ref = rest
    f = pl.program_id(1)

    @pl.when(f == 0)
    def _():
        x = x_ref[...]
        n_ref[...] = _rms(x, nw_ref[...]).astype(BF16)
        o_ref[...] = x

    n = n_ref[...]
    h1 = _dot(n, w1_ref[...])
    h3 = _dot(n, w3_ref[...])
    g = (h1 * _sigmoid(h1) * h3 * 0.5).astype(BF16)
    o_ref[...] += _dot(g, w2_ref[...])

    if final:
        @pl.when(f == n_f - 1)
        def _():
            o_ref[...] = _rms(o_ref[...], fw_ref[...])


def _ffn(h, nw, w1, w3, w2, layer, final_w=None):
    t = h.shape[0]
    n_f = D_FF // FFN_TF
    final = final_w is not None
    in_specs = [
        pl.BlockSpec((FFN_TM, D_MODEL), lambda i, f: (i, 0), pipeline_mode=pl.Buffered(1)),
        pl.BlockSpec((1, D_MODEL), lambda i, f: (0, 0)),
        pl.BlockSpec((None, D_MODEL, FFN_TF), lambda i, f: (layer, 0, f)),
        pl.BlockSpec((None, D_MODEL, FFN_TF), lambda i, f: (layer, 0, f)),
        pl.BlockSpec((None, FFN_TF, D_MODEL), lambda i, f: (layer, f, 0)),
    ]
    args = [h, nw, w1, w3, w2]
    if final:
        in_specs.append(pl.BlockSpec((1, D_MODEL), lambda i, f: (0, 0)))
        args.append(final_w)
    return pl.pallas_call(
        functools.partial(_ffn_kernel, n_f=n_f, final=final),
        grid=(t // FFN_TM, n_f),
        in_specs=in_specs,
        out_specs=pl.BlockSpec((FFN_TM, D_MODEL), lambda i, f: (i, 0)),
        out_shape=jax.ShapeDtypeStruct((t, D_MODEL), F32),
        scratch_shapes=[pltpu.VMEM((FFN_TM, D_MODEL), BF16)],
        compiler_params=_cparams(("parallel", "arbitrary")),
        name="ffn_final" if final else "ffn",
    )(*args)


def _inproj_kernel(x_ref, nw_ref, w_ref, o_ref, n_ref):
    @pl.when(pl.program_id(1) == 0)
    def _():
        n_ref[...] = _rms(x_ref[...], nw_ref[...]).astype(BF16)

    o_ref[...] = _dot(n_ref[...], w_ref[...])


def _inproj(h, nw, w, layer):
    t = h.shape[0]
    return pl.pallas_call(
        _inproj_kernel,
        grid=(t // PROJ_TM, D_IN_PAD // IN_TN),
        in_specs=[
            pl.BlockSpec((PROJ_TM, D_MODEL), lambda i, j: (i, 0)),
            pl.BlockSpec((1, D_MODEL), lambda i, j: (0, 0)),
            pl.BlockSpec((None, D_MODEL, IN_TN), lambda i, j: (layer, 0, j)),
        ],
        out_specs=pl.BlockSpec((PROJ_TM, IN_TN), lambda i, j: (i, j)),
        out_shape=jax.ShapeDtypeStruct((t, D_IN_PAD), F32),
        scratch_shapes=[pltpu.VMEM((PROJ_TM, D_MODEL), BF16)],
        compiler_params=_cparams(("parallel", "arbitrary")),
        name="inproj",
    )(h, nw, w)


def _rope_pair(p, c4, sa, sb):
    return p * c4 + pltpu.roll(p, 96, 1) * sa + pltpu.roll(p, 32, 1) * sb


def _rope_full(x, c, s):
    return x * c + pltpu.roll(x, 64, 1) * s


def _kv_kernel(ckv_ref, kr_ref, nw_ref, w_ref, c4_ref, sa_ref, sb_ref, kcat_ref, vt_ref):
    n = _rms(ckv_ref[...], nw_ref[...]).astype(BF16)
    kv = _dot(n, w_ref[...])
    kr = _rope_pair(kr_ref[...], c4_ref[...], sa_ref[...], sb_ref[...])
    kr_lo = kr.astype(BF16)
    kr_hi = pltpu.roll(kr, 64, 1).astype(BF16)
    for h in range(MLA_HEADS):
        kcat_ref[:, h * 256:h * 256 + 128] = kv[:, h * 128:(h + 1) * 128].astype(BF16)
        kcat_ref[:, h * 256 + 128:(h + 1) * 256] = kr_lo if h % 2 == 0 else kr_hi
    vt_ref[...] = kv[:, MLA_HEADS * MLA_NOPE:].T.astype(BF16)


def _kv_prep(proj, nw, w, c4, sa, sb, seq, layer):
    t = proj.shape[0]
    ns = seq // KV_TM
    tab = pl.BlockSpec((KV_TM, LANES), lambda i: (i % ns, 0))
    return pl.pallas_call(
        _kv_kernel,
        grid=(t // KV_TM,),
        in_specs=[
            pl.BlockSpec((KV_TM, KV_RANK), lambda i: (i, COL_CKV // KV_RANK)),
            pl.BlockSpec((KV_TM, LANES), lambda i: (i, COL_KR // LANES)),
            pl.BlockSpec((1, KV_RANK), lambda i: (0, 0)),
            pl.BlockSpec((None, KV_RANK, 2048), lambda i: (layer, 0, 0)),
            tab, tab, tab,
        ],
        out_specs=[
            pl.BlockSpec((KV_TM, 2048), lambda i: (i, 0)),
            pl.BlockSpec((None, 1024, KV_TM), lambda i: (i, 0, 0)),
        ],
        out_shape=[
            jax.ShapeDtypeStruct((t, 2048), BF16),
            jax.ShapeDtypeStruct((t // KV_TM, 1024, KV_TM), BF16),
        ],
        compiler_params=_cparams(("parallel",)),
        name="kv_prep",
    )(proj, proj, nw, w, c4, sa, sb)


def _attn_kernel(qn_ref, qr_ref, c4_ref, sa_ref, sb_ref, k_ref, vt_ref, nw_ref, o_ref,
                 qt_s, m_s, acc_s):
    i = pl.program_id(1)
    tq, tk = ATT_TQ, ATT_TK
    scale = (MLA_NOPE + MLA_ROPE) ** -0.5

    for h in range(MLA_HEADS):
        if h % 2 == 0:
            pair = qr_ref[:, (h // 2) * LANES:(h // 2 + 1) * LANES]
            roped = _rope_pair(pair, c4_ref[...], sa_ref[...], sb_ref[...]) * scale
        qn = qn_ref[:, h * LANES:(h + 1) * LANES] * scale
        qt_s[h] = jnp.concatenate([qn, roped], axis=-1).T.astype(BF16)
    m_s[...] = jnp.full(m_s.shape, MASK_VALUE, F32)
    acc_s[...] = jnp.zeros(acc_s.shape, F32)

    def tile(j, masked, r0=0, rn=ATT_TK):
        ones = jnp.ones((ATT_ONES, rn), BF16)

        def scores(h):
            return _dot(k_ref[j, r0:r0 + rn, h * 256:(h + 1) * 256], qt_s[h])

        st_next = scores(0)
        for h in range(MLA_HEADS):
            st = st_next
            if h + 1 < MLA_HEADS:
                st_next = scores(h + 1)
            if masked:
                key = j * tk + r0 + lax.broadcasted_iota(jnp.int32, (rn, tq), 0)
                qry = i * tq + lax.broadcasted_iota(jnp.int32, (rn, tq), 1)
                st = jnp.where(key <= qry, st, MASK_VALUE)
            m_old = m_s[h]
            m_new = jnp.maximum(m_old, jnp.max(st, axis=0, keepdims=True))
            p = jnp.exp(st - m_new).astype(BF16)
            alpha = jnp.exp(m_old - m_new)
            vext = jnp.concatenate([vt_ref[j, h * LANES:(h + 1) * LANES, r0:r0 + rn], ones], axis=0)
            acc_s[h] = alpha * acc_s[h] + _dot(vext, p)
            m_s[h] = m_new

    def body(j, carry):
        tile(j, False)
        return carry

    n_full = i // 2
    lax.fori_loop(0, n_full, body, 0)

    @pl.when(i % 2 == 0)
    def _():
        tile(n_full, True, 0, tq)

    @pl.when(i % 2 == 1)
    def _():
        tile(n_full, False, 0, tq)
        tile(n_full, True, tq, tq)

    for h in range(MLA_HEADS):
        acc = acc_s[h]
        ot = acc[:MLA_V, :] / acc[MLA_V:MLA_V + 1, :]
        ot = ot * lax.rsqrt(jnp.mean(ot * ot, axis=0, keepdims=True) + EPS)
        ot = ot * nw_ref[h * LANES:(h + 1) * LANES, :]
        o_ref[:, h * LANES:(h + 1) * LANES] = ot.T.astype(BF16)


def _attention(proj, c4, sa, sb, kcat, vt, nw_col, batch, seq):
    t = proj.shape[0]
    nq = seq // ATT_TQ
    nk = seq // ATT_TK
    tab = pl.BlockSpec((ATT_TQ, LANES), lambda b, i: (i, 0))
    return pl.pallas_call(
        _attn_kernel,
        grid=(batch, nq),
        in_specs=[
            pl.BlockSpec((ATT_TQ, 1024), lambda b, i: (b * nq + i, COL_QN // 1024)),
            pl.BlockSpec((ATT_TQ, 512), lambda b, i: (b * nq + i, COL_QR // 512)),
            tab, tab, tab,
            pl.BlockSpec((nk, ATT_TK, 2048), lambda b, i: (b, 0, 0)),
            pl.BlockSpec((nk, 1024, ATT_TK), lambda b, i: (b, 0, 0)),
            pl.BlockSpec((1024, 1), lambda b, i: (0, 0)),
        ],
        out_specs=pl.BlockSpec((ATT_TQ, 1024), lambda b, i: (b * nq + i, 0)),
        out_shape=jax.ShapeDtypeStruct((t, 1024), BF16),
        scratch_shapes=[
            pltpu.VMEM((MLA_HEADS, 256, ATT_TQ), BF16),
            pltpu.VMEM((MLA_HEADS, 1, ATT_TQ), F32),
            pltpu.VMEM((MLA_HEADS, MLA_V + ATT_ONES, ATT_TQ), F32),
        ],
        compiler_params=_cparams(("parallel", "arbitrary")),
        name="mla_attention",
    )(proj, proj, c4, sa, sb, kcat.reshape(t // ATT_TK, ATT_TK, 2048), vt, nw_col)


def _ret_kernel(x_ref, c_ref, s_ref, gw_ref, o_ref, state_ref):
    c = RET_C

    @pl.when(pl.program_id(1) == 0)
    def _():
        state_ref[...] = jnp.zeros_like(state_ref)

    row = lax.broadcasted_iota(jnp.int32, (c, c), 0)
    col = lax.broadcasted_iota(jnp.int32, (c, c), 1)
    rel = (row - col).astype(F32)
    idx = lax.broadcasted_iota(jnp.int32, (c, 1), 0).astype(F32)
    cosv = c_ref[...]
    sinv = s_ref[...]
    for h in range(RET_HEADS):
        lg = math.log1p(-(2.0 ** (-5.0 - h)))
        sl = slice(h * RET_D, (h + 1) * RET_D)
        q = _rope_full(x_ref[:, sl], cosv, sinv)
        k = _rope_full(x_ref[:, 512 + h * RET_D:512 + (h + 1) * RET_D], cosv, sinv) * (RET_D ** -0.5)
        v = x_ref[:, 1024 + h * RET_D:1024 + (h + 1) * RET_D]
        g = x_ref[:, 1536 + h * RET_D:1536 + (h + 1) * RET_D]
        decay = jnp.where(rel >= 0, jnp.exp(lg * jnp.maximum(rel, 0.0)), 0.0)
        xi = jnp.exp(lg * (idx + 1.0))
        zeta = jnp.exp(lg * (c - 1.0 - idx))
        qb = q.astype(BF16)
        vb = v.astype(BF16)
        state = state_ref[h]
        intra = _dot_nt(qb, k.astype(BF16)) * decay
        o = _dot(intra.astype(BF16), vb) + _dot(qb, state.astype(BF16)) * xi
        state_ref[h] = state * math.exp(lg * c) + _dot_tn((k * zeta).astype(BF16), vb)
        mu = jnp.mean(o, axis=-1, keepdims=True)
        d = o - mu
        var = jnp.mean(d * d, axis=-1, keepdims=True)
        o = d * lax.rsqrt(var + EPS) * gw_ref[:, sl]
        o_ref[:, sl] = (g * _sigmoid(g) * o).astype(BF16)


def _retention(proj, cosf, sinf, gw, batch, seq):
    t = proj.shape[0]
    nc = seq // RET_C
    tab = pl.BlockSpec((RET_C, LANES), lambda b, c: (c, 0))
    return pl.pallas_call(
        _ret_kernel,
        grid=(batch, nc),
        in_specs=[
            pl.BlockSpec((RET_C, 2048), lambda b, c: (b * nc + c, COL_RET // 2048)),
            tab, tab,
            pl.BlockSpec((1, 512), lambda b, c: (0, 0)),
        ],
        out_specs=pl.BlockSpec((RET_C, 512), lambda b, c: (b * nc + c, 0)),
        out_shape=jax.ShapeDtypeStruct((t, 512), BF16),
        scratch_shapes=[pltpu.VMEM((RET_HEADS, RET_D, RET_D), F32)],
        compiler_params=_cparams(("parallel", "arbitrary")),
        name="retention",
    )(proj, cosf, sinf, gw)


def _group_roll(x, shift):
    n, w = x.shape
    return pltpu.roll(x.reshape(n // SUBLANES, SUBLANES, w), shift, 1).reshape(n, w)


def _hgrn_kernel(x_ref, lbl_ref, nw_ref, o_ref, state_ref, *, layer):
    c = HG_C

    @pl.when(pl.program_id(1) == 0)
    def _():
        state_ref[...] = jnp.zeros_like(state_ref)

    logits = lbl_ref[...]
    e = jnp.exp(logits - jnp.max(logits, axis=0, keepdims=True))
    p = e / jnp.sum(e, axis=0, keepdims=True)
    lb_all = jnp.zeros((1, HG_HEADS * HG_D), F32)
    for m in range(layer + 1):
        lb_all = lb_all + p[m:m + 1, :]
    lb_all = lb_all - p[0:1, :]

    rowv = lax.broadcasted_iota(jnp.int32, (c, LANES), 0)
    row = lax.broadcasted_iota(jnp.int32, (c, c), 0)
    col = lax.broadcasted_iota(jnp.int32, (c, c), 1)

    z = x_ref[:, 512:1024]
    ez = jnp.exp(-jnp.abs(z))
    r = 1.0 / (1.0 + ez)
    pos = z >= 0
    sig_p = jnp.where(pos, r, ez * r)
    sig_n = jnp.where(pos, ez * r, r)
    f = lb_all + (1.0 - lb_all) * sig_p
    lf = jnp.log(jnp.maximum(f, MIN_FORGET))
    kk_all = (1.0 - lb_all) * sig_n

    tri = jnp.where(col <= row, 1.0, 0.0).astype(BF16)
    lf_hi = lf.astype(BF16)
    rem = lf - lf_hi.astype(F32)
    lf_mid = rem.astype(BF16)
    lf_lo = (rem - lf_mid.astype(F32)).astype(BF16)
    b_all = _dot(tri, lf_hi) + _dot(tri, lf_mid) + _dot(tri, lf_lo)

    heads = range(HG_HEADS)
    sls = [slice(h * HG_D, (h + 1) * HG_D) for h in heads]
    qs = [x_ref[:, sls[h]] for h in heads]
    kks = [kk_all[:, sls[h]] for h in heads]
    bs = [b_all[:, sls[h]] for h in heads]
    vbs = [x_ref[:, 1024 + h * HG_D:1024 + (h + 1) * HG_D].astype(BF16) for h in heads]


    o_far = []
    for h in heads:
        state = state_ref[h]
        b_last = bs[h][c - 1:c, :]
        o_far.append(_dot_nt((qs[h] * jnp.exp(bs[h])).astype(BF16), state.astype(BF16)))
        upd = _dot_tn(vbs[h], (kks[h] * jnp.exp(b_last - bs[h])).astype(BF16))
        state_ref[h] = jnp.exp(b_last) * state + upd

    a_lvl = []
    for h in heads:
        q, kk, b = qs[h], kks[h], bs[h]
        a = None
        m = HG_NEAR
        while m < c:
            parts = []
            for blk in range(c // (2 * m)):
                lo = blk * 2 * m
                ref = b[lo + m - 1:lo + m, :]
                parts.append(b[lo:lo + 2 * m, :] - ref)
            d = parts[0] if len(parts) == 1 else jnp.concatenate(parts, axis=0)
            second = (rowv & (2 * m - 1)) >= m
            efac = jnp.exp(jnp.where(second, d, -d))
            ql = jnp.where(second, q * efac, 0.0).astype(BF16)
            kl = jnp.where(second, 0.0, kk * efac).astype(BF16)
            al = _dot_nt(ql, kl)
            if 2 * m < c:
                sft = (2 * m).bit_length() - 1
                al = jnp.where((row >> sft) == (col >> sft), al, 0.0)
            a = al if a is None else a + al
            m *= 2
        a_lvl.append(a)

    a_all = []
    for h in heads:
        q, kk, b = qs[h], kks[h], bs[h]
        a = a_lvl[h] + jnp.where(col == row, jnp.sum(q * kk, axis=-1, keepdims=True), 0.0)
        for dlt in range(1, HG_NEAR):
            ok = (rowv & (HG_NEAR - 1)) >= dlt
            diff = jnp.where(ok, b - _group_roll(b, dlt), 0.0)
            a_d = jnp.sum(q * _group_roll(kk, dlt) * jnp.exp(diff), axis=-1, keepdims=True)
            a = a + jnp.where((col == row - dlt) & ((row & (HG_NEAR - 1)) >= dlt), a_d, 0.0)
        a_all.append(a)

    for h in heads:
        o = _dot(a_all[h].astype(BF16), vbs[h]) + o_far[h]
        o = o * lax.rsqrt(jnp.mean(o * o, axis=-1, keepdims=True) + EPS)
        o = o * nw_ref[:, sls[h]]
        g = x_ref[:, 1536 + h * HG_D:1536 + (h + 1) * HG_D]
        o_ref[:, sls[h]] = (g * _sigmoid(g) * o).astype(BF16)


def _hgrn(proj, lb_logits, nw, layer, batch, seq):
    t = proj.shape[0]
    nc = seq // HG_C
    return pl.pallas_call(
        functools.partial(_hgrn_kernel, layer=layer),
        grid=(batch, nc),
        in_specs=[
            pl.BlockSpec((HG_C, 2048), lambda b, c: (b * nc + c, COL_HG // 2048)),
            pl.BlockSpec((DEPTH, 512), lambda b, c: (0, 0)),
            pl.BlockSpec((1, 512), lambda b, c: (0, 0)),
        ],
        out_specs=pl.BlockSpec((HG_C, 512), lambda b, c: (b * nc + c, 0)),
        out_shape=jax.ShapeDtypeStruct((t, 512), BF16),
        scratch_shapes=[pltpu.VMEM((HG_HEADS, HG_D, HG_D), F32)],
        compiler_params=_cparams(("parallel", "arbitrary")),
        name="hgrn2",
    )(proj, lb_logits, nw)


def _outproj_kernel(h_ref, oa_ref, ob_ref, oc_ref, w_ref, o_ref):
    acc = _dot(oa_ref[...], w_ref[0:1024, :])
    acc += _dot(ob_ref[...], w_ref[1024:1536, :])
    acc += _dot(oc_ref[...], w_ref[1536:2048, :])
    o_ref[...] = h_ref[...] + acc


def _outproj(h, oa, ob, oc, w, layer):
    t = h.shape[0]
    return pl.pallas_call(
        _outproj_kernel,
        grid=(t // OUT_TM,),
        in_specs=[
            pl.BlockSpec((OUT_TM, D_MODEL), lambda i: (i, 0)),
            pl.BlockSpec((OUT_TM, 1024), lambda i: (i, 0)),
            pl.BlockSpec((OUT_TM, 512), lambda i: (i, 0)),
            pl.BlockSpec((OUT_TM, 512), lambda i: (i, 0)),
            pl.BlockSpec((None, D_MODEL, D_MODEL), lambda i: (layer, 0, 0)),
        ],
        out_specs=pl.BlockSpec((OUT_TM, D_MODEL), lambda i: (i, 0)),
        out_shape=jax.ShapeDtypeStruct((t, D_MODEL), F32),
        compiler_params=_cparams(("parallel",)),
        name="outproj",
    )(h, oa, ob, oc, w)


def _rope_tables(seq):
    inv64 = ROPE_BASE ** (-jnp.arange(0, MLA_ROPE, 2, dtype=F32) / MLA_ROPE)
    ang64 = jnp.arange(seq, dtype=F32)[:, None] * inv64[None, :]
    c, s = jnp.cos(ang64), jnp.sin(ang64)
    z = jnp.zeros_like(s)
    c4 = jnp.concatenate([c, c, c, c], axis=-1)
    sa = jnp.concatenate([-s, z, -s, z], axis=-1)
    sb = jnp.concatenate([z, s, z, s], axis=-1)
    inv128 = ROPE_BASE ** (-jnp.arange(0, RET_D, 2, dtype=F32) / RET_D)
    ang128 = jnp.arange(seq, dtype=F32)[:, None] * inv128[None, :]
    cf, sf = jnp.cos(ang128), jnp.sin(ang128)
    return c4, sa, sb, jnp.concatenate([cf, cf], axis=-1), jnp.concatenate([-sf, sf], axis=-1)


def _prep_w_in(w_in):
    d = w_in.shape[1]
    w_in = w_in.astype(BF16)
    nq = MLA_HEADS * (MLA_NOPE + MLA_ROPE)
    q = w_in[:, :, :nq].reshape(DEPTH, d, MLA_HEADS, MLA_NOPE + MLA_ROPE)
    qn = q[..., :MLA_NOPE].reshape(DEPTH, d, MLA_HEADS * MLA_NOPE)
    qr = q[..., MLA_NOPE:].reshape(DEPTH, d, MLA_HEADS * MLA_ROPE)
    ckv = w_in[:, :, nq:nq + KV_RANK]
    kr = w_in[:, :, nq + KV_RANK:nq + KV_RANK + MLA_ROPE]
    rest = w_in[:, :, nq + KV_RANK + MLA_ROPE:]
    pad = jnp.zeros((DEPTH, d, D_IN_PAD - COL_KR - MLA_ROPE), w_in.dtype)
    return jnp.concatenate([qn, qr, ckv, rest, kr, pad], axis=-1)


def _prep_w_kv(w):
    w = w.reshape(DEPTH, KV_RANK, MLA_HEADS, MLA_NOPE + MLA_V)
    kn = w[..., :MLA_NOPE].reshape(DEPTH, KV_RANK, MLA_HEADS * MLA_NOPE)
    v = w[..., MLA_NOPE:].reshape(DEPTH, KV_RANK, MLA_HEADS * MLA_V)
    return jnp.concatenate([kn, v], axis=-1).astype(BF16)


def kernel(x, ffn1_norm, ffn1_w1, ffn1_w3, ffn1_w2, mix_norm, w_in, mla_kv_norm, mla_w_kv_b,
           mla_out_norm, ret_gn, hgrn_lb_logits, hgrn_out_norm, w_o, ffn2_norm, ffn2_w1,
           ffn2_w3, ffn2_w2, final_norm):
    batch, seq, d = x.shape
    assert d == D_MODEL and seq % ATT_TQ == 0 and seq % ATT_TK == 0 and seq % RET_C == 0
    assert ATT_TK == 2 * ATT_TQ
    t = batch * seq
    assert t % PROJ_TM == 0 and t % FFN_TM == 0
    c4, sa, sb, cosf, sinf = _rope_tables(seq)
    w_in_p = _prep_w_in(w_in)
    w_kv_p = _prep_w_kv(mla_w_kv_b)
    row = lambda a: a.reshape(1, -1)

    w_o_b = w_o.astype(BF16)
    ffn1_w1, ffn1_w3, ffn1_w2 = (w.astype(BF16) for w in (ffn1_w1, ffn1_w3, ffn1_w2))
    ffn2_w1, ffn2_w3, ffn2_w2 = (w.astype(BF16) for w in (ffn2_w1, ffn2_w3, ffn2_w2))
    h = x.reshape(t, d)
    for l in range(DEPTH):
        h = _ffn(h, row(ffn1_norm[l]), ffn1_w1, ffn1_w3, ffn1_w2, l)
        proj = _inproj(h, row(mix_norm[l]), w_in_p, l)
        kcat, vt = _kv_prep(proj, row(mla_kv_norm[l]), w_kv_p, c4, sa, sb, seq, l)
        oa = _attention(proj, c4, sa, sb, kcat, vt, mla_out_norm[l].reshape(-1, 1), batch, seq)
        ob = _retention(proj, cosf, sinf, row(ret_gn[l]), batch, seq)
        oc = _hgrn(proj, hgrn_lb_logits, row(hgrn_out_norm[l]), l, batch, seq)
        h = _outproj(h, oa, ob, oc, w_o_b, l)
        h = _ffn(h, row(ffn2_norm[l]), ffn2_w1, ffn2_w3, ffn2_w2, l,
                 final_w=row(final_norm) if l == DEPTH - 1 else None)
    return h.reshape(batch, seq, d)
```

```python
import functools
import math

import jax
import jax.numpy as jnp
from jax import lax
from jax.experimental import pallas as pl
from jax.experimental.pallas import tpu as pltpu

F32 = jnp.float32
BF16 = jnp.bfloat16

D_MODEL = 2048
DEPTH = 4
MLA_HEADS = 8
MLA_NOPE = 128
MLA_ROPE = 64
MLA_V = 128
KV_RANK = 512
RET_HEADS = 4
RET_D = 128
HG_HEADS = 4
HG_D = 128
D_FF = 5632
ROPE_BASE = 10000.0
EPS = 1e-6
MASK_VALUE = -1e30
MIN_FORGET = 1e-20

LANES = 128
SUBLANES = 8
VMEM_LIMIT = 60 * 1024 * 1024

COL_QN = 0
COL_QR = 1024
COL_CKV = 1536
COL_RET = 2048
COL_HG = 4096
COL_KR = 6144
D_IN_PAD = 6400
IN_TN = 1280

FFN_TM = 1024
FFN_TF = 512
PROJ_TM = 1024
ATT_TQ = 512
ATT_TK = 512
ATT_AHEAD = 2
ATT_ONES = 16
KV_TM = ATT_TK
RET_C = 256
HG_C = 128
HG_NEAR = 4
OUT_TM = 512


def _cparams(sem):
    return pltpu.CompilerParams(dimension_semantics=sem, vmem_limit_bytes=VMEM_LIMIT)


def _rms(x, w):
    return (x * lax.rsqrt(jnp.mean(x * x, axis=-1, keepdims=True) + EPS)) * w


def _sigmoid(x):
    return 1.0 / (1.0 + jnp.exp(-x))


def _dot(a, b):
    return jnp.dot(a, b, preferred_element_type=F32)


def _dot_nt(a, b):
    return lax.dot_general(a, b, (((1,), (1,)), ((), ())), preferred_element_type=F32)


def _dot_tn(a, b):
    return lax.dot_general(a, b, (((0,), (0,)), ((), ())), preferred_element_type=F32)


def _ffn_kernel(x_ref, nw_ref, w1_ref, w3_ref, w2_ref, *rest, n_f, final):
    if final:
        fw_ref, o_ref, n_ref = rest
    else:
        o_ref, n_ref = rest
    f = pl.program_id(1)

    @pl.when(f == 0)
    def _():
        x = x_ref[...]
        n_ref[...] = _rms(x, nw_ref[...]).astype(BF16)
        o_ref[...] = x

    n = n_ref[...]
    h1 = _dot(n, w1_ref[...])
    h3 = _dot(n, w3_ref[...])
    g = (h1 * _sigmoid(h1) * h3 * 0.5).astype(BF16)
    o_ref[...] += _dot(g, w2_ref[...])

    if final:
        @pl.when(f == n_f - 1)
        def _():
            o_ref[...] = _rms(o_ref[...], fw_ref[...])


def _ffn(h, nw, w1, w3, w2, layer, final_w=None):
    t = h.shape[0]
    n_f = D_FF // FFN_TF
    final = final_w is not None
    in_specs = [
        pl.BlockSpec((FFN_TM, D_MODEL), lambda i, f: (i, 0)),
        pl.BlockSpec((1, D_MODEL), lambda i, f: (0, 0)),
        pl.BlockSpec((None, D_MODEL, FFN_TF), lambda i, f: (layer, 0, f)),
        pl.BlockSpec((None, D_MODEL, FFN_TF), lambda i, f: (layer, 0, f)),
        pl.BlockSpec((None, FFN_TF, D_MODEL), lambda i, f: (layer, f, 0)),
    ]
    args = [h, nw, w1, w3, w2]
    if final:
        in_specs.append(pl.BlockSpec((1, D_MODEL), lambda i, f: (0, 0)))
        args.append(final_w)
    return pl.pallas_call(
        functools.partial(_ffn_kernel, n_f=n_f, final=final),
        grid=(t // FFN_TM, n_f),
        in_specs=in_specs,
        out_specs=pl.BlockSpec((FFN_TM, D_MODEL), lambda i, f: (i, 0)),
        out_shape=jax.ShapeDtypeStruct((t, D_MODEL), F32),
        scratch_shapes=[pltpu.VMEM((FFN_TM, D_MODEL), BF16)],
        compiler_params=_cparams(("parallel", "arbitrary")),
        name="ffn_final" if final else "ffn",
    )(*args)


def _inproj_kernel(x_ref, nw_ref, w_ref, o_ref, n_ref):
    @pl.when(pl.program_id(1) == 0)
    def _():
        n_ref[...] = _rms(x_ref[...], nw_ref[...]).astype(BF16)

    o_ref[...] = _dot(n_ref[...], w_ref[...])


def _inproj(h, nw, w, layer):
    t = h.shape[0]
    return pl.pallas_call(
        _inproj_kernel,
        grid=(t // PROJ_TM, D_IN_PAD // IN_TN),
        in_specs=[
            pl.BlockSpec((PROJ_TM, D_MODEL), lambda i, j: (i, 0)),
            pl.BlockSpec((1, D_MODEL), lambda i, j: (0, 0)),
            pl.BlockSpec((None, D_MODEL, IN_TN), lambda i, j: (layer, 0, j)),
        ],
        out_specs=pl.BlockSpec((PROJ_TM, IN_TN), lambda i, j: (i, j)),
        out_shape=jax.ShapeDtypeStruct((t, D_IN_PAD), F32),
        scratch_shapes=[pltpu.VMEM((PROJ_TM, D_MODEL), BF16)],
        compiler_params=_cparams(("parallel", "arbitrary")),
        name="inproj",
    )(h, nw, w)


def _rope_pair(p, c4, sa, sb):
    return p * c4 + pltpu.roll(p, 96, 1) * sa + pltpu.roll(p, 32, 1) * sb


def _rope_full(x, c, s):
    return x * c + pltpu.roll(x, 64, 1) * s


def _kv_kernel(ckv_ref, kr_ref, nw_ref, w_ref, c4_ref, sa_ref, sb_ref, kcat_ref, vt_ref):
    n = _rms(ckv_ref[...], nw_ref[...]).astype(BF16)
    kv = _dot(n, w_ref[...])
    kr = _rope_pair(kr_ref[...], c4_ref[...], sa_ref[...], sb_ref[...])
    kr_lo = kr.astype(BF16)
    kr_hi = pltpu.roll(kr, 64, 1).astype(BF16)
    for h in range(MLA_HEADS):
        kcat_ref[:, h * 256:h * 256 + 128] = kv[:, h * 128:(h + 1) * 128].astype(BF16)
        kcat_ref[:, h * 256 + 128:(h + 1) * 256] = kr_lo if h % 2 == 0 else kr_hi
    vt_ref[...] = kv[:, MLA_HEADS * MLA_NOPE:].T.astype(BF16)


def _kv_prep(proj, nw, w, c4, sa, sb, seq, layer):
    t = proj.shape[0]
    ns = seq // KV_TM
    tab = pl.BlockSpec((KV_TM, LANES), lambda i: (i % ns, 0))
    return pl.pallas_call(
        _kv_kernel,
        grid=(t // KV_TM,),
        in_specs=[
            pl.BlockSpec((KV_TM, KV_RANK), lambda i: (i, COL_CKV // KV_RANK)),
            pl.BlockSpec((KV_TM, LANES), lambda i: (i, COL_KR // LANES)),
            pl.BlockSpec((1, KV_RANK), lambda i: (0, 0)),
            pl.BlockSpec((None, KV_RANK, 2048), lambda i: (layer, 0, 0)),
            tab, tab, tab,
        ],
        out_specs=[
            pl.BlockSpec((KV_TM, 2048), lambda i: (i, 0)),
            pl.BlockSpec((None, 1024, KV_TM), lambda i: (i, 0, 0)),
        ],
        out_shape=[
            jax.ShapeDtypeStruct((t, 2048), BF16),
            jax.ShapeDtypeStruct((t // KV_TM, 1024, KV_TM), BF16),
        ],
        compiler_params=_cparams(("parallel",)),
        name="kv_prep",
    )(proj, proj, nw, w, c4, sa, sb)


def _attn_kernel(qn_ref, qr_ref, c4_ref, sa_ref, sb_ref, k_ref, vt_ref, nw_ref, o_ref,
                 qt_s, m_s, acc_s):
    i = pl.program_id(1)
    tq, tk = ATT_TQ, ATT_TK
    scale = (MLA_NOPE + MLA_ROPE) ** -0.5

    for h in range(MLA_HEADS):
        if h % 2 == 0:
            pair = qr_ref[:, (h // 2) * LANES:(h // 2 + 1) * LANES]
            roped = _rope_pair(pair, c4_ref[...], sa_ref[...], sb_ref[...]) * scale
        qn = qn_ref[:, h * LANES:(h + 1) * LANES] * scale
        qt_s[h] = jnp.concatenate([qn, roped], axis=-1).T.astype(BF16)
    m_s[...] = jnp.full(m_s.shape, MASK_VALUE, F32)
    acc_s[...] = jnp.zeros(acc_s.shape, F32)

    def tile(j, masked):
        ones = jnp.ones((ATT_ONES, tk), BF16)

        def scores(h):
            return _dot(k_ref[j, :, h * 256:(h + 1) * 256], qt_s[h])

        pending = [scores(h) for h in range(ATT_AHEAD)]
        for h in range(MLA_HEADS):
            st = pending.pop(0)
            if h + ATT_AHEAD < MLA_HEADS:
                pending.append(scores(h + ATT_AHEAD))
            if masked:
                key = lax.broadcasted_iota(jnp.int32, (tk, tq), 0)
                qry = lax.broadcasted_iota(jnp.int32, (tk, tq), 1)
                st = jnp.where(key <= qry, st, MASK_VALUE)
            m_old = m_s[h]
            m_new = jnp.maximum(m_old, jnp.max(st, axis=0, keepdims=True))
            p = jnp.exp(st - m_new).astype(BF16)
            alpha = jnp.exp(m_old - m_new)
            vext = jnp.concatenate([vt_ref[j, h * LANES:(h + 1) * LANES, :], ones], axis=0)
            acc_s[h] = alpha * acc_s[h] + _dot(vext, p)
            m_s[h] = m_new

    def body(j, carry):
        tile(j, False)
        return carry

    lax.fori_loop(0, i, body, 0)
    tile(i, True)

    for h in range(MLA_HEADS):
        acc = acc_s[h]
        ot = acc[:MLA_V, :] / acc[MLA_V:MLA_V + 1, :]
        ot = ot * lax.rsqrt(jnp.mean(ot * ot, axis=0, keepdims=True) + EPS)
        ot = ot * nw_ref[h * LANES:(h + 1) * LANES, :]
        o_ref[:, h * LANES:(h + 1) * LANES] = ot.T.astype(BF16)


def _attention(proj, c4, sa, sb, kcat, vt, nw_col, batch, seq):
    t = proj.shape[0]
    nq = seq // ATT_TQ
    nk = seq // ATT_TK
    tab = pl.BlockSpec((ATT_TQ, LANES), lambda b, i: (i, 0))
    return pl.pallas_call(
        _attn_kernel,
        grid=(batch, nq),
        in_specs=[
            pl.BlockSpec((ATT_TQ, 1024), lambda b, i: (b * nq + i, COL_QN // 1024)),
            pl.BlockSpec((ATT_TQ, 512), lambda b, i: (b * nq + i, COL_QR // 512)),
            tab, tab, tab,
            pl.BlockSpec((nk, ATT_TK, 2048), lambda b, i: (b, 0, 0)),
            pl.BlockSpec((nk, 1024, ATT_TK), lambda b, i: (b, 0, 0)),
            pl.BlockSpec((1024, 1), lambda b, i: (0, 0)),
        ],
        out_specs=pl.BlockSpec((ATT_TQ, 1024), lambda b, i: (b * nq + i, 0)),
        out_shape=jax.ShapeDtypeStruct((t, 1024), BF16),
        scratch_shapes=[
            pltpu.VMEM((MLA_HEADS, 256, ATT_TQ), BF16),
            pltpu.VMEM((MLA_HEADS, 1, ATT_TQ), F32),
            pltpu.VMEM((MLA_HEADS, MLA_V + ATT_ONES, ATT_TQ), F32),
        ],
        compiler_params=_cparams(("parallel", "arbitrary")),
        name="mla_attention",
    )(proj, proj, c4, sa, sb, kcat.reshape(t // ATT_TK, ATT_TK, 2048), vt, nw_col)


def _ret_kernel(x_ref, c_ref, s_ref, gw_ref, o_ref, state_ref):
    c = RET_C

    @pl.when(pl.program_id(1) == 0)
    def _():
        state_ref[...] = jnp.zeros_like(state_ref)

    row = lax.broadcasted_iota(jnp.int32, (c, c), 0)
    col = lax.broadcasted_iota(jnp.int32, (c, c), 1)
    rel = (row - col).astype(F32)
    idx = lax.broadcasted_iota(jnp.int32, (c, 1), 0).astype(F32)
    cosv = c_ref[...]
    sinv = s_ref[...]
    for h in range(RET_HEADS):
        lg = math.log1p(-(2.0 ** (-5.0 - h)))
        sl = slice(h * RET_D, (h + 1) * RET_D)
        q = _rope_full(x_ref[:, sl], cosv, sinv)
        k = _rope_full(x_ref[:, 512 + h * RET_D:512 + (h + 1) * RET_D], cosv, sinv) * (RET_D ** -0.5)
        v = x_ref[:, 1024 + h * RET_D:1024 + (h + 1) * RET_D]
        g = x_ref[:, 1536 + h * RET_D:1536 + (h + 1) * RET_D]
        decay = jnp.where(rel >= 0, jnp.exp(lg * jnp.maximum(rel, 0.0)), 0.0)
        xi = jnp.exp(lg * (idx + 1.0))
        zeta = jnp.exp(lg * (c - 1.0 - idx))
        qb = q.astype(BF16)
        vb = v.astype(BF16)
        state = state_ref[h]
        intra = _dot_nt(qb, k.astype(BF16)) * decay
        o = _dot(intra.astype(BF16), vb) + _dot(qb, state.astype(BF16)) * xi
        state_ref[h] = state * math.exp(lg * c) + _dot_tn((k * zeta).astype(BF16), vb)
        mu = jnp.mean(o, axis=-1, keepdims=True)
        d = o - mu
        var = jnp.mean(d * d, axis=-1, keepdims=True)
        o = d * lax.rsqrt(var + EPS) * gw_ref[:, sl]
        o_ref[:, sl] = (g * _sigmoid(g) * o).astype(BF16)


def _retention(proj, cosf, sinf, gw, batch, seq):
    t = proj.shape[0]
    nc = seq // RET_C
    tab = pl.BlockSpec((RET_C, LANES), lambda b, c: (c, 0))
    return pl.pallas_call(
        _ret_kernel,
        grid=(batch, nc),
        in_specs=[
            pl.BlockSpec((RET_C, 2048), lambda b, c: (b * nc + c, COL_RET // 2048)),
            tab, tab,
            pl.BlockSpec((1, 512), lambda b, c: (0, 0)),
        ],
        out_specs=pl.BlockSpec((RET_C, 512), lambda b, c: (b * nc + c, 0)),
        out_shape=jax.ShapeDtypeStruct((t, 512), BF16),
        scratch_shapes=[pltpu.VMEM((RET_HEADS, RET_D, RET_D), F32)],
        compiler_params=_cparams(("parallel", "arbitrary")),
        name="retention",
    )(proj, cosf, sinf, gw)


def _group_roll(x, shift):
    n, w = x.shape
    return pltpu.roll(x.reshape(n // SUBLANES, SUBLANES, w), shift, 1).reshape(n, w)


def _hgrn_kernel(x_ref, lbl_ref, nw_ref, o_ref, state_ref, *, layer):
    c = HG_C

    @pl.when(pl.program_id(1) == 0)
    def _():
        state_ref[...] = jnp.zeros_like(state_ref)

    logits = lbl_ref[...]
    e = jnp.exp(logits - jnp.max(logits, axis=0, keepdims=True))
    p = e / jnp.sum(e, axis=0, keepdims=True)
    lb_all = jnp.zeros((1, HG_HEADS * HG_D), F32)
    for m in range(layer + 1):
        lb_all = lb_all + p[m:m + 1, :]
    lb_all = lb_all - p[0:1, :]

    rowv = lax.broadcasted_iota(jnp.int32, (c, LANES), 0)
    row = lax.broadcasted_iota(jnp.int32, (c, c), 0)
    col = lax.broadcasted_iota(jnp.int32, (c, c), 1)

    z = x_ref[:, 512:1024]
    ez = jnp.exp(-jnp.abs(z))
    r = 1.0 / (1.0 + ez)
    pos = z >= 0
    sig_p = jnp.where(pos, r, ez * r)
    sig_n = jnp.where(pos, ez * r, r)
    f = lb_all + (1.0 - lb_all) * sig_p
    lf = jnp.log(jnp.maximum(f, MIN_FORGET))
    kk_all = (1.0 - lb_all) * sig_n

    tri = jnp.where(col <= row, 1.0, 0.0).astype(BF16)
    lf_hi = lf.astype(BF16)
    rem = lf - lf_hi.astype(F32)
    lf_mid = rem.astype(BF16)
    lf_lo = (rem - lf_mid.astype(F32)).astype(BF16)
    b_all = _dot(tri, lf_hi) + _dot(tri, lf_mid) + _dot(tri, lf_lo)

    heads = range(HG_HEADS)
    sls = [slice(h * HG_D, (h + 1) * HG_D) for h in heads]
    qs = [x_ref[:, sls[h]] for h in heads]
    kks = [kk_all[:, sls[h]] for h in heads]
    bs = [b_all[:, sls[h]] for h in heads]
    vbs = [x_ref[:, 1024 + h * HG_D:1024 + (h + 1) * HG_D].astype(BF16) for h in heads]


    o_far = []
    for h in heads:
        state = state_ref[h]
        b_last = bs[h][c - 1:c, :]
        o_far.append(_dot_nt((qs[h] * jnp.exp(bs[h])).astype(BF16), state.astype(BF16)))
        upd = _dot_tn(vbs[h], (kks[h] * jnp.exp(b_last - bs[h])).astype(BF16))
        state_ref[h] = jnp.exp(b_last) * state + upd

    a_lvl = []
    for h in heads:
        q, kk, b = qs[h], kks[h], bs[h]
        a = None
        m = HG_NEAR
        while m < c:
            parts = []
            for blk in range(c // (2 * m)):
                lo = blk * 2 * m
                ref = b[lo + m - 1:lo + m, :]
                parts.append(b[lo:lo + 2 * m, :] - ref)
            d = parts[0] if len(parts) == 1 else jnp.concatenate(parts, axis=0)
            second = (rowv & (2 * m - 1)) >= m
            efac = jnp.exp(jnp.where(second, d, -d))
            ql = jnp.where(second, q * efac, 0.0).astype(BF16)
            kl = jnp.where(second, 0.0, kk * efac).astype(BF16)
            al = _dot_nt(ql, kl)
            if 2 * m < c:
                sft = (2 * m).bit_length() - 1
                al = jnp.where((row >> sft) == (col >> sft), al, 0.0)
            a = al if a is None else a + al
            m *= 2
        a_lvl.append(a)

    a_all = []
    for h in heads:
        q, kk, b = qs[h], kks[h], bs[h]
        a = a_lvl[h] + jnp.where(col == row, jnp.sum(q * kk, axis=-1, keepdims=True), 0.0)
        for dlt in range(1, HG_NEAR):
            ok = (rowv & (HG_NEAR - 1)) >= dlt
            diff = jnp.where(ok, b - _group_roll(b, dlt), 0.0)
            a_d = jnp.sum(q * _group_roll(kk, dlt) * jnp.exp(diff), axis=-1, keepdims=True)
            a = a + jnp.where((col == row - dlt) & ((row & (HG_NEAR - 1)) >= dlt), a_d, 0.0)
        a_all.append(a)

    for h in heads:
        o = _dot(a_all[h].astype(BF16), vbs[h]) + o_far[h]
        o = o * lax.rsqrt(jnp.mean(o * o, axis=-1, keepdims=True) + EPS)
        o = o * nw_ref[:, sls[h]]
        g = x_ref[:, 1536 + h * HG_D:1536 + (h + 1) * HG_D]
        o_ref[:, sls[h]] = (g * _sigmoid(g) * o).astype(BF16)


def _hgrn(proj, lb_logits, nw, layer, batch, seq):
    t = proj.shape[0]
    nc = seq // HG_C
    return pl.pallas_call(
        functools.partial(_hgrn_kernel, layer=layer),
        grid=(batch, nc),
        in_specs=[
            pl.BlockSpec((HG_C, 2048), lambda b, c: (b * nc + c, COL_HG // 2048)),
            pl.BlockSpec((DEPTH, 512), lambda b, c: (0, 0)),
            pl.BlockSpec((1, 512), lambda b, c: (0, 0)),
        ],
        out_specs=pl.BlockSpec((HG_C, 512), lambda b, c: (b * nc + c, 0)),
        out_shape=jax.ShapeDtypeStruct((t, 512), BF16),
        scratch_shapes=[pltpu.VMEM((HG_HEADS, HG_D, HG_D), F32)],
        compiler_params=_cparams(("parallel", "arbitrary")),
        name="hgrn2",
    )(proj, lb_logits, nw)


def _outproj_kernel(h_ref, oa_ref, ob_ref, oc_ref, w_ref, o_ref):
    acc = _dot(oa_ref[...], w_ref[0:1024, :])
    acc += _dot(ob_ref[...], w_ref[1024:1536, :])
    acc += _dot(oc_ref[...], w_ref[1536:2048, :])
    o_ref[...] = h_ref[...] + acc


def _outproj(h, oa, ob, oc, w, layer):
    t = h.shape[0]
    return pl.pallas_call(
        _outproj_kernel,
        grid=(t // OUT_TM,),
        in_specs=[
            pl.BlockSpec((OUT_TM, D_MODEL), lambda i: (i, 0)),
            pl.BlockSpec((OUT_TM, 1024), lambda i: (i, 0)),
            pl.BlockSpec((OUT_TM, 512), lambda i: (i, 0)),
            pl.BlockSpec((OUT_TM, 512), lambda i: (i, 0)),
            pl.BlockSpec((None, D_MODEL, D_MODEL), lambda i: (layer, 0, 0)),
        ],
        out_specs=pl.BlockSpec((OUT_TM, D_MODEL), lambda i: (i, 0)),
        out_shape=jax.ShapeDtypeStruct((t, D_MODEL), F32),
        compiler_params=_cparams(("parallel",)),
        name="outproj",
    )(h, oa, ob, oc, w)


def _rope_tables(seq):
    inv64 = ROPE_BASE ** (-jnp.arange(0, MLA_ROPE, 2, dtype=F32) / MLA_ROPE)
    ang64 = jnp.arange(seq, dtype=F32)[:, None] * inv64[None, :]
    c, s = jnp.cos(ang64), jnp.sin(ang64)
    z = jnp.zeros_like(s)
    c4 = jnp.concatenate([c, c, c, c], axis=-1)
    sa = jnp.concatenate([-s, z, -s, z], axis=-1)
    sb = jnp.concatenate([z, s, z, s], axis=-1)
    inv128 = ROPE_BASE ** (-jnp.arange(0, RET_D, 2, dtype=F32) / RET_D)
    ang128 = jnp.arange(seq, dtype=F32)[:, None] * inv128[None, :]
    cf, sf = jnp.cos(ang128), jnp.sin(ang128)
    return c4, sa, sb, jnp.concatenate([cf, cf], axis=-1), jnp.concatenate([-sf, sf], axis=-1)


def _prep_w_in(w_in):
    d = w_in.shape[1]
    w_in = w_in.astype(BF16)
    nq = MLA_HEADS * (MLA_NOPE + MLA_ROPE)
    q = w_in[:, :, :nq].reshape(DEPTH, d, MLA_HEADS, MLA_NOPE + MLA_ROPE)
    qn = q[..., :MLA_NOPE].reshape(DEPTH, d, MLA_HEADS * MLA_NOPE)
    qr = q[..., MLA_NOPE:].reshape(DEPTH, d, MLA_HEADS * MLA_ROPE)
    ckv = w_in[:, :, nq:nq + KV_RANK]
    kr = w_in[:, :, nq + KV_RANK:nq + KV_RANK + MLA_ROPE]
    rest = w_in[:, :, nq + KV_RANK + MLA_ROPE:]
    pad = jnp.zeros((DEPTH, d, D_IN_PAD - COL_KR - MLA_ROPE), w_in.dtype)
    return jnp.concatenate([qn, qr, ckv, rest, kr, pad], axis=-1)


def _prep_w_kv(w):
    w = w.reshape(DEPTH, KV_RANK, MLA_HEADS, MLA_NOPE + MLA_V)
    kn = w[..., :MLA_NOPE].reshape(DEPTH, KV_RANK, MLA_HEADS * MLA_NOPE)
    v = w[..., MLA_NOPE:].reshape(DEPTH, KV_RANK, MLA_HEADS * MLA_V)
    return jnp.concatenate([kn, v], axis=-1).astype(BF16)


def kernel(x, ffn1_norm, ffn1_w1, ffn1_w3, ffn1_w2, mix_norm, w_in, mla_kv_norm, mla_w_kv_b,
           mla_out_norm, ret_gn, hgrn_lb_logits, hgrn_out_norm, w_o, ffn2_norm, ffn2_w1,
           ffn2_w3, ffn2_w2, final_norm):
    batch, seq, d = x.shape
    assert d == D_MODEL and seq % ATT_TQ == 0 and seq % ATT_TK == 0 and seq % RET_C == 0
    assert ATT_TK == ATT_TQ
    t = batch * seq
    assert t % PROJ_TM == 0 and t % FFN_TM == 0
    c4, sa, sb, cosf, sinf = _rope_tables(seq)
    w_in_p = _prep_w_in(w_in)
    w_kv_p = _prep_w_kv(mla_w_kv_b)
    row = lambda a: a.reshape(1, -1)

    w_o_b = w_o.astype(BF16)
    ffn1_w1, ffn1_w3, ffn1_w2 = (w.astype(BF16) for w in (ffn1_w1, ffn1_w3, ffn1_w2))
    ffn2_w1, ffn2_w3, ffn2_w2 = (w.astype(BF16) for w in (ffn2_w1, ffn2_w3, ffn2_w2))
    h = x.reshape(t, d)
    for l in range(DEPTH):
        h = _ffn(h, row(ffn1_norm[l]), ffn1_w1, ffn1_w3, ffn1_w2, l)
        proj = _inproj(h, row(mix_norm[l]), w_in_p, l)
        kcat, vt = _kv_prep(proj, row(mla_kv_norm[l]), w_kv_p, c4, sa, sb, seq, l)
        oa = _attention(proj, c4, sa, sb, kcat, vt, mla_out_norm[l].reshape(-1, 1), batch, seq)
        ob = _retention(proj, cosf, sinf, row(ret_gn[l]), batch, seq)
        oc = _hgrn(proj, hgrn_lb_logits, row(hgrn_out_norm[l]), l, batch, seq)
        h = _outproj(h, oa, ob, oc, w_o_b, l)
        h = _ffn(h, row(ffn2_norm[l]), ffn2_w1, ffn2_w3, ffn2_w2, l,
                 final_w=row(final_norm) if l == DEPTH - 1 else None)
    return h.reshape(batch, seq, d)
```

```python
import functools
import math

import jax
import jax.numpy as jnp
from jax import lax
from jax.experimental import pallas as pl
from jax.experimental.pallas import tpu as pltpu

F32 = jnp.float32
BF16 = jnp.bfloat16

D_MODEL = 2048
DEPTH = 4
MLA_HEADS = 8
MLA_NOPE = 128
MLA_ROPE = 64
MLA_V = 128
KV_RANK = 512
RET_HEADS = 4
RET_D = 128
HG_HEADS = 4
HG_D = 128
D_FF = 5632
ROPE_BASE = 10000.0
EPS = 1e-6
MASK_VALUE = -1e30
MIN_FORGET = 1e-20

LANES = 128
SUBLANES = 8
VMEM_LIMIT = 60 * 1024 * 1024

COL_QN = 0
COL_QR = 1024
COL_CKV = 1536
COL_RET = 2048
COL_HG = 4096
COL_KR = 6144
D_IN_PAD = 6400
IN_TN = 1280

FFN_TM = 1024
FFN_TF = 512
PROJ_TM = 1024
ATT_TQ = 512
ATT_TK = 512
ATT_AHEAD = 2
ATT_ONES = 16
KV_TM = ATT_TK
RET_C = 256
HG_C = 128
HG_NEAR = 4
OUT_TM = 512
WIN_TR = 256


def _cparams(sem):
    return pltpu.CompilerParams(dimension_semantics=sem, vmem_limit_bytes=VMEM_LIMIT)


def _rms(x, w):
    return (x * lax.rsqrt(jnp.mean(x * x, axis=-1, keepdims=True) + EPS)) * w


def _sigmoid(x):
    return 1.0 / (1.0 + jnp.exp(-x))


def _dot(a, b):
    return jnp.dot(a, b, preferred_element_type=F32)


def _dot_nt(a, b):
    return lax.dot_general(a, b, (((1,), (1,)), ((), ())), preferred_element_type=F32)


def _dot_tn(a, b):
    return lax.dot_general(a, b, (((0,), (0,)), ((), ())), preferred_element_type=F32)


def _ffn_kernel(x_ref, nw_ref, w13_ref, w2_ref, *rest, n_f, final):
    if final:
        fw_ref, o_ref, n_ref = rest
    else:
        o_ref, n_ref = rest
    f = pl.program_id(1)

    @pl.when(f == 0)
    def _():
        x = x_ref[...]
        n_ref[...] = _rms(x, nw_ref[...]).astype(BF16)
        o_ref[...] = x

    n = n_ref[...]
    h13 = _dot(n, w13_ref[...])
    h1 = h13[:, :FFN_TF]
    h3 = h13[:, FFN_TF:]
    g = (h1 * _sigmoid(h1) * h3 * 0.5).astype(BF16)
    o_ref[...] += _dot(g, w2_ref[...])

    if final:
        @pl.when(f == n_f - 1)
        def _():
            o_ref[...] = _rms(o_ref[...], fw_ref[...])


def _pack_w13(w1, w3):
    depth, d, ff = w1.shape
    w = jnp.stack([w1.reshape(depth, d, ff // FFN_TF, FFN_TF),
                   w3.reshape(depth, d, ff // FFN_TF, FFN_TF)], axis=3)
    return w.reshape(depth, d, 2 * ff).astype(BF16)


def _ffn(h, nw, w13, w2, layer, final_w=None):
    t = h.shape[0]
    n_f = D_FF // FFN_TF
    final = final_w is not None
    in_specs = [
        pl.BlockSpec((FFN_TM, D_MODEL), lambda i, f: (i, 0)),
        pl.BlockSpec((1, D_MODEL), lambda i, f: (0, 0)),
        pl.BlockSpec((None, D_MODEL, 2 * FFN_TF), lambda i, f: (layer, 0, f)),
        pl.BlockSpec((None, FFN_TF, D_MODEL), lambda i, f: (layer, f, 0)),
    ]
    args = [h, nw, w13, w2]
    if final:
        in_specs.append(pl.BlockSpec((1, D_MODEL), lambda i, f: (0, 0)))
        args.append(final_w)
    return pl.pallas_call(
        functools.partial(_ffn_kernel, n_f=n_f, final=final),
        grid=(t // FFN_TM, n_f),
        in_specs=in_specs,
        out_specs=pl.BlockSpec((FFN_TM, D_MODEL), lambda i, f: (i, 0)),
        out_shape=jax.ShapeDtypeStruct((t, D_MODEL), F32),
        scratch_shapes=[pltpu.VMEM((FFN_TM, D_MODEL), BF16)],
        compiler_params=_cparams(("parallel", "arbitrary")),
        name="ffn_final" if final else "ffn",
    )(*args)


def _inproj_kernel(x_ref, nw_ref, w_ref, o_ref, n_ref):
    @pl.when(pl.program_id(1) == 0)
    def _():
        n_ref[...] = _rms(x_ref[...], nw_ref[...]).astype(BF16)

    o_ref[...] = _dot(n_ref[...], w_ref[...])


def _inproj(h, nw, w, layer):
    t = h.shape[0]
    return pl.pallas_call(
        _inproj_kernel,
        grid=(t // PROJ_TM, D_IN_PAD // IN_TN),
        in_specs=[
            pl.BlockSpec((PROJ_TM, D_MODEL), lambda i, j: (i, 0)),
            pl.BlockSpec((1, D_MODEL), lambda i, j: (0, 0)),
            pl.BlockSpec((None, D_MODEL, IN_TN), lambda i, j: (layer, 0, j)),
        ],
        out_specs=pl.BlockSpec((PROJ_TM, IN_TN), lambda i, j: (i, j)),
        out_shape=jax.ShapeDtypeStruct((t, D_IN_PAD), F32),
        scratch_shapes=[pltpu.VMEM((PROJ_TM, D_MODEL), BF16)],
        compiler_params=_cparams(("parallel", "arbitrary")),
        name="inproj",
    )(h, nw, w)


def _rope_pair(p, c4, sa, sb):
    return p * c4 + pltpu.roll(p, 96, 1) * sa + pltpu.roll(p, 32, 1) * sb


def _rope_full(x, c, s):
    return x * c + pltpu.roll(x, 64, 1) * s


def _kv_kernel(ckv_ref, kr_ref, nw_ref, w_ref, c4_ref, sa_ref, sb_ref, kcat_ref, vt_ref):
    n = _rms(ckv_ref[...], nw_ref[...]).astype(BF16)
    kv = _dot(n, w_ref[...])
    kr = _rope_pair(kr_ref[...], c4_ref[...], sa_ref[...], sb_ref[...])
    kr_lo = kr.astype(BF16)
    kr_hi = pltpu.roll(kr, 64, 1).astype(BF16)
    for h in range(MLA_HEADS):
        kcat_ref[:, h * 256:h * 256 + 128] = kv[:, h * 128:(h + 1) * 128].astype(BF16)
        kcat_ref[:, h * 256 + 128:(h + 1) * 256] = kr_lo if h % 2 == 0 else kr_hi
    vt_ref[...] = kv[:, MLA_HEADS * MLA_NOPE:].T.astype(BF16)


def _kv_prep(proj, nw, w, c4, sa, sb, seq, layer):
    t = proj.shape[0]
    ns = seq // KV_TM
    tab = pl.BlockSpec((KV_TM, LANES), lambda i: (i % ns, 0))
    return pl.pallas_call(
        _kv_kernel,
        grid=(t // KV_TM,),
        in_specs=[
            pl.BlockSpec((KV_TM, KV_RANK), lambda i: (i, COL_CKV // KV_RANK)),
            pl.BlockSpec((KV_TM, LANES), lambda i: (i, COL_KR // LANES)),
            pl.BlockSpec((1, KV_RANK), lambda i: (0, 0)),
            pl.BlockSpec((None, KV_RANK, 2048), lambda i: (layer, 0, 0)),
            tab, tab, tab,
        ],
        out_specs=[
            pl.BlockSpec((KV_TM, 2048), lambda i: (i, 0)),
            pl.BlockSpec((None, 1024, KV_TM), lambda i: (i, 0, 0)),
        ],
        out_shape=[
            jax.ShapeDtypeStruct((t, 2048), BF16),
            jax.ShapeDtypeStruct((t // KV_TM, 1024, KV_TM), BF16),
        ],
        compiler_params=_cparams(("parallel",)),
        name="kv_prep",
    )(proj, proj, nw, w, c4, sa, sb)


def _attn_kernel(qn_ref, qr_ref, c4_ref, sa_ref, sb_ref, k_ref, vt_ref, nw_ref, o_ref,
                 qt_s, m_s, acc_s):
    i = pl.program_id(1)
    tq, tk = ATT_TQ, ATT_TK
    scale = (MLA_NOPE + MLA_ROPE) ** -0.5

    for h in range(MLA_HEADS):
        if h % 2 == 0:
            pair = qr_ref[:, (h // 2) * LANES:(h // 2 + 1) * LANES]
            roped = _rope_pair(pair, c4_ref[...], sa_ref[...], sb_ref[...]) * scale
        qn = qn_ref[:, h * LANES:(h + 1) * LANES] * scale
        qt_s[h] = jnp.concatenate([qn, roped], axis=-1).T.astype(BF16)
    m_s[...] = jnp.full(m_s.shape, MASK_VALUE, F32)
    acc_s[...] = jnp.zeros(acc_s.shape, F32)

    def tile(j, masked):
        ones = jnp.ones((ATT_ONES, tk), BF16)

        def scores(h):
            return _dot(k_ref[j, :, h * 256:(h + 1) * 256], qt_s[h])

        pending = [scores(h) for h in range(ATT_AHEAD)]
        for h in range(MLA_HEADS):
            st = pending.pop(0)
            if h + ATT_AHEAD < MLA_HEADS:
                pending.append(scores(h + ATT_AHEAD))
            if masked:
                key = lax.broadcasted_iota(jnp.int32, (tk, tq), 0)
                qry = lax.broadcasted_iota(jnp.int32, (tk, tq), 1)
                st = jnp.where(key <= qry, st, MASK_VALUE)
            m_old = m_s[h]
            m_new = jnp.maximum(m_old, jnp.max(st, axis=0, keepdims=True))
            p = jnp.exp(st - m_new).astype(BF16)
            alpha = jnp.exp(m_old - m_new)
            vext = jnp.concatenate([vt_ref[j, h * LANES:(h + 1) * LANES, :], ones], axis=0)
            acc_s[h] = alpha * acc_s[h] + _dot(vext, p)
            m_s[h] = m_new

    def body(j, carry):
        tile(j, False)
        return carry

    lax.fori_loop(0, i, body, 0)
    tile(i, True)

    for h in range(MLA_HEADS):
        acc = acc_s[h]
        ot = acc[:MLA_V, :] / acc[MLA_V:MLA_V + 1, :]
        ot = ot * lax.rsqrt(jnp.mean(ot * ot, axis=0, keepdims=True) + EPS)
        ot = ot * nw_ref[h * LANES:(h + 1) * LANES, :]
        o_ref[:, h * LANES:(h + 1) * LANES] = ot.T.astype(BF16)


def _attention(proj, c4, sa, sb, kcat, vt, nw_col, batch, seq):
    t = proj.shape[0]
    nq = seq // ATT_TQ
    nk = seq // ATT_TK
    tab = pl.BlockSpec((ATT_TQ, LANES), lambda b, i: (i, 0))
    return pl.pallas_call(
        _attn_kernel,
        grid=(batch, nq),
        in_specs=[
            pl.BlockSpec((ATT_TQ, 1024), lambda b, i: (b * nq + i, COL_QN // 1024)),
            pl.BlockSpec((ATT_TQ, 512), lambda b, i: (b * nq + i, COL_QR // 512)),
            tab, tab, tab,
            pl.BlockSpec((nk, ATT_TK, 2048), lambda b, i: (b, 0, 0)),
            pl.BlockSpec((nk, 1024, ATT_TK), lambda b, i: (b, 0, 0)),
            pl.BlockSpec((1024, 1), lambda b, i: (0, 0)),
        ],
        out_specs=pl.BlockSpec((ATT_TQ, 1024), lambda b, i: (b * nq + i, 0)),
        out_shape=jax.ShapeDtypeStruct((t, 1024), BF16),
        scratch_shapes=[
            pltpu.VMEM((MLA_HEADS, 256, ATT_TQ), BF16),
            pltpu.VMEM((MLA_HEADS, 1, ATT_TQ), F32),
            pltpu.VMEM((MLA_HEADS, MLA_V + ATT_ONES, ATT_TQ), F32),
        ],
        compiler_params=_cparams(("parallel", "arbitrary")),
        name="mla_attention",
    )(proj, proj, c4, sa, sb, kcat.reshape(t // ATT_TK, ATT_TK, 2048), vt, nw_col)


def _ret_kernel(x_ref, c_ref, s_ref, gw_ref, o_ref, state_ref, decay_ref):
    c = RET_C
    lgs = [math.log1p(-(2.0 ** (-5.0 - h))) for h in range(RET_HEADS)]

    @pl.when(pl.program_id(1) == 0)
    def _():
        state_ref[...] = jnp.zeros_like(state_ref)
        row = lax.broadcasted_iota(jnp.int32, (c, c), 0)
        col = lax.broadcasted_iota(jnp.int32, (c, c), 1)
        rel = (row - col).astype(F32)
        for h in range(RET_HEADS):
            decay_ref[h] = jnp.where(rel >= 0, jnp.exp(lgs[h] * jnp.maximum(rel, 0.0)), 0.0)

    idx = lax.broadcasted_iota(jnp.int32, (c, 1), 0).astype(F32)
    cosv = c_ref[...]
    sinv = s_ref[...]
    heads = range(RET_HEADS)
    sls = [slice(h * RET_D, (h + 1) * RET_D) for h in heads]

    qbs, vbs, scores, far = [], [], [], []
    for h in heads:
        q = _rope_full(x_ref[:, sls[h]], cosv, sinv)
        k = _rope_full(x_ref[:, 512 + h * RET_D:512 + (h + 1) * RET_D], cosv, sinv) * (RET_D ** -0.5)
        qb = q.astype(BF16)
        vb = x_ref[:, 1024 + h * RET_D:1024 + (h + 1) * RET_D].astype(BF16)
        state = state_ref[h]
        scores.append(_dot_nt(qb, k.astype(BF16)))
        far.append(_dot(qb, state.astype(BF16)))
        zeta = jnp.exp(lgs[h] * (c - 1.0 - idx))
        state_ref[h] = state * math.exp(lgs[h] * c) + _dot_tn((k * zeta).astype(BF16), vb)
        qbs.append(qb)
        vbs.append(vb)

    outs = []
    for h in heads:
        xi = jnp.exp(lgs[h] * (idx + 1.0))
        outs.append(_dot((scores[h] * decay_ref[h]).astype(BF16), vbs[h]) + far[h] * xi)

    for h in heads:
        o = outs[h]
        mu = jnp.mean(o, axis=-1, keepdims=True)
        d = o - mu
        var = jnp.mean(d * d, axis=-1, keepdims=True)
        o = d * lax.rsqrt(var + EPS) * gw_ref[:, sls[h]]
        g = x_ref[:, 1536 + h * RET_D:1536 + (h + 1) * RET_D]
        o_ref[:, sls[h]] = (g * _sigmoid(g) * o).astype(BF16)


def _retention(proj, cosf, sinf, gw, batch, seq):
    t = proj.shape[0]
    nc = seq // RET_C
    tab = pl.BlockSpec((RET_C, LANES), lambda b, c: (c, 0))
    return pl.pallas_call(
        _ret_kernel,
        grid=(batch, nc),
        in_specs=[
            pl.BlockSpec((RET_C, 2048), lambda b, c: (b * nc + c, COL_RET // 2048)),
            tab, tab,
            pl.BlockSpec((1, 512), lambda b, c: (0, 0)),
        ],
        out_specs=pl.BlockSpec((RET_C, 512), lambda b, c: (b * nc + c, 0)),
        out_shape=jax.ShapeDtypeStruct((t, 512), BF16),
        scratch_shapes=[pltpu.VMEM((RET_HEADS, RET_D, RET_D), F32),
                        pltpu.VMEM((RET_HEADS, RET_C, RET_C), F32)],
        compiler_params=_cparams(("parallel", "arbitrary")),
        name="retention",
    )(proj, cosf, sinf, gw)


def _group_roll(x, shift):
    n, w = x.shape
    return pltpu.roll(x.reshape(n // SUBLANES, SUBLANES, w), shift, 1).reshape(n, w)


def _hgrn_kernel(x_ref, lbl_ref, nw_ref, o_ref, state_ref, *, layer):
    c = HG_C

    @pl.when(pl.program_id(1) == 0)
    def _():
        state_ref[...] = jnp.zeros_like(state_ref)

    logits = lbl_ref[...]
    e = jnp.exp(logits - jnp.max(logits, axis=0, keepdims=True))
    p = e / jnp.sum(e, axis=0, keepdims=True)
    lb_all = jnp.zeros((1, HG_HEADS * HG_D), F32)
    for m in range(layer + 1):
        lb_all = lb_all + p[m:m + 1, :]
    lb_all = lb_all - p[0:1, :]

    rowv = lax.broadcasted_iota(jnp.int32, (c, LANES), 0)
    row = lax.broadcasted_iota(jnp.int32, (c, c), 0)
    col = lax.broadcasted_iota(jnp.int32, (c, c), 1)

    z = x_ref[:, 512:1024]
    ez = jnp.exp(-jnp.abs(z))
    r = 1.0 / (1.0 + ez)
    pos = z >= 0
    sig_p = jnp.where(pos, r, ez * r)
    sig_n = jnp.where(pos, ez * r, r)
    f = lb_all + (1.0 - lb_all) * sig_p
    lf = jnp.log(jnp.maximum(f, MIN_FORGET))
    kk_all = (1.0 - lb_all) * sig_n

    tri = jnp.where(col <= row, 1.0, 0.0).astype(BF16)
    lf_hi = lf.astype(BF16)
    rem = lf - lf_hi.astype(F32)
    lf_mid = rem.astype(BF16)
    lf_lo = (rem - lf_mid.astype(F32)).astype(BF16)
    b_all = _dot(tri, lf_hi) + _dot(tri, lf_mid) + _dot(tri, lf_lo)

    heads = range(HG_HEADS)
    sls = [slice(h * HG_D, (h + 1) * HG_D) for h in heads]
    qs = [x_ref[:, sls[h]] for h in heads]
    kks = [kk_all[:, sls[h]] for h in heads]
    bs = [b_all[:, sls[h]] for h in heads]
    vbs = [x_ref[:, 1024 + h * HG_D:1024 + (h + 1) * HG_D].astype(BF16) for h in heads]


    o_far = []
    for h in heads:
        state = state_ref[h]
        b_last = bs[h][c - 1:c, :]
        o_far.append(_dot_nt((qs[h] * jnp.exp(bs[h])).astype(BF16), state.astype(BF16)))
        upd = _dot_tn(vbs[h], (kks[h] * jnp.exp(b_last - bs[h])).astype(BF16))
        state_ref[h] = jnp.exp(b_last) * state + upd

    a_lvl = []
    for h in heads:
        q, kk, b = qs[h], kks[h], bs[h]
        a = None
        m = HG_NEAR
        while m < c:
            parts = []
            for blk in range(c // (2 * m)):
                lo = blk * 2 * m
                ref = b[lo + m - 1:lo + m, :]
                parts.append(b[lo:lo + 2 * m, :] - ref)
            d = parts[0] if len(parts) == 1 else jnp.concatenate(parts, axis=0)
            second = (rowv & (2 * m - 1)) >= m
            efac = jnp.exp(jnp.where(second, d, -d))
            ql = jnp.where(second, q * efac, 0.0).astype(BF16)
            kl = jnp.where(second, 0.0, kk * efac).astype(BF16)
            al = _dot_nt(ql, kl)
            if 2 * m < c:
                sft = (2 * m).bit_length() - 1
                al = jnp.where((row >> sft) == (col >> sft), al, 0.0)
            a = al if a is None else a + al
            m *= 2
        a_lvl.append(a)

    a_all = []
    for h in heads:
        q, kk, b = qs[h], kks[h], bs[h]
        a = a_lvl[h] + jnp.where(col == row, jnp.sum(q * kk, axis=-1, keepdims=True), 0.0)
        for dlt in range(1, HG_NEAR):
            ok = (rowv & (HG_NEAR - 1)) >= dlt
            diff = jnp.where(ok, b - _group_roll(b, dlt), 0.0)
            a_d = jnp.sum(q * _group_roll(kk, dlt) * jnp.exp(diff), axis=-1, keepdims=True)
            a = a + jnp.where((col == row - dlt) & ((row & (HG_NEAR - 1)) >= dlt), a_d, 0.0)
        a_all.append(a)

    for h in heads:
        o = _dot(a_all[h].astype(BF16), vbs[h]) + o_far[h]
        o = o * lax.rsqrt(jnp.mean(o * o, axis=-1, keepdims=True) + EPS)
        o = o * nw_ref[:, sls[h]]
        g = x_ref[:, 1536 + h * HG_D:1536 + (h + 1) * HG_D]
        o_ref[:, sls[h]] = (g * _sigmoid(g) * o).astype(BF16)


def _hgrn(proj, lb_logits, nw, layer, batch, seq):
    t = proj.shape[0]
    nc = seq // HG_C
    return pl.pallas_call(
        functools.partial(_hgrn_kernel, layer=layer),
        grid=(batch, nc),
        in_specs=[
            pl.BlockSpec((HG_C, 2048), lambda b, c: (b * nc + c, COL_HG // 2048)),
            pl.BlockSpec((DEPTH, 512), lambda b, c: (0, 0)),
            pl.BlockSpec((1, 512), lambda b, c: (0, 0)),
        ],
        out_specs=pl.BlockSpec((HG_C, 512), lambda b, c: (b * nc + c, 0)),
        out_shape=jax.ShapeDtypeStruct((t, 512), BF16),
        scratch_shapes=[pltpu.VMEM((HG_HEADS, HG_D, HG_D), F32)],
        compiler_params=_cparams(("parallel", "arbitrary")),
        name="hgrn2",
    )(proj, lb_logits, nw)


def _outproj_kernel(h_ref, oa_ref, ob_ref, oc_ref, w_ref, o_ref):
    acc = _dot(oa_ref[...], w_ref[0:1024, :])
    acc += _dot(ob_ref[...], w_ref[1024:1536, :])
    acc += _dot(oc_ref[...], w_ref[1536:2048, :])
    o_ref[...] = h_ref[...] + acc


def _outproj(h, oa, ob, oc, w, layer):
    t = h.shape[0]
    return pl.pallas_call(
        _outproj_kernel,
        grid=(t // OUT_TM,),
        in_specs=[
            pl.BlockSpec((OUT_TM, D_MODEL), lambda i: (i, 0)),
            pl.BlockSpec((OUT_TM, 1024), lambda i: (i, 0)),
            pl.BlockSpec((OUT_TM, 512), lambda i: (i, 0)),
            pl.BlockSpec((OUT_TM, 512), lambda i: (i, 0)),
            pl.BlockSpec((None, D_MODEL, D_MODEL), lambda i: (layer, 0, 0)),
        ],
        out_specs=pl.BlockSpec((OUT_TM, D_MODEL), lambda i: (i, 0)),
        out_shape=jax.ShapeDtypeStruct((t, D_MODEL), F32),
        compiler_params=_cparams(("parallel",)),
        name="outproj",
    )(h, oa, ob, oc, w)


def _rope_tables(seq):
    inv64 = ROPE_BASE ** (-jnp.arange(0, MLA_ROPE, 2, dtype=F32) / MLA_ROPE)
    ang64 = jnp.arange(seq, dtype=F32)[:, None] * inv64[None, :]
    c, s = jnp.cos(ang64), jnp.sin(ang64)
    z = jnp.zeros_like(s)
    c4 = jnp.concatenate([c, c, c, c], axis=-1)
    sa = jnp.concatenate([-s, z, -s, z], axis=-1)
    sb = jnp.concatenate([z, s, z, s], axis=-1)
    inv128 = ROPE_BASE ** (-jnp.arange(0, RET_D, 2, dtype=F32) / RET_D)
    ang128 = jnp.arange(seq, dtype=F32)[:, None] * inv128[None, :]
    cf, sf = jnp.cos(ang128), jnp.sin(ang128)
    return c4, sa, sb, jnp.concatenate([cf, cf], axis=-1), jnp.concatenate([-sf, sf], axis=-1)


def _win_kernel(w_ref, o_ref):
    hd = MLA_NOPE + MLA_ROPE
    nq = MLA_HEADS * hd
    rows = o_ref.shape[0]
    for h in range(MLA_HEADS):
        o_ref[:, COL_QN + h * MLA_NOPE:COL_QN + (h + 1) * MLA_NOPE] = (
            w_ref[:, h * hd:h * hd + MLA_NOPE].astype(BF16))
    for p in range(MLA_HEADS // 2):
        pair = jnp.concatenate([w_ref[:, (2 * p) * hd + MLA_NOPE:(2 * p + 1) * hd],
                                w_ref[:, (2 * p + 1) * hd + MLA_NOPE:(2 * p + 2) * hd]], axis=-1)
        o_ref[:, COL_QR + p * LANES:COL_QR + (p + 1) * LANES] = pair.astype(BF16)
    o_ref[:, COL_CKV:COL_CKV + KV_RANK] = w_ref[:, nq:nq + KV_RANK].astype(BF16)
    src = nq + KV_RANK + MLA_ROPE
    for blk in range((COL_KR - COL_RET) // 512):
        o_ref[:, COL_RET + blk * 512:COL_RET + (blk + 1) * 512] = (
            w_ref[:, src + blk * 512:src + (blk + 1) * 512].astype(BF16))
    kr = jnp.concatenate([w_ref[:, nq + KV_RANK:nq + KV_RANK + MLA_ROPE],
                          jnp.zeros((rows, LANES - MLA_ROPE), F32)], axis=-1)
    o_ref[:, COL_KR:COL_KR + LANES] = kr.astype(BF16)
    o_ref[:, COL_KR + LANES:] = jnp.zeros((rows, D_IN_PAD - COL_KR - LANES), BF16)


def _prep_w_in(w_in):
    depth, d, n = w_in.shape
    return pl.pallas_call(
        _win_kernel,
        grid=(depth, d // WIN_TR),
        in_specs=[pl.BlockSpec((None, WIN_TR, n), lambda l, r: (l, r, 0))],
        out_specs=pl.BlockSpec((None, WIN_TR, D_IN_PAD), lambda l, r: (l, r, 0)),
        out_shape=jax.ShapeDtypeStruct((depth, d, D_IN_PAD), BF16),
        compiler_params=_cparams(("parallel", "parallel")),
        name="w_in_relayout",
    )(w_in)


def _prep_w_kv(w):
    w = w.reshape(DEPTH, KV_RANK, MLA_HEADS, MLA_NOPE + MLA_V)
    kn = w[..., :MLA_NOPE].reshape(DEPTH, KV_RANK, MLA_HEADS * MLA_NOPE)
    v = w[..., MLA_NOPE:].reshape(DEPTH, KV_RANK, MLA_HEADS * MLA_V)
    return jnp.concatenate([kn, v], axis=-1).astype(BF16)


def kernel(x, ffn1_norm, ffn1_w1, ffn1_w3, ffn1_w2, mix_norm, w_in, mla_kv_norm, mla_w_kv_b,
           mla_out_norm, ret_gn, hgrn_lb_logits, hgrn_out_norm, w_o, ffn2_norm, ffn2_w1,
           ffn2_w3, ffn2_w2, final_norm):
    batch, seq, d = x.shape
    assert d == D_MODEL and seq % ATT_TQ == 0 and seq % ATT_TK == 0 and seq % RET_C == 0
    assert ATT_TK == ATT_TQ
    t = batch * seq
    assert t % PROJ_TM == 0 and t % FFN_TM == 0
    c4, sa, sb, cosf, sinf = _rope_tables(seq)
    w_in_p = _prep_w_in(w_in)
    w_kv_p = _prep_w_kv(mla_w_kv_b)
    row = lambda a: a.reshape(1, -1)

    w_o_b = w_o.astype(BF16)
    ffn1_w13, ffn1_w2 = _pack_w13(ffn1_w1, ffn1_w3), ffn1_w2.astype(BF16)
    ffn2_w13, ffn2_w2 = _pack_w13(ffn2_w1, ffn2_w3), ffn2_w2.astype(BF16)
    h = x.reshape(t, d)
    for l in range(DEPTH):
        h = _ffn(h, row(ffn1_norm[l]), ffn1_w13, ffn1_w2, l)
        proj = _inproj(h, row(mix_norm[l]), w_in_p, l)
        kcat, vt = _kv_prep(proj, row(mla_kv_norm[l]), w_kv_p, c4, sa, sb, seq, l)
        oa = _attention(proj, c4, sa, sb, kcat, vt, mla_out_norm[l].reshape(-1, 1), batch, seq)
        ob = _retention(proj, cosf, sinf, row(ret_gn[l]), batch, seq)
        oc = _hgrn(proj, hgrn_lb_logits, row(hgrn_out_norm[l]), l, batch, seq)
        h = _outproj(h, oa, ob, oc, w_o_b, l)
        h = _ffn(h, row(ffn2_norm[l]), ffn2_w13, ffn2_w2, l,
                 final_w=row(final_norm) if l == DEPTH - 1 else None)
    return h.reshape(batch, seq, d)
```

```python
import functools
import math

import jax
import jax.numpy as jnp
from jax import lax
from jax.experimental import pallas as pl
from jax.experimental.pallas import tpu as pltpu

F32 = jnp.float32
BF16 = jnp.bfloat16

D_MODEL = 2048
DEPTH = 4
MLA_HEADS = 8
MLA_NOPE = 128
MLA_ROPE = 64
MLA_V = 128
KV_RANK = 512
RET_HEADS = 4
RET_D = 128
HG_HEADS = 4
HG_D = 128
D_FF = 5632
ROPE_BASE = 10000.0
EPS = 1e-6
MASK_VALUE = -1e30
MIN_FORGET = 1e-20

LANES = 128
SUBLANES = 8
VMEM_LIMIT = 60 * 1024 * 1024

COL_QN = 0
COL_QR = 1024
COL_CKV = 1536
COL_RET = 2048
COL_HG = 4096
COL_KR = 6144
D_IN_PAD = 6400
IN_TN = 1280

FFN_TM = 1024
FFN_TF = 512
PROJ_TM = 1024
ATT_TQ = 512
ATT_TK = 512
ATT_AHEAD = 2
ATT_ONES = 16
KV_TM = ATT_TK
RET_C = 256
HG_C = 128
HG_NEAR = 4
OUT_TM = 512
WIN_TR = 256


def _cparams(sem):
    return pltpu.CompilerParams(dimension_semantics=sem, vmem_limit_bytes=VMEM_LIMIT)


def _rms(x, w):
    return (x * lax.rsqrt(jnp.mean(x * x, axis=-1, keepdims=True) + EPS)) * w


def _sigmoid(x):
    return 1.0 / (1.0 + jnp.exp(-x))


def _dot(a, b):
    return jnp.dot(a, b, preferred_element_type=F32)


def _dot_nt(a, b):
    return lax.dot_general(a, b, (((1,), (1,)), ((), ())), preferred_element_type=F32)


def _dot_tn(a, b):
    return lax.dot_general(a, b, (((0,), (0,)), ((), ())), preferred_element_type=F32)


def _ffn_kernel(x_ref, nw_ref, w1_ref, w3_ref, w2_ref, *rest, n_f, final, convert):
    rest = list(rest)
    cast_in = [rest.pop(0) for _ in range(3)] if convert else []
    fw_ref = rest.pop(0) if final else None
    o_ref = rest.pop(0)
    cast_out = [rest.pop(0) for _ in range(3)] if convert else []
    (n_ref,) = rest
    f = pl.program_id(1)

    for src, dst in zip(cast_in, cast_out):
        dst[...] = src[...].astype(BF16)

    @pl.when(f == 0)
    def _():
        x = x_ref[...]
        n_ref[...] = _rms(x, nw_ref[...]).astype(BF16)
        o_ref[...] = x

    n = n_ref[...]
    h1 = _dot(n, w1_ref[...])
    h3 = _dot(n, w3_ref[...])
    g = (h1 * _sigmoid(h1) * h3 * 0.5).astype(BF16)
    o_ref[...] += _dot(g, w2_ref[...])

    if final:
        @pl.when(f == n_f - 1)
        def _():
            o_ref[...] = _rms(o_ref[...], fw_ref[...])


def _ffn(h, nw, w1, w3, w2, nxt=None, final_w=None):
    t = h.shape[0]
    n_i = t // FFN_TM
    n_f = D_FF // FFN_TF
    final = final_w is not None
    convert = nxt is not None
    in_specs = [
        pl.BlockSpec((FFN_TM, D_MODEL), lambda i, f: (i, 0)),
        pl.BlockSpec((1, D_MODEL), lambda i, f: (0, 0)),
        pl.BlockSpec((D_MODEL, FFN_TF), lambda i, f: (0, f)),
        pl.BlockSpec((D_MODEL, FFN_TF), lambda i, f: (0, f)),
        pl.BlockSpec((FFN_TF, D_MODEL), lambda i, f: (f, 0)),
    ]
    args = [h, nw, w1, w3, w2]
    out_specs = [pl.BlockSpec((FFN_TM, D_MODEL), lambda i, f: (i, 0))]
    out_shape = [jax.ShapeDtypeStruct((t, D_MODEL), F32)]
    if convert:
        n1, n3, n2, layer = nxt
        dr = D_MODEL // n_i
        in_specs += [
            pl.BlockSpec((None, dr, FFN_TF), lambda i, f: (layer, i, f)),
            pl.BlockSpec((None, dr, FFN_TF), lambda i, f: (layer, i, f)),
            pl.BlockSpec((None, FFN_TF, dr), lambda i, f: (layer, f, i)),
        ]
        args += [n1, n3, n2]
        out_specs += [
            pl.BlockSpec((dr, FFN_TF), lambda i, f: (i, f)),
            pl.BlockSpec((dr, FFN_TF), lambda i, f: (i, f)),
            pl.BlockSpec((FFN_TF, dr), lambda i, f: (f, i)),
        ]
        out_shape += [
            jax.ShapeDtypeStruct((D_MODEL, D_FF), BF16),
            jax.ShapeDtypeStruct((D_MODEL, D_FF), BF16),
            jax.ShapeDtypeStruct((D_FF, D_MODEL), BF16),
        ]
    if final:
        in_specs.append(pl.BlockSpec((1, D_MODEL), lambda i, f: (0, 0)))
        args.append(final_w)
    outs = pl.pallas_call(
        functools.partial(_ffn_kernel, n_f=n_f, final=final, convert=convert),
        grid=(n_i, n_f),
        in_specs=in_specs,
        out_specs=out_specs,
        out_shape=out_shape,
        scratch_shapes=[pltpu.VMEM((FFN_TM, D_MODEL), BF16)],
        compiler_params=_cparams(("parallel", "arbitrary")),
        name="ffn_final" if final else "ffn",
    )(*args)
    return outs[0], tuple(outs[1:])


def _inproj_kernel(x_ref, nw_ref, w_ref, o_ref, n_ref):
    @pl.when(pl.program_id(1) == 0)
    def _():
        n_ref[...] = _rms(x_ref[...], nw_ref[...]).astype(BF16)

    o_ref[...] = _dot(n_ref[...], w_ref[...])


def _inproj(h, nw, w, layer):
    t = h.shape[0]
    return pl.pallas_call(
        _inproj_kernel,
        grid=(t // PROJ_TM, D_IN_PAD // IN_TN),
        in_specs=[
            pl.BlockSpec((PROJ_TM, D_MODEL), lambda i, j: (i, 0)),
            pl.BlockSpec((1, D_MODEL), lambda i, j: (0, 0)),
            pl.BlockSpec((None, D_MODEL, IN_TN), lambda i, j: (layer, 0, j)),
        ],
        out_specs=pl.BlockSpec((PROJ_TM, IN_TN), lambda i, j: (i, j)),
        out_shape=jax.ShapeDtypeStruct((t, D_IN_PAD), F32),
        scratch_shapes=[pltpu.VMEM((PROJ_TM, D_MODEL), BF16)],
        compiler_params=_cparams(("parallel", "arbitrary")),
        name="inproj",
    )(h, nw, w)


def _rope_pair(p, c4, sa, sb):
    return p * c4 + pltpu.roll(p, 96, 1) * sa + pltpu.roll(p, 32, 1) * sb


def _rope_full(x, c, s):
    return x * c + pltpu.roll(x, 64, 1) * s


def _kv_kernel(ckv_ref, kr_ref, nw_ref, w_ref, c4_ref, sa_ref, sb_ref, kcat_ref, vt_ref):
    n = _rms(ckv_ref[...], nw_ref[...]).astype(BF16)
    kv = _dot(n, w_ref[...])
    kr = _rope_pair(kr_ref[...], c4_ref[...], sa_ref[...], sb_ref[...])
    kr_lo = kr.astype(BF16)
    kr_hi = pltpu.roll(kr, 64, 1).astype(BF16)
    for h in range(MLA_HEADS):
        kcat_ref[:, h * 256:h * 256 + 128] = kv[:, h * 128:(h + 1) * 128].astype(BF16)
        kcat_ref[:, h * 256 + 128:(h + 1) * 256] = kr_lo if h % 2 == 0 else kr_hi
    vt_ref[...] = kv[:, MLA_HEADS * MLA_NOPE:].T.astype(BF16)


def _kv_prep(proj, nw, w, c4, sa, sb, seq, layer):
    t = proj.shape[0]
    ns = seq // KV_TM
    tab = pl.BlockSpec((KV_TM, LANES), lambda i: (i % ns, 0))
    return pl.pallas_call(
        _kv_kernel,
        grid=(t // KV_TM,),
        in_specs=[
            pl.BlockSpec((KV_TM, KV_RANK), lambda i: (i, COL_CKV // KV_RANK)),
            pl.BlockSpec((KV_TM, LANES), lambda i: (i, COL_KR // LANES)),
            pl.BlockSpec((1, KV_RANK), lambda i: (0, 0)),
            pl.BlockSpec((None, KV_RANK, 2048), lambda i: (layer, 0, 0)),
            tab, tab, tab,
        ],
        out_specs=[
            pl.BlockSpec((KV_TM, 2048), lambda i: (i, 0)),
            pl.BlockSpec((None, 1024, KV_TM), lambda i: (i, 0, 0)),
        ],
        out_shape=[
            jax.ShapeDtypeStruct((t, 2048), BF16),
            jax.ShapeDtypeStruct((t // KV_TM, 1024, KV_TM), BF16),
        ],
        compiler_params=_cparams(("parallel",)),
        name="kv_prep",
    )(proj, proj, nw, w, c4, sa, sb)


def _attn_kernel(qn_ref, qr_ref, c4_ref, sa_ref, sb_ref, k_ref, vt_ref, nw_ref, o_ref,
                 qt_s, m_s, acc_s):
    i = pl.program_id(1)
    tq, tk = ATT_TQ, ATT_TK
    scale = (MLA_NOPE + MLA_ROPE) ** -0.5

    for h in range(MLA_HEADS):
        if h % 2 == 0:
            pair = qr_ref[:, (h // 2) * LANES:(h // 2 + 1) * LANES]
            roped = _rope_pair(pair, c4_ref[...], sa_ref[...], sb_ref[...]) * scale
        qn = qn_ref[:, h * LANES:(h + 1) * LANES] * scale
        qt_s[h] = jnp.concatenate([qn, roped], axis=-1).T.astype(BF16)
    m_s[...] = jnp.full(m_s.shape, MASK_VALUE, F32)
    acc_s[...] = jnp.zeros(acc_s.shape, F32)

    def tile(j, masked):
        ones = jnp.ones((ATT_ONES, tk), BF16)

        def scores(h):
            return _dot(k_ref[j, :, h * 256:(h + 1) * 256], qt_s[h])

        pending = [scores(h) for h in range(ATT_AHEAD)]
        for h in range(MLA_HEADS):
            st = pending.pop(0)
            if h + ATT_AHEAD < MLA_HEADS:
                pending.append(scores(h + ATT_AHEAD))
            if masked:
                key = lax.broadcasted_iota(jnp.int32, (tk, tq), 0)
                qry = lax.broadcasted_iota(jnp.int32, (tk, tq), 1)
                st = jnp.where(key <= qry, st, MASK_VALUE)
            m_old = m_s[h]
            m_new = jnp.maximum(m_old, jnp.max(st, axis=0, keepdims=True))
            p = jnp.exp(st - m_new).astype(BF16)
            alpha = jnp.exp(m_old - m_new)
            vext = jnp.concatenate([vt_ref[j, h * LANES:(h + 1) * LANES, :], ones], axis=0)
            acc_s[h] = alpha * acc_s[h] + _dot(vext, p)
            m_s[h] = m_new

    def body(j, carry):
        tile(j, False)
        return carry

    lax.fori_loop(0, i, body, 0)
    tile(i, True)

    for h in range(MLA_HEADS):
        acc = acc_s[h]
        ot = acc[:MLA_V, :] / acc[MLA_V:MLA_V + 1, :]
        ot = ot * lax.rsqrt(jnp.mean(ot * ot, axis=0, keepdims=True) + EPS)
        ot = ot * nw_ref[h * LANES:(h + 1) * LANES, :]
        o_ref[:, h * LANES:(h + 1) * LANES] = ot.T.astype(BF16)


def _attention(proj, c4, sa, sb, kcat, vt, nw_col, batch, seq):
    t = proj.shape[0]
    nq = seq // ATT_TQ
    nk = seq // ATT_TK
    tab = pl.BlockSpec((ATT_TQ, LANES), lambda b, i: (i, 0))
    return pl.pallas_call(
        _attn_kernel,
        grid=(batch, nq),
        in_specs=[
            pl.BlockSpec((ATT_TQ, 1024), lambda b, i: (b * nq + i, COL_QN // 1024)),
            pl.BlockSpec((ATT_TQ, 512), lambda b, i: (b * nq + i, COL_QR // 512)),
            tab, tab, tab,
            pl.BlockSpec((nk, ATT_TK, 2048), lambda b, i: (b, 0, 0)),
            pl.BlockSpec((nk, 1024, ATT_TK), lambda b, i: (b, 0, 0)),
            pl.BlockSpec((1024, 1), lambda b, i: (0, 0)),
        ],
        out_specs=pl.BlockSpec((ATT_TQ, 1024), lambda b, i: (b * nq + i, 0)),
        out_shape=jax.ShapeDtypeStruct((t, 1024), BF16),
        scratch_shapes=[
            pltpu.VMEM((MLA_HEADS, 256, ATT_TQ), BF16),
            pltpu.VMEM((MLA_HEADS, 1, ATT_TQ), F32),
            pltpu.VMEM((MLA_HEADS, MLA_V + ATT_ONES, ATT_TQ), F32),
        ],
        compiler_params=_cparams(("parallel", "arbitrary")),
        name="mla_attention",
    )(proj, proj, c4, sa, sb, kcat.reshape(t // ATT_TK, ATT_TK, 2048), vt, nw_col)


def _ret_kernel(x_ref, c_ref, s_ref, gw_ref, o_ref, state_ref, decay_ref):
    c = RET_C
    lgs = [math.log1p(-(2.0 ** (-5.0 - h))) for h in range(RET_HEADS)]

    @pl.when(pl.program_id(1) == 0)
    def _():
        state_ref[...] = jnp.zeros_like(state_ref)
        row = lax.broadcasted_iota(jnp.int32, (c, c), 0)
        col = lax.broadcasted_iota(jnp.int32, (c, c), 1)
        rel = (row - col).astype(F32)
        for h in range(RET_HEADS):
            decay_ref[h] = jnp.where(rel >= 0, jnp.exp(lgs[h] * jnp.maximum(rel, 0.0)), 0.0)

    idx = lax.broadcasted_iota(jnp.int32, (c, 1), 0).astype(F32)
    cosv = c_ref[...]
    sinv = s_ref[...]
    heads = range(RET_HEADS)
    sls = [slice(h * RET_D, (h + 1) * RET_D) for h in heads]

    qbs, vbs, scores, far = [], [], [], []
    for h in heads:
        q = _rope_full(x_ref[:, sls[h]], cosv, sinv)
        k = _rope_full(x_ref[:, 512 + h * RET_D:512 + (h + 1) * RET_D], cosv, sinv) * (RET_D ** -0.5)
        qb = q.astype(BF16)
        vb = x_ref[:, 1024 + h * RET_D:1024 + (h + 1) * RET_D].astype(BF16)
        state = state_ref[h]
        scores.append(_dot_nt(qb, k.astype(BF16)))
        far.append(_dot(qb, state.astype(BF16)))
        zeta = jnp.exp(lgs[h] * (c - 1.0 - idx))
        state_ref[h] = state * math.exp(lgs[h] * c) + _dot_tn((k * zeta).astype(BF16), vb)
        qbs.append(qb)
        vbs.append(vb)

    outs = []
    for h in heads:
        xi = jnp.exp(lgs[h] * (idx + 1.0))
        outs.append(_dot((scores[h] * decay_ref[h]).astype(BF16), vbs[h]) + far[h] * xi)

    for h in heads:
        o = outs[h]
        mu = jnp.mean(o, axis=-1, keepdims=True)
        d = o - mu
        var = jnp.mean(d * d, axis=-1, keepdims=True)
        o = d * lax.rsqrt(var + EPS) * gw_ref[:, sls[h]]
        g = x_ref[:, 1536 + h * RET_D:1536 + (h + 1) * RET_D]
        o_ref[:, sls[h]] = (g * _sigmoid(g) * o).astype(BF16)


def _retention(proj, cosf, sinf, gw, batch, seq):
    t = proj.shape[0]
    nc = seq // RET_C
    tab = pl.BlockSpec((RET_C, LANES), lambda b, c: (c, 0))
    return pl.pallas_call(
        _ret_kernel,
        grid=(batch, nc),
        in_specs=[
            pl.BlockSpec((RET_C, 2048), lambda b, c: (b * nc + c, COL_RET // 2048)),
            tab, tab,
            pl.BlockSpec((1, 512), lambda b, c: (0, 0)),
        ],
        out_specs=pl.BlockSpec((RET_C, 512), lambda b, c: (b * nc + c, 0)),
        out_shape=jax.ShapeDtypeStruct((t, 512), BF16),
        scratch_shapes=[pltpu.VMEM((RET_HEADS, RET_D, RET_D), F32),
                        pltpu.VMEM((RET_HEADS, RET_C, RET_C), F32)],
        compiler_params=_cparams(("parallel", "arbitrary")),
        name="retention",
    )(proj, cosf, sinf, gw)


def _group_roll(x, shift):
    n, w = x.shape
    return pltpu.roll(x.reshape(n // SUBLANES, SUBLANES, w), shift, 1).reshape(n, w)


def _hgrn_kernel(x_ref, lbl_ref, nw_ref, o_ref, state_ref, *, layer):
    c = HG_C

    @pl.when(pl.program_id(1) == 0)
    def _():
        state_ref[...] = jnp.zeros_like(state_ref)

    logits = lbl_ref[...]
    e = jnp.exp(logits - jnp.max(logits, axis=0, keepdims=True))
    p = e / jnp.sum(e, axis=0, keepdims=True)
    lb_all = jnp.zeros((1, HG_HEADS * HG_D), F32)
    for m in range(layer + 1):
        lb_all = lb_all + p[m:m + 1, :]
    lb_all = lb_all - p[0:1, :]

    rowv = lax.broadcasted_iota(jnp.int32, (c, LANES), 0)
    row = lax.broadcasted_iota(jnp.int32, (c, c), 0)
    col = lax.broadcasted_iota(jnp.int32, (c, c), 1)

    z = x_ref[:, 512:1024]
    ez = jnp.exp(-jnp.abs(z))
    r = 1.0 / (1.0 + ez)
    pos = z >= 0
    sig_p = jnp.where(pos, r, ez * r)
    sig_n = jnp.where(pos, ez * r, r)
    f = lb_all + (1.0 - lb_all) * sig_p
    lf = jnp.log(jnp.maximum(f, MIN_FORGET))
    kk_all = (1.0 - lb_all) * sig_n

    tri = jnp.where(col <= row, 1.0, 0.0).astype(BF16)
    lf_hi = lf.astype(BF16)
    rem = lf - lf_hi.astype(F32)
    lf_mid = rem.astype(BF16)
    lf_lo = (rem - lf_mid.astype(F32)).astype(BF16)
    b_all = _dot(tri, lf_hi) + _dot(tri, lf_mid) + _dot(tri, lf_lo)

    heads = range(HG_HEADS)
    sls = [slice(h * HG_D, (h + 1) * HG_D) for h in heads]
    qs = [x_ref[:, sls[h]] for h in heads]
    kks = [kk_all[:, sls[h]] for h in heads]
    bs = [b_all[:, sls[h]] for h in heads]
    vbs = [x_ref[:, 1024 + h * HG_D:1024 + (h + 1) * HG_D].astype(BF16) for h in heads]


    o_far = []
    for h in heads:
        state = state_ref[h]
        b_last = bs[h][c - 1:c, :]
        o_far.append(_dot_nt((qs[h] * jnp.exp(bs[h])).astype(BF16), state.astype(BF16)))
        upd = _dot_tn(vbs[h], (kks[h] * jnp.exp(b_last - bs[h])).astype(BF16))
        state_ref[h] = jnp.exp(b_last) * state + upd

    a_lvl = []
    for h in heads:
        q, kk, b = qs[h], kks[h], bs[h]
        a = None
        m = HG_NEAR
        while m < c:
            parts = []
            for blk in range(c // (2 * m)):
                lo = blk * 2 * m
                ref = b[lo + m - 1:lo + m, :]
                parts.append(b[lo:lo + 2 * m, :] - ref)
            d = parts[0] if len(parts) == 1 else jnp.concatenate(parts, axis=0)
            second = (rowv & (2 * m - 1)) >= m
            efac = jnp.exp(jnp.where(second, d, -d))
            ql = jnp.where(second, q * efac, 0.0).astype(BF16)
            kl = jnp.where(second, 0.0, kk * efac).astype(BF16)
            al = _dot_nt(ql, kl)
            if 2 * m < c:
                sft = (2 * m).bit_length() - 1
                al = jnp.where((row >> sft) == (col >> sft), al, 0.0)
            a = al if a is None else a + al
            m *= 2
        a_lvl.append(a)

    a_all = []
    for h in heads:
        q, kk, b = qs[h], kks[h], bs[h]
        a = a_lvl[h] + jnp.where(col == row, jnp.sum(q * kk, axis=-1, keepdims=True), 0.0)
        for dlt in range(1, HG_NEAR):
            ok = (rowv & (HG_NEAR - 1)) >= dlt
            diff = jnp.where(ok, b - _group_roll(b, dlt), 0.0)
            a_d = jnp.sum(q * _group_roll(kk, dlt) * jnp.exp(diff), axis=-1, keepdims=True)
            a = a + jnp.where((col == row - dlt) & ((row & (HG_NEAR - 1)) >= dlt), a_d, 0.0)
        a_all.append(a)

    for h in heads:
        o = _dot(a_all[h].astype(BF16), vbs[h]) + o_far[h]
        o = o * lax.rsqrt(jnp.mean(o * o, axis=-1, keepdims=True) + EPS)
        o = o * nw_ref[:, sls[h]]
        g = x_ref[:, 1536 + h * HG_D:1536 + (h + 1) * HG_D]
        o_ref[:, sls[h]] = (g * _sigmoid(g) * o).astype(BF16)


def _hgrn(proj, lb_logits, nw, layer, batch, seq):
    t = proj.shape[0]
    nc = seq // HG_C
    return pl.pallas_call(
        functools.partial(_hgrn_kernel, layer=layer),
        grid=(batch, nc),
        in_specs=[
            pl.BlockSpec((HG_C, 2048), lambda b, c: (b * nc + c, COL_HG // 2048)),
            pl.BlockSpec((DEPTH, 512), lambda b, c: (0, 0)),
            pl.BlockSpec((1, 512), lambda b, c: (0, 0)),
        ],
        out_specs=pl.BlockSpec((HG_C, 512), lambda b, c: (b * nc + c, 0)),
        out_shape=jax.ShapeDtypeStruct((t, 512), BF16),
        scratch_shapes=[pltpu.VMEM((HG_HEADS, HG_D, HG_D), F32)],
        compiler_params=_cparams(("parallel", "arbitrary")),
        name="hgrn2",
    )(proj, lb_logits, nw)


def _outproj_kernel(h_ref, oa_ref, ob_ref, oc_ref, w_ref, o_ref):
    acc = _dot(oa_ref[...], w_ref[0:1024, :])
    acc += _dot(ob_ref[...], w_ref[1024:1536, :])
    acc += _dot(oc_ref[...], w_ref[1536:2048, :])
    o_ref[...] = h_ref[...] + acc


def _outproj(h, oa, ob, oc, w, layer):
    t = h.shape[0]
    return pl.pallas_call(
        _outproj_kernel,
        grid=(t // OUT_TM,),
        in_specs=[
            pl.BlockSpec((OUT_TM, D_MODEL), lambda i: (i, 0)),
            pl.BlockSpec((OUT_TM, 1024), lambda i: (i, 0)),
            pl.BlockSpec((OUT_TM, 512), lambda i: (i, 0)),
            pl.BlockSpec((OUT_TM, 512), lambda i: (i, 0)),
            pl.BlockSpec((None, D_MODEL, D_MODEL), lambda i: (layer, 0, 0)),
        ],
        out_specs=pl.BlockSpec((OUT_TM, D_MODEL), lambda i: (i, 0)),
        out_shape=jax.ShapeDtypeStruct((t, D_MODEL), F32),
        compiler_params=_cparams(("parallel",)),
        name="outproj",
    )(h, oa, ob, oc, w)


def _rope_tables(seq):
    inv64 = ROPE_BASE ** (-jnp.arange(0, MLA_ROPE, 2, dtype=F32) / MLA_ROPE)
    ang64 = jnp.arange(seq, dtype=F32)[:, None] * inv64[None, :]
    c, s = jnp.cos(ang64), jnp.sin(ang64)
    z = jnp.zeros_like(s)
    c4 = jnp.concatenate([c, c, c, c], axis=-1)
    sa = jnp.concatenate([-s, z, -s, z], axis=-1)
    sb = jnp.concatenate([z, s, z, s], axis=-1)
    inv128 = ROPE_BASE ** (-jnp.arange(0, RET_D, 2, dtype=F32) / RET_D)
    ang128 = jnp.arange(seq, dtype=F32)[:, None] * inv128[None, :]
    cf, sf = jnp.cos(ang128), jnp.sin(ang128)
    return c4, sa, sb, jnp.concatenate([cf, cf], axis=-1), jnp.concatenate([-sf, sf], axis=-1)


def _win_kernel(w_ref, o_ref):
    hd = MLA_NOPE + MLA_ROPE
    nq = MLA_HEADS * hd
    rows = o_ref.shape[0]
    for h in range(MLA_HEADS):
        o_ref[:, COL_QN + h * MLA_NOPE:COL_QN + (h + 1) * MLA_NOPE] = (
            w_ref[:, h * hd:h * hd + MLA_NOPE].astype(BF16))
    for p in range(MLA_HEADS // 2):
        pair = jnp.concatenate([w_ref[:, (2 * p) * hd + MLA_NOPE:(2 * p + 1) * hd],
                                w_ref[:, (2 * p + 1) * hd + MLA_NOPE:(2 * p + 2) * hd]], axis=-1)
        o_ref[:, COL_QR + p * LANES:COL_QR + (p + 1) * LANES] = pair.astype(BF16)
    o_ref[:, COL_CKV:COL_CKV + KV_RANK] = w_ref[:, nq:nq + KV_RANK].astype(BF16)
    src = nq + KV_RANK + MLA_ROPE
    for blk in range((COL_KR - COL_RET) // 512):
        o_ref[:, COL_RET + blk * 512:COL_RET + (blk + 1) * 512] = (
            w_ref[:, src + blk * 512:src + (blk + 1) * 512].astype(BF16))
    kr = jnp.concatenate([w_ref[:, nq + KV_RANK:nq + KV_RANK + MLA_ROPE],
                          jnp.zeros((rows, LANES - MLA_ROPE), F32)], axis=-1)
    o_ref[:, COL_KR:COL_KR + LANES] = kr.astype(BF16)
    o_ref[:, COL_KR + LANES:] = jnp.zeros((rows, D_IN_PAD - COL_KR - LANES), BF16)


def _prep_w_in(w_in):
    depth, d, n = w_in.shape
    return pl.pallas_call(
        _win_kernel,
        grid=(depth, d // WIN_TR),
        in_specs=[pl.BlockSpec((None, WIN_TR, n), lambda l, r: (l, r, 0))],
        out_specs=pl.BlockSpec((None, WIN_TR, D_IN_PAD), lambda l, r: (l, r, 0)),
        out_shape=jax.ShapeDtypeStruct((depth, d, D_IN_PAD), BF16),
        compiler_params=_cparams(("parallel", "parallel")),
        name="w_in_relayout",
    )(w_in)


def _prep_w_kv(w):
    w = w.reshape(DEPTH, KV_RANK, MLA_HEADS, MLA_NOPE + MLA_V)
    kn = w[..., :MLA_NOPE].reshape(DEPTH, KV_RANK, MLA_HEADS * MLA_NOPE)
    v = w[..., MLA_NOPE:].reshape(DEPTH, KV_RANK, MLA_HEADS * MLA_V)
    return jnp.concatenate([kn, v], axis=-1).astype(BF16)


def kernel(x, ffn1_norm, ffn1_w1, ffn1_w3, ffn1_w2, mix_norm, w_in, mla_kv_norm, mla_w_kv_b,
           mla_out_norm, ret_gn, hgrn_lb_logits, hgrn_out_norm, w_o, ffn2_norm, ffn2_w1,
           ffn2_w3, ffn2_w2, final_norm):
    batch, seq, d = x.shape
    assert d == D_MODEL and seq % ATT_TQ == 0 and seq % ATT_TK == 0 and seq % RET_C == 0
    assert ATT_TK == ATT_TQ
    t = batch * seq
    assert t % PROJ_TM == 0 and t % FFN_TM == 0
    c4, sa, sb, cosf, sinf = _rope_tables(seq)
    w_in_p = _prep_w_in(w_in)
    w_kv_p = _prep_w_kv(mla_w_kv_b)
    row = lambda a: a.reshape(1, -1)

    w_o_b = w_o.astype(BF16)
    ffn_w = (ffn1_w1[0].astype(BF16), ffn1_w3[0].astype(BF16), ffn1_w2[0].astype(BF16))
    h = x.reshape(t, d)
    for l in range(DEPTH):
        h, ffn_w = _ffn(h, row(ffn1_norm[l]), *ffn_w, nxt=(ffn2_w1, ffn2_w3, ffn2_w2, l))
        proj = _inproj(h, row(mix_norm[l]), w_in_p, l)
        kcat, vt = _kv_prep(proj, row(mla_kv_norm[l]), w_kv_p, c4, sa, sb, seq, l)
        oa = _attention(proj, c4, sa, sb, kcat, vt, mla_out_norm[l].reshape(-1, 1), batch, seq)
        ob = _retention(proj, cosf, sinf, row(ret_gn[l]), batch, seq)
        oc = _hgrn(proj, hgrn_lb_logits, row(hgrn_out_norm[l]), l, batch, seq)
        h = _outproj(h, oa, ob, oc, w_o_b, l)
        last = l == DEPTH - 1
        h, ffn_w = _ffn(h, row(ffn2_norm[l]), *ffn_w,
                        nxt=None if last else (ffn1_w1, ffn1_w3, ffn1_w2, l + 1),
                        final_w=row(final_norm) if last else None)
    return h.reshape(batch, seq, d)
```

```python
import functools
import math

import jax
import jax.numpy as jnp
from jax import lax
from jax.experimental import pallas as pl
from jax.experimental.pallas import tpu as pltpu

F32 = jnp.float32
BF16 = jnp.bfloat16

D_MODEL = 2048
DEPTH = 4
MLA_HEADS = 8
MLA_NOPE = 128
MLA_ROPE = 64
MLA_V = 128
KV_RANK = 512
RET_HEADS = 4
RET_D = 128
HG_HEADS = 4
HG_D = 128
D_FF = 5632
ROPE_BASE = 10000.0
EPS = 1e-6
MASK_VALUE = -1e30
MIN_FORGET = 1e-20

LANES = 128
SUBLANES = 8
VMEM_LIMIT = 60 * 1024 * 1024

COL_QN = 0
COL_QR = 1024
COL_CKV = 1536
COL_RET = 2048
COL_HG = 4096
COL_KR = 6144
D_IN_PAD = 6400
IN_TN = 1280

FFN_TM = 1024
FFN_TF = 512
PROJ_TM = 1024
ATT_TQ = 512
ATT_TK = 512
ATT_AHEAD = 2
ATT_ONES = 16
KV_TM = ATT_TK
RET_C = 256
HG_C = 128
HG_NEAR = 4
OUT_TM = 512
WIN_TK = 256


def _cparams(sem):
    return pltpu.CompilerParams(dimension_semantics=sem, vmem_limit_bytes=VMEM_LIMIT)


def _rms(x, w):
    return (x * lax.rsqrt(jnp.mean(x * x, axis=-1, keepdims=True) + EPS)) * w


def _sigmoid(x):
    return 1.0 / (1.0 + jnp.exp(-x))


def _dot(a, b):
    return jnp.dot(a, b, preferred_element_type=F32)


def _dot_nt(a, b):
    return lax.dot_general(a, b, (((1,), (1,)), ((), ())), preferred_element_type=F32)


def _dot_tn(a, b):
    return lax.dot_general(a, b, (((0,), (0,)), ((), ())), preferred_element_type=F32)


def _ffn_kernel(x_ref, nw_ref, w1_ref, w3_ref, w2_ref, *rest, n_f, final, convert):
    rest = list(rest)
    cast_in = [rest.pop(0) for _ in range(3)] if convert else []
    fw_ref = rest.pop(0) if final else None
    o_ref = rest.pop(0)
    cast_out = [rest.pop(0) for _ in range(3)] if convert else []
    (n_ref,) = rest
    f = pl.program_id(1)

    for src, dst in zip(cast_in, cast_out):
        dst[...] = src[...].astype(BF16)

    @pl.when(f == 0)
    def _():
        x = x_ref[...]
        n_ref[...] = _rms(x, nw_ref[...]).astype(BF16)
        o_ref[...] = x

    n = n_ref[...]
    h1 = _dot(n, w1_ref[...])
    h3 = _dot(n, w3_ref[...])
    g = (h1 * _sigmoid(h1) * h3 * 0.5).astype(BF16)
    o_ref[...] += _dot(g, w2_ref[...])

    if final:
        @pl.when(f == n_f - 1)
        def _():
            o_ref[...] = _rms(o_ref[...], fw_ref[...])


def _ffn(h, nw, w1, w3, w2, nxt=None, final_w=None):
    t = h.shape[0]
    n_i = t // FFN_TM
    n_f = D_FF // FFN_TF
    final = final_w is not None
    convert = nxt is not None
    in_specs = [
        pl.BlockSpec((FFN_TM, D_MODEL), lambda i, f: (i, 0)),
        pl.BlockSpec((1, D_MODEL), lambda i, f: (0, 0)),
        pl.BlockSpec((D_MODEL, FFN_TF), lambda i, f: (0, f)),
        pl.BlockSpec((D_MODEL, FFN_TF), lambda i, f: (0, f)),
        pl.BlockSpec((FFN_TF, D_MODEL), lambda i, f: (f, 0)),
    ]
    args = [h, nw, w1, w3, w2]
    out_specs = [pl.BlockSpec((FFN_TM, D_MODEL), lambda i, f: (i, 0))]
    out_shape = [jax.ShapeDtypeStruct((t, D_MODEL), F32)]
    if convert:
        n1, n3, n2, layer = nxt
        dr = D_MODEL // n_i
        in_specs += [
            pl.BlockSpec((None, dr, FFN_TF), lambda i, f: (layer, i, f)),
            pl.BlockSpec((None, dr, FFN_TF), lambda i, f: (layer, i, f)),
            pl.BlockSpec((None, FFN_TF, dr), lambda i, f: (layer, f, i)),
        ]
        args += [n1, n3, n2]
        out_specs += [
            pl.BlockSpec((dr, FFN_TF), lambda i, f: (i, f)),
            pl.BlockSpec((dr, FFN_TF), lambda i, f: (i, f)),
            pl.BlockSpec((FFN_TF, dr), lambda i, f: (f, i)),
        ]
        out_shape += [
            jax.ShapeDtypeStruct((D_MODEL, D_FF), BF16),
            jax.ShapeDtypeStruct((D_MODEL, D_FF), BF16),
            jax.ShapeDtypeStruct((D_FF, D_MODEL), BF16),
        ]
    if final:
        in_specs.append(pl.BlockSpec((1, D_MODEL), lambda i, f: (0, 0)))
        args.append(final_w)
    outs = pl.pallas_call(
        functools.partial(_ffn_kernel, n_f=n_f, final=final, convert=convert),
        grid=(n_i, n_f),
        in_specs=in_specs,
        out_specs=out_specs,
        out_shape=out_shape,
        scratch_shapes=[pltpu.VMEM((FFN_TM, D_MODEL), BF16)],
        compiler_params=_cparams(("parallel", "arbitrary")),
        name="ffn_final" if final else "ffn",
    )(*args)
    return outs[0], tuple(outs[1:])


def _inproj_kernel(x_ref, nw_ref, w_ref, o_ref, n_ref):
    @pl.when(pl.program_id(1) == 0)
    def _():
        n_ref[...] = _rms(x_ref[...], nw_ref[...]).astype(BF16)

    o_ref[...] = _dot(n_ref[...], w_ref[...])


def _inproj(h, nw, w, layer):
    t = h.shape[0]
    return pl.pallas_call(
        _inproj_kernel,
        grid=(t // PROJ_TM, D_IN_PAD // IN_TN),
        in_specs=[
            pl.BlockSpec((PROJ_TM, D_MODEL), lambda i, j: (i, 0)),
            pl.BlockSpec((1, D_MODEL), lambda i, j: (0, 0)),
            pl.BlockSpec((None, D_MODEL, IN_TN), lambda i, j: (layer, 0, j)),
        ],
        out_specs=pl.BlockSpec((PROJ_TM, IN_TN), lambda i, j: (i, j)),
        out_shape=jax.ShapeDtypeStruct((t, D_IN_PAD), F32),
        scratch_shapes=[pltpu.VMEM((PROJ_TM, D_MODEL), BF16)],
        compiler_params=_cparams(("parallel", "arbitrary")),
        name="inproj",
    )(h, nw, w)


def _rope_pair(p, c4, sa, sb):
    return p * c4 + pltpu.roll(p, 96, 1) * sa + pltpu.roll(p, 32, 1) * sb


def _rope_full(x, c, s):
    return x * c + pltpu.roll(x, 64, 1) * s


def _kv_kernel(ckv_ref, kr_ref, nw_ref, w_ref, c4_ref, sa_ref, sb_ref, kcat_ref, vt_ref):
    n = _rms(ckv_ref[...], nw_ref[...]).astype(BF16)
    kv = _dot(n, w_ref[...])
    kr = _rope_pair(kr_ref[...], c4_ref[...], sa_ref[...], sb_ref[...])
    kr_lo = kr.astype(BF16)
    kr_hi = pltpu.roll(kr, 64, 1).astype(BF16)
    for h in range(MLA_HEADS):
        kcat_ref[:, h * 256:h * 256 + 128] = kv[:, h * 128:(h + 1) * 128].astype(BF16)
        kcat_ref[:, h * 256 + 128:(h + 1) * 256] = kr_lo if h % 2 == 0 else kr_hi
    vt_ref[...] = kv[:, MLA_HEADS * MLA_NOPE:].T.astype(BF16)


def _kv_prep(proj, nw, w, c4, sa, sb, seq, layer):
    t = proj.shape[0]
    ns = seq // KV_TM
    tab = pl.BlockSpec((KV_TM, LANES), lambda i: (i % ns, 0))
    return pl.pallas_call(
        _kv_kernel,
        grid=(t // KV_TM,),
        in_specs=[
            pl.BlockSpec((KV_TM, KV_RANK), lambda i: (i, COL_CKV // KV_RANK)),
            pl.BlockSpec((KV_TM, LANES), lambda i: (i, COL_KR // LANES)),
            pl.BlockSpec((1, KV_RANK), lambda i: (0, 0)),
            pl.BlockSpec((None, KV_RANK, 2048), lambda i: (layer, 0, 0)),
            tab, tab, tab,
        ],
        out_specs=[
            pl.BlockSpec((KV_TM, 2048), lambda i: (i, 0)),
            pl.BlockSpec((None, 1024, KV_TM), lambda i: (i, 0, 0)),
        ],
        out_shape=[
            jax.ShapeDtypeStruct((t, 2048), BF16),
            jax.ShapeDtypeStruct((t // KV_TM, 1024, KV_TM), BF16),
        ],
        compiler_params=_cparams(("parallel",)),
        name="kv_prep",
    )(proj, proj, nw, w, c4, sa, sb)


def _attn_kernel(qn_ref, qr_ref, c4_ref, sa_ref, sb_ref, k_ref, vt_ref, nw_ref, o_ref,
                 qt_s, m_s, acc_s):
    i = pl.program_id(1)
    tq, tk = ATT_TQ, ATT_TK
    scale = (MLA_NOPE + MLA_ROPE) ** -0.5

    for h in range(MLA_HEADS):
        if h % 2 == 0:
            pair = qr_ref[:, (h // 2) * LANES:(h // 2 + 1) * LANES]
            roped = _rope_pair(pair, c4_ref[...], sa_ref[...], sb_ref[...]) * scale
        qn = qn_ref[:, h * LANES:(h + 1) * LANES] * scale
        qt_s[h] = jnp.concatenate([qn, roped], axis=-1).T.astype(BF16)
    m_s[...] = jnp.full(m_s.shape, MASK_VALUE, F32)
    acc_s[...] = jnp.zeros(acc_s.shape, F32)

    def tile(j, masked):
        ones = jnp.ones((ATT_ONES, tk), BF16)

        def scores(h):
            return _dot(k_ref[j, :, h * 256:(h + 1) * 256], qt_s[h])

        pending = [scores(h) for h in range(ATT_AHEAD)]
        for h in range(MLA_HEADS):
            st = pending.pop(0)
            if h + ATT_AHEAD < MLA_HEADS:
                pending.append(scores(h + ATT_AHEAD))
            if masked:
                key = lax.broadcasted_iota(jnp.int32, (tk, tq), 0)
                qry = lax.broadcasted_iota(jnp.int32, (tk, tq), 1)
                st = jnp.where(key <= qry, st, MASK_VALUE)
            m_old = m_s[h]
            m_new = jnp.maximum(m_old, jnp.max(st, axis=0, keepdims=True))
            p = jnp.exp(st - m_new).astype(BF16)
            alpha = jnp.exp(m_old - m_new)
            vext = jnp.concatenate([vt_ref[j, h * LANES:(h + 1) * LANES, :], ones], axis=0)
            acc_s[h] = alpha * acc_s[h] + _dot(vext, p)
            m_s[h] = m_new

    def body(j, carry):
        tile(j, False)
        return carry

    lax.fori_loop(0, i, body, 0)
    tile(i, True)

    for h in range(MLA_HEADS):
        acc = acc_s[h]
        ot = acc[:MLA_V, :] / acc[MLA_V:MLA_V + 1, :]
        ot = ot * lax.rsqrt(jnp.mean(ot * ot, axis=0, keepdims=True) + EPS)
        ot = ot * nw_ref[h * LANES:(h + 1) * LANES, :]
        o_ref[:, h * LANES:(h + 1) * LANES] = ot.T.astype(BF16)


def _attention(proj, c4, sa, sb, kcat, vt, nw_col, batch, seq):
    t = proj.shape[0]
    nq = seq // ATT_TQ
    nk = seq // ATT_TK
    tab = pl.BlockSpec((ATT_TQ, LANES), lambda b, i: (i, 0))
    return pl.pallas_call(
        _attn_kernel,
        grid=(batch, nq),
        in_specs=[
            pl.BlockSpec((ATT_TQ, 1024), lambda b, i: (b * nq + i, COL_QN // 1024)),
            pl.BlockSpec((ATT_TQ, 512), lambda b, i: (b * nq + i, COL_QR // 512)),
            tab, tab, tab,
            pl.BlockSpec((nk, ATT_TK, 2048), lambda b, i: (b, 0, 0)),
            pl.BlockSpec((nk, 1024, ATT_TK), lambda b, i: (b, 0, 0)),
            pl.BlockSpec((1024, 1), lambda b, i: (0, 0)),
        ],
        out_specs=pl.BlockSpec((ATT_TQ, 1024), lambda b, i: (b * nq + i, 0)),
        out_shape=jax.ShapeDtypeStruct((t, 1024), BF16),
        scratch_shapes=[
            pltpu.VMEM((MLA_HEADS, 256, ATT_TQ), BF16),
            pltpu.VMEM((MLA_HEADS, 1, ATT_TQ), F32),
            pltpu.VMEM((MLA_HEADS, MLA_V + ATT_ONES, ATT_TQ), F32),
        ],
        compiler_params=_cparams(("parallel", "arbitrary")),
        name="mla_attention",
    )(proj, proj, c4, sa, sb, kcat.reshape(t // ATT_TK, ATT_TK, 2048), vt, nw_col)


def _ret_kernel(x_ref, c_ref, s_ref, gw_ref, o_ref, state_ref, decay_ref):
    c = RET_C
    lgs = [math.log1p(-(2.0 ** (-5.0 - h))) for h in range(RET_HEADS)]

    @pl.when(pl.program_id(1) == 0)
    def _():
        state_ref[...] = jnp.zeros_like(state_ref)
        row = lax.broadcasted_iota(jnp.int32, (c, c), 0)
        col = lax.broadcasted_iota(jnp.int32, (c, c), 1)
        rel = (row - col).astype(F32)
        for h in range(RET_HEADS):
            decay_ref[h] = jnp.where(rel >= 0, jnp.exp(lgs[h] * jnp.maximum(rel, 0.0)), 0.0)

    idx = lax.broadcasted_iota(jnp.int32, (c, 1), 0).astype(F32)
    cosv = c_ref[...]
    sinv = s_ref[...]
    heads = range(RET_HEADS)
    sls = [slice(h * RET_D, (h + 1) * RET_D) for h in heads]

    qbs, vbs, scores, far = [], [], [], []
    for h in heads:
        q = _rope_full(x_ref[:, sls[h]], cosv, sinv)
        k = _rope_full(x_ref[:, 512 + h * RET_D:512 + (h + 1) * RET_D], cosv, sinv) * (RET_D ** -0.5)
        qb = q.astype(BF16)
        vb = x_ref[:, 1024 + h * RET_D:1024 + (h + 1) * RET_D].astype(BF16)
        state = state_ref[h]
        scores.append(_dot_nt(qb, k.astype(BF16)))
        far.append(_dot(qb, state.astype(BF16)))
        zeta = jnp.exp(lgs[h] * (c - 1.0 - idx))
        state_ref[h] = state * math.exp(lgs[h] * c) + _dot_tn((k * zeta).astype(BF16), vb)
        qbs.append(qb)
        vbs.append(vb)

    outs = []
    for h in heads:
        xi = jnp.exp(lgs[h] * (idx + 1.0))
        outs.append(_dot((scores[h] * decay_ref[h]).astype(BF16), vbs[h]) + far[h] * xi)

    for h in heads:
        o = outs[h]
        mu = jnp.mean(o, axis=-1, keepdims=True)
        d = o - mu
        var = jnp.mean(d * d, axis=-1, keepdims=True)
        o = d * lax.rsqrt(var + EPS) * gw_ref[:, sls[h]]
        g = x_ref[:, 1536 + h * RET_D:1536 + (h + 1) * RET_D]
        o_ref[:, sls[h]] = (g * _sigmoid(g) * o).astype(BF16)


def _retention(proj, cosf, sinf, gw, batch, seq):
    t = proj.shape[0]
    nc = seq // RET_C
    tab = pl.BlockSpec((RET_C, LANES), lambda b, c: (c, 0))
    return pl.pallas_call(
        _ret_kernel,
        grid=(batch, nc),
        in_specs=[
            pl.BlockSpec((RET_C, 2048), lambda b, c: (b * nc + c, COL_RET // 2048)),
            tab, tab,
            pl.BlockSpec((1, 512), lambda b, c: (0, 0)),
        ],
        out_specs=pl.BlockSpec((RET_C, 512), lambda b, c: (b * nc + c, 0)),
        out_shape=jax.ShapeDtypeStruct((t, 512), BF16),
        scratch_shapes=[pltpu.VMEM((RET_HEADS, RET_D, RET_D), F32),
                        pltpu.VMEM((RET_HEADS, RET_C, RET_C), F32)],
        compiler_params=_cparams(("parallel", "arbitrary")),
        name="retention",
    )(proj, cosf, sinf, gw)


def _group_roll(x, shift):
    n, w = x.shape
    return pltpu.roll(x.reshape(n // SUBLANES, SUBLANES, w), shift, 1).reshape(n, w)


def _hgrn_kernel(x_ref, lbl_ref, nw_ref, o_ref, state_ref, *, layer):
    c = HG_C

    @pl.when(pl.program_id(1) == 0)
    def _():
        state_ref[...] = jnp.zeros_like(state_ref)

    logits = lbl_ref[...]
    e = jnp.exp(logits - jnp.max(logits, axis=0, keepdims=True))
    p = e / jnp.sum(e, axis=0, keepdims=True)
    lb_all = jnp.zeros((1, HG_HEADS * HG_D), F32)
    for m in range(layer + 1):
        lb_all = lb_all + p[m:m + 1, :]
    lb_all = lb_all - p[0:1, :]

    rowv = lax.broadcasted_iota(jnp.int32, (c, LANES), 0)
    row = lax.broadcasted_iota(jnp.int32, (c, c), 0)
    col = lax.broadcasted_iota(jnp.int32, (c, c), 1)

    z = x_ref[:, 512:1024]
    ez = jnp.exp(-jnp.abs(z))
    r = 1.0 / (1.0 + ez)
    pos = z >= 0
    sig_p = jnp.where(pos, r, ez * r)
    sig_n = jnp.where(pos, ez * r, r)
    f = lb_all + (1.0 - lb_all) * sig_p
    lf = jnp.log(jnp.maximum(f, MIN_FORGET))
    kk_all = (1.0 - lb_all) * sig_n

    tri = jnp.where(col <= row, 1.0, 0.0).astype(BF16)
    lf_hi = lf.astype(BF16)
    rem = lf - lf_hi.astype(F32)
    lf_mid = rem.astype(BF16)
    lf_lo = (rem - lf_mid.astype(F32)).astype(BF16)
    b_all = _dot(tri, lf_hi) + _dot(tri, lf_mid) + _dot(tri, lf_lo)

    heads = range(HG_HEADS)
    sls = [slice(h * HG_D, (h + 1) * HG_D) for h in heads]
    qs = [x_ref[:, sls[h]] for h in heads]
    kks = [kk_all[:, sls[h]] for h in heads]
    bs = [b_all[:, sls[h]] for h in heads]
    vbs = [x_ref[:, 1024 + h * HG_D:1024 + (h + 1) * HG_D].astype(BF16) for h in heads]


    o_far = []
    for h in heads:
        state = state_ref[h]
        b_last = bs[h][c - 1:c, :]
        o_far.append(_dot_nt((qs[h] * jnp.exp(bs[h])).astype(BF16), state.astype(BF16)))
        upd = _dot_tn(vbs[h], (kks[h] * jnp.exp(b_last - bs[h])).astype(BF16))
        state_ref[h] = jnp.exp(b_last) * state + upd

    levels = []
    m = HG_NEAR
    while m < c:
        sft = (2 * m).bit_length() - 1
        keep = ((row & (2 * m - 1)) >= m) & ((col & (2 * m - 1)) < m) & ((row >> sft) == (col >> sft))
        levels.append((m, (rowv & (2 * m - 1)) >= m, keep))
        m *= 2
    a_lvl = []
    for h in heads:
        q, kk, b = qs[h], kks[h], bs[h]
        a = None
        for m, second, keep in levels:
            parts = []
            for blk in range(c // (2 * m)):
                lo = blk * 2 * m
                ref = b[lo + m - 1:lo + m, :]
                if m >= SUBLANES:
                    parts += [ref - b[lo:lo + m, :], b[lo + m:lo + 2 * m, :] - ref]
                else:
                    blk_d = b[lo:lo + 2 * m, :] - ref
                    parts.append(jnp.where(second[lo:lo + 2 * m, :], blk_d, -blk_d))
            neg = jnp.concatenate(parts, axis=0)
            u = (jnp.where(second, q, kk) * jnp.exp(neg)).astype(BF16)
            al = jnp.where(keep, _dot_nt(u, u), 0.0)
            a = al if a is None else a + al
        a_lvl.append(a)

    a_all = []
    for h in heads:
        q, kk, b = qs[h], kks[h], bs[h]
        a = a_lvl[h] + jnp.where(col == row, jnp.sum(q * kk, axis=-1, keepdims=True), 0.0)
        for dlt in range(1, HG_NEAR):
            ok = (rowv & (HG_NEAR - 1)) >= dlt
            diff = jnp.where(ok, b - _group_roll(b, dlt), 0.0)
            a_d = jnp.sum(q * _group_roll(kk, dlt) * jnp.exp(diff), axis=-1, keepdims=True)
            a = a + jnp.where((col == row - dlt) & ((row & (HG_NEAR - 1)) >= dlt), a_d, 0.0)
        a_all.append(a)

    for h in heads:
        o = _dot(a_all[h].astype(BF16), vbs[h]) + o_far[h]
        o = o * lax.rsqrt(jnp.mean(o * o, axis=-1, keepdims=True) + EPS)
        o = o * nw_ref[:, sls[h]]
        g = x_ref[:, 1536 + h * HG_D:1536 + (h + 1) * HG_D]
        o_ref[:, sls[h]] = (g * _sigmoid(g) * o).astype(BF16)


def _hgrn(proj, lb_logits, nw, layer, batch, seq):
    t = proj.shape[0]
    nc = seq // HG_C
    return pl.pallas_call(
        functools.partial(_hgrn_kernel, layer=layer),
        grid=(batch, nc),
        in_specs=[
            pl.BlockSpec((HG_C, 2048), lambda b, c: (b * nc + c, COL_HG // 2048)),
            pl.BlockSpec((DEPTH, 512), lambda b, c: (0, 0)),
            pl.BlockSpec((1, 512), lambda b, c: (0, 0)),
        ],
        out_specs=pl.BlockSpec((HG_C, 512), lambda b, c: (b * nc + c, 0)),
        out_shape=jax.ShapeDtypeStruct((t, 512), BF16),
        scratch_shapes=[pltpu.VMEM((HG_HEADS, HG_D, HG_D), F32)],
        compiler_params=_cparams(("parallel", "arbitrary")),
        name="hgrn2",
    )(proj, lb_logits, nw)


def _outproj_kernel(h_ref, oa_ref, ob_ref, oc_ref, w_ref, o_ref):
    acc = _dot(oa_ref[...], w_ref[0:1024, :])
    acc += _dot(ob_ref[...], w_ref[1024:1536, :])
    acc += _dot(oc_ref[...], w_ref[1536:2048, :])
    o_ref[...] = h_ref[...] + acc


def _outproj(h, oa, ob, oc, w, layer):
    t = h.shape[0]
    return pl.pallas_call(
        _outproj_kernel,
        grid=(t // OUT_TM,),
        in_specs=[
            pl.BlockSpec((OUT_TM, D_MODEL), lambda i: (i, 0)),
            pl.BlockSpec((OUT_TM, 1024), lambda i: (i, 0)),
            pl.BlockSpec((OUT_TM, 512), lambda i: (i, 0)),
            pl.BlockSpec((OUT_TM, 512), lambda i: (i, 0)),
            pl.BlockSpec((None, D_MODEL, D_MODEL), lambda i: (layer, 0, 0)),
        ],
        out_specs=pl.BlockSpec((OUT_TM, D_MODEL), lambda i: (i, 0)),
        out_shape=jax.ShapeDtypeStruct((t, D_MODEL), F32),
        compiler_params=_cparams(("parallel",)),
        name="outproj",
    )(h, oa, ob, oc, w)


def _rope_tables(seq):
    inv64 = ROPE_BASE ** (-jnp.arange(0, MLA_ROPE, 2, dtype=F32) / MLA_ROPE)
    ang64 = jnp.arange(seq, dtype=F32)[:, None] * inv64[None, :]
    c, s = jnp.cos(ang64), jnp.sin(ang64)
    z = jnp.zeros_like(s)
    c4 = jnp.concatenate([c, c, c, c], axis=-1)
    sa = jnp.concatenate([-s, z, -s, z], axis=-1)
    sb = jnp.concatenate([z, s, z, s], axis=-1)
    inv128 = ROPE_BASE ** (-jnp.arange(0, RET_D, 2, dtype=F32) / RET_D)
    ang128 = jnp.arange(seq, dtype=F32)[:, None] * inv128[None, :]
    cf, sf = jnp.cos(ang128), jnp.sin(ang128)
    return c4, sa, sb, jnp.concatenate([cf, cf], axis=-1), jnp.concatenate([-sf, sf], axis=-1)


def _win_kernel(wt_ref, o_ref):
    hd = MLA_NOPE + MLA_ROPE
    nq = MLA_HEADS * hd
    tk = o_ref.shape[0]

    def put(col, *row_ranges):
        pieces = [wt_ref[r0:r1, :] for r0, r1 in row_ranges]
        piece = pieces[0] if len(pieces) == 1 else jnp.concatenate(pieces, axis=0)
        o_ref[:, col:col + piece.shape[0]] = piece.T.astype(BF16)

    for h in range(MLA_HEADS):
        put(COL_QN + h * MLA_NOPE, (h * hd, h * hd + MLA_NOPE))
    for p in range(MLA_HEADS // 2):
        put(COL_QR + p * LANES, ((2 * p) * hd + MLA_NOPE, (2 * p + 1) * hd),
            ((2 * p + 1) * hd + MLA_NOPE, (2 * p + 2) * hd))
    for blk in range(KV_RANK // LANES):
        put(COL_CKV + blk * LANES, (nq + blk * LANES, nq + (blk + 1) * LANES))
    src = nq + KV_RANK + MLA_ROPE
    for blk in range((COL_KR - COL_RET) // LANES):
        put(COL_RET + blk * LANES, (src + blk * LANES, src + (blk + 1) * LANES))
    kr = jnp.concatenate([wt_ref[nq + KV_RANK:nq + KV_RANK + MLA_ROPE, :],
                          jnp.zeros((LANES - MLA_ROPE, tk), F32)], axis=0)
    o_ref[:, COL_KR:COL_KR + LANES] = kr.T.astype(BF16)
    o_ref[:, COL_KR + LANES:] = jnp.zeros((tk, D_IN_PAD - COL_KR - LANES), BF16)


def _prep_w_in(w_in):
    depth, d, n = w_in.shape
    return pl.pallas_call(
        _win_kernel,
        grid=(depth, d // WIN_TK),
        in_specs=[pl.BlockSpec((None, n, WIN_TK), lambda l, r: (l, 0, r))],
        out_specs=pl.BlockSpec((None, WIN_TK, D_IN_PAD), lambda l, r: (l, r, 0)),
        out_shape=jax.ShapeDtypeStruct((depth, d, D_IN_PAD), BF16),
        compiler_params=_cparams(("parallel", "parallel")),
        name="w_in_relayout",
    )(jnp.swapaxes(w_in, 1, 2))


def _prep_w_kv(w):
    w = w.reshape(DEPTH, KV_RANK, MLA_HEADS, MLA_NOPE + MLA_V)
    kn = w[..., :MLA_NOPE].reshape(DEPTH, KV_RANK, MLA_HEADS * MLA_NOPE)
    v = w[..., MLA_NOPE:].reshape(DEPTH, KV_RANK, MLA_HEADS * MLA_V)
    return jnp.concatenate([kn, v], axis=-1).astype(BF16)


def kernel(x, ffn1_norm, ffn1_w1, ffn1_w3, ffn1_w2, mix_norm, w_in, mla_kv_norm, mla_w_kv_b,
           mla_out_norm, ret_gn, hgrn_lb_logits, hgrn_out_norm, w_o, ffn2_norm, ffn2_w1,
           ffn2_w3, ffn2_w2, final_norm):
    batch, seq, d = x.shape
    assert d == D_MODEL and seq % ATT_TQ == 0 and seq % ATT_TK == 0 and seq % RET_C == 0
    assert ATT_TK == ATT_TQ
    t = batch * seq
    assert t % PROJ_TM == 0 and t % FFN_TM == 0
    c4, sa, sb, cosf, sinf = _rope_tables(seq)
    w_in_p = _prep_w_in(w_in)
    w_kv_p = _prep_w_kv(mla_w_kv_b)
    row = lambda a: a.reshape(1, -1)

    w_o_b = w_o.astype(BF16)
    ffn_w = (ffn1_w1[0].astype(BF16), ffn1_w3[0].astype(BF16), ffn1_w2[0].astype(BF16))
    h = x.reshape(t, d)
    for l in range(DEPTH):
        h, ffn_w = _ffn(h, row(ffn1_norm[l]), *ffn_w, nxt=(ffn2_w1, ffn2_w3, ffn2_w2, l))
        proj = _inproj(h, row(mix_norm[l]), w_in_p, l)
        kcat, vt = _kv_prep(proj, row(mla_kv_norm[l]), w_kv_p, c4, sa, sb, seq, l)
        oa = _attention(proj, c4, sa, sb, kcat, vt, mla_out_norm[l].reshape(-1, 1), batch, seq)
        ob = _retention(proj, cosf, sinf, row(ret_gn[l]), batch, seq)
        oc = _hgrn(proj, hgrn_lb_logits, row(hgrn_out_norm[l]), l, batch, seq)
        h = _outproj(h, oa, ob, oc, w_o_b, l)
        last = l == DEPTH - 1
        h, ffn_w = _ffn(h, row(ffn2_norm[l]), *ffn_w,
                        nxt=None if last else (ffn1_w1, ffn1_w3, ffn1_w2, l + 1),
                        final_w=row(final_norm) if last else None)
    return h.reshape(batch, seq, d)
```

```python
import functools
import math

import jax
import jax.numpy as jnp
from jax import lax
from jax.experimental import pallas as pl
from jax.experimental.pallas import tpu as pltpu

F32 = jnp.float32
BF16 = jnp.bfloat16

D_MODEL = 2048
DEPTH = 4
MLA_HEADS = 8
MLA_NOPE = 128
MLA_ROPE = 64
MLA_V = 128
KV_RANK = 512
RET_HEADS = 4
RET_D = 128
HG_HEADS = 4
HG_D = 128
D_FF = 5632
ROPE_BASE = 10000.0
EPS = 1e-6
MASK_VALUE = -1e30
MIN_FORGET = 1e-20

LANES = 128
SUBLANES = 8
VMEM_LIMIT = 60 * 1024 * 1024

COL_QN = 0
COL_QR = 1024
COL_CKV = 1536
COL_RET = 2048
COL_HG = 4096
COL_KR = 6144
D_IN_PAD = 6400
IN_TN = 1280

FFN_TM = 1024
FFN_TF = 512
PROJ_TM = 1024
ATT_TQ = 512
ATT_TK = 512
ATT_AHEAD = 2
ATT_ONES = 16
KV_SUB = 2
KV_TM = KV_SUB * ATT_TK
RET_C = 256
HG_C = 128
HG_SUB = 2
RET_SUB = 2
HG_NEAR = 4
OUT_TM = 512
WIN_TK = 256


def _cparams(sem):
    return pltpu.CompilerParams(dimension_semantics=sem, vmem_limit_bytes=VMEM_LIMIT)


def _rms(x, w):
    return (x * lax.rsqrt(jnp.mean(x * x, axis=-1, keepdims=True) + EPS)) * w


def _sigmoid(x):
    return 1.0 / (1.0 + jnp.exp(-x))


def _dot(a, b):
    return jnp.dot(a, b, preferred_element_type=F32)


def _dot_nt(a, b):
    return lax.dot_general(a, b, (((1,), (1,)), ((), ())), preferred_element_type=F32)


def _dot_tn(a, b):
    return lax.dot_general(a, b, (((0,), (0,)), ((), ())), preferred_element_type=F32)


def _ffn_kernel(x_ref, nw_ref, w1_ref, w3_ref, w2_ref, *rest, n_f, final, convert):
    rest = list(rest)
    cast_in = [rest.pop(0) for _ in range(3)] if convert else []
    fw_ref = rest.pop(0) if final else None
    o_ref = rest.pop(0)
    cast_out = [rest.pop(0) for _ in range(3)] if convert else []
    (n_ref,) = rest
    f = pl.program_id(1)

    for src, dst in zip(cast_in, cast_out):
        dst[...] = src[...].astype(BF16)

    @pl.when(f == 0)
    def _():
        x = x_ref[...]
        n_ref[...] = _rms(x, nw_ref[...]).astype(BF16)
        o_ref[...] = x

    n = n_ref[...]
    h1 = _dot(n, w1_ref[...])
    h3 = _dot(n, w3_ref[...])
    g = (h1 * _sigmoid(h1) * h3 * 0.5).astype(BF16)
    o_ref[...] += _dot(g, w2_ref[...])

    if final:
        @pl.when(f == n_f - 1)
        def _():
            o_ref[...] = _rms(o_ref[...], fw_ref[...])


def _ffn(h, nw, w1, w3, w2, nxt=None, final_w=None):
    t = h.shape[0]
    n_i = t // FFN_TM
    n_f = D_FF // FFN_TF
    final = final_w is not None
    convert = nxt is not None
    in_specs = [
        pl.BlockSpec((FFN_TM, D_MODEL), lambda i, f: (i, 0)),
        pl.BlockSpec((1, D_MODEL), lambda i, f: (0, 0)),
        pl.BlockSpec((D_MODEL, FFN_TF), lambda i, f: (0, f)),
        pl.BlockSpec((D_MODEL, FFN_TF), lambda i, f: (0, f)),
        pl.BlockSpec((FFN_TF, D_MODEL), lambda i, f: (f, 0)),
    ]
    args = [h, nw, w1, w3, w2]
    out_specs = [pl.BlockSpec((FFN_TM, D_MODEL), lambda i, f: (i, 0))]
    out_shape = [jax.ShapeDtypeStruct((t, D_MODEL), F32)]
    if convert:
        n1, n3, n2, layer = nxt
        dr = D_MODEL // n_i
        in_specs += [
            pl.BlockSpec((None, dr, FFN_TF), lambda i, f: (layer, i, f)),
            pl.BlockSpec((None, dr, FFN_TF), lambda i, f: (layer, i, f)),
            pl.BlockSpec((None, FFN_TF, dr), lambda i, f: (layer, f, i)),
        ]
        args += [n1, n3, n2]
        out_specs += [
            pl.BlockSpec((dr, FFN_TF), lambda i, f: (i, f)),
            pl.BlockSpec((dr, FFN_TF), lambda i, f: (i, f)),
            pl.BlockSpec((FFN_TF, dr), lambda i, f: (f, i)),
        ]
        out_shape += [
            jax.ShapeDtypeStruct((D_MODEL, D_FF), BF16),
            jax.ShapeDtypeStruct((D_MODEL, D_FF), BF16),
            jax.ShapeDtypeStruct((D_FF, D_MODEL), BF16),
        ]
    if final:
        in_specs.append(pl.BlockSpec((1, D_MODEL), lambda i, f: (0, 0)))
        args.append(final_w)
    outs = pl.pallas_call(
        functools.partial(_ffn_kernel, n_f=n_f, final=final, convert=convert),
        grid=(n_i, n_f),
        in_specs=in_specs,
        out_specs=out_specs,
        out_shape=out_shape,
        scratch_shapes=[pltpu.VMEM((FFN_TM, D_MODEL), BF16)],
        compiler_params=_cparams(("parallel", "arbitrary")),
        name="ffn_final" if final else "ffn",
    )(*args)
    return outs[0], tuple(outs[1:])


def _inproj_kernel(x_ref, nw_ref, w_ref, o_ref, n_ref):
    @pl.when(pl.program_id(1) == 0)
    def _():
        n_ref[...] = _rms(x_ref[...], nw_ref[...]).astype(BF16)

    o_ref[...] = _dot(n_ref[...], w_ref[...])


def _inproj(h, nw, w, layer):
    t = h.shape[0]
    return pl.pallas_call(
        _inproj_kernel,
        grid=(t // PROJ_TM, D_IN_PAD // IN_TN),
        in_specs=[
            pl.BlockSpec((PROJ_TM, D_MODEL), lambda i, j: (i, 0)),
            pl.BlockSpec((1, D_MODEL), lambda i, j: (0, 0)),
            pl.BlockSpec((None, D_MODEL, IN_TN), lambda i, j: (layer, 0, j)),
        ],
        out_specs=pl.BlockSpec((PROJ_TM, IN_TN), lambda i, j: (i, j)),
        out_shape=jax.ShapeDtypeStruct((t, D_IN_PAD), F32),
        scratch_shapes=[pltpu.VMEM((PROJ_TM, D_MODEL), BF16)],
        compiler_params=_cparams(("parallel", "arbitrary")),
        name="inproj",
    )(h, nw, w)


def _rope_pair(p, c4, sa, sb):
    return p * c4 + pltpu.roll(p, 96, 1) * sa + pltpu.roll(p, 32, 1) * sb


def _rope_full(x, c, s):
    return x * c + pltpu.roll(x, 64, 1) * s


def _kv_kernel(ckv_ref, kr_ref, nw_ref, w_ref, c4_ref, sa_ref, sb_ref, kcat_ref, vt_ref):
    n = _rms(ckv_ref[...], nw_ref[...]).astype(BF16)
    kv = _dot(n, w_ref[...])
    kr = _rope_pair(kr_ref[...], c4_ref[...], sa_ref[...], sb_ref[...])
    kr_lo = kr.astype(BF16)
    kr_hi = pltpu.roll(kr, 64, 1).astype(BF16)
    for h in range(MLA_HEADS):
        kcat_ref[:, h * 256:h * 256 + 128] = kv[:, h * 128:(h + 1) * 128].astype(BF16)
        kcat_ref[:, h * 256 + 128:(h + 1) * 256] = kr_lo if h % 2 == 0 else kr_hi
    for s in range(KV_SUB):
        vt_ref[s] = kv[s * ATT_TK:(s + 1) * ATT_TK, MLA_HEADS * MLA_NOPE:].T.astype(BF16)


def _kv_prep(proj, nw, w, c4, sa, sb, seq, layer):
    t = proj.shape[0]
    ns = seq // KV_TM
    tab = pl.BlockSpec((KV_TM, LANES), lambda i: (i % ns, 0))
    return pl.pallas_call(
        _kv_kernel,
        grid=(t // KV_TM,),
        in_specs=[
            pl.BlockSpec((KV_TM, KV_RANK), lambda i: (i, COL_CKV // KV_RANK)),
            pl.BlockSpec((KV_TM, LANES), lambda i: (i, COL_KR // LANES)),
            pl.BlockSpec((1, KV_RANK), lambda i: (0, 0)),
            pl.BlockSpec((None, KV_RANK, 2048), lambda i: (layer, 0, 0)),
            tab, tab, tab,
        ],
        out_specs=[
            pl.BlockSpec((KV_TM, 2048), lambda i: (i, 0)),
            pl.BlockSpec((KV_SUB, 1024, ATT_TK), lambda i: (i, 0, 0)),
        ],
        out_shape=[
            jax.ShapeDtypeStruct((t, 2048), BF16),
            jax.ShapeDtypeStruct((t // ATT_TK, 1024, ATT_TK), BF16),
        ],
        compiler_params=_cparams(("parallel",)),
        name="kv_prep",
    )(proj, proj, nw, w, c4, sa, sb)


def _attn_kernel(qn_ref, qr_ref, c4_ref, sa_ref, sb_ref, k_ref, vt_ref, nw_ref, o_ref,
                 qt_s, m_s, acc_s):
    i = pl.program_id(1)
    tq, tk = ATT_TQ, ATT_TK
    scale = (MLA_NOPE + MLA_ROPE) ** -0.5

    for h in range(MLA_HEADS):
        if h % 2 == 0:
            pair = qr_ref[:, (h // 2) * LANES:(h // 2 + 1) * LANES]
            roped = _rope_pair(pair, c4_ref[...], sa_ref[...], sb_ref[...]) * scale
        qn = qn_ref[:, h * LANES:(h + 1) * LANES] * scale
        qt_s[h] = jnp.concatenate([qn, roped], axis=-1).T.astype(BF16)
    m_s[...] = jnp.full(m_s.shape, MASK_VALUE, F32)
    acc_s[...] = jnp.zeros(acc_s.shape, F32)

    def tile(j, masked):
        ones = jnp.ones((ATT_ONES, tk), BF16)

        def scores(h):
            return _dot(k_ref[j, :, h * 256:(h + 1) * 256], qt_s[h])

        pending = [scores(h) for h in range(ATT_AHEAD)]
        for h in range(MLA_HEADS):
            st = pending.pop(0)
            if h + ATT_AHEAD < MLA_HEADS:
                pending.append(scores(h + ATT_AHEAD))
            if masked:
                key = lax.broadcasted_iota(jnp.int32, (tk, tq), 0)
                qry = lax.broadcasted_iota(jnp.int32, (tk, tq), 1)
                st = jnp.where(key <= qry, st, MASK_VALUE)
            m_old = m_s[h]
            m_new = jnp.maximum(m_old, jnp.max(st, axis=0, keepdims=True))
            p = jnp.exp(st - m_new).astype(BF16)
            alpha = jnp.exp(m_old - m_new)
            vext = jnp.concatenate([vt_ref[j, h * LANES:(h + 1) * LANES, :], ones], axis=0)
            acc_s[h] = alpha * acc_s[h] + _dot(vext, p)
            m_s[h] = m_new

    def body(j, carry):
        tile(j, False)
        return carry

    lax.fori_loop(0, i, body, 0)
    tile(i, True)

    for h in range(MLA_HEADS):
        acc = acc_s[h]
        ot = acc[:MLA_V, :] / acc[MLA_V:MLA_V + 1, :]
        ot = ot * lax.rsqrt(jnp.mean(ot * ot, axis=0, keepdims=True) + EPS)
        ot = ot * nw_ref[h * LANES:(h + 1) * LANES, :]
        o_ref[:, h * LANES:(h + 1) * LANES] = ot.T.astype(BF16)


def _attention(proj, c4, sa, sb, kcat, vt, nw_col, batch, seq):
    t = proj.shape[0]
    nq = seq // ATT_TQ
    nk = seq // ATT_TK
    tab = pl.BlockSpec((ATT_TQ, LANES), lambda b, i: (i, 0))
    return pl.pallas_call(
        _attn_kernel,
        grid=(batch, nq),
        in_specs=[
            pl.BlockSpec((ATT_TQ, 1024), lambda b, i: (b * nq + i, COL_QN // 1024)),
            pl.BlockSpec((ATT_TQ, 512), lambda b, i: (b * nq + i, COL_QR // 512)),
            tab, tab, tab,
            pl.BlockSpec((nk, ATT_TK, 2048), lambda b, i: (b, 0, 0)),
            pl.BlockSpec((nk, 1024, ATT_TK), lambda b, i: (b, 0, 0)),
            pl.BlockSpec((1024, 1), lambda b, i: (0, 0)),
        ],
        out_specs=pl.BlockSpec((ATT_TQ, 1024), lambda b, i: (b * nq + i, 0)),
        out_shape=jax.ShapeDtypeStruct((t, 1024), BF16),
        scratch_shapes=[
            pltpu.VMEM((MLA_HEADS, 256, ATT_TQ), BF16),
            pltpu.VMEM((MLA_HEADS, 1, ATT_TQ), F32),
            pltpu.VMEM((MLA_HEADS, MLA_V + ATT_ONES, ATT_TQ), F32),
        ],
        compiler_params=_cparams(("parallel", "arbitrary")),
        name="mla_attention",
    )(proj, proj, c4, sa, sb, kcat.reshape(t // ATT_TK, ATT_TK, 2048), vt, nw_col)


def _ret_kernel(x_ref, c_ref, s_ref, gw_ref, o_ref, state_ref, decay_ref):
    c = RET_C
    lgs = [math.log1p(-(2.0 ** (-5.0 - h))) for h in range(RET_HEADS)]

    @pl.when(pl.program_id(1) == 0)
    def _():
        state_ref[...] = jnp.zeros_like(state_ref)
        row = lax.broadcasted_iota(jnp.int32, (c, c), 0)
        col = lax.broadcasted_iota(jnp.int32, (c, c), 1)
        rel = (row - col).astype(F32)
        for h in range(RET_HEADS):
            decay_ref[h] = jnp.where(rel >= 0, jnp.exp(lgs[h] * jnp.maximum(rel, 0.0)), 0.0)

    for s in range(RET_SUB):
        rows = pl.ds(s * c, c)
        _ret_chunk(x_ref.at[rows], c_ref.at[rows], s_ref.at[rows], gw_ref, o_ref.at[rows],
                   state_ref, decay_ref, lgs)


def _ret_chunk(x_ref, c_ref, s_ref, gw_ref, o_ref, state_ref, decay_ref, lgs):
    c = RET_C
    idx = lax.broadcasted_iota(jnp.int32, (c, 1), 0).astype(F32)
    cosv = c_ref[...]
    sinv = s_ref[...]
    heads = range(RET_HEADS)
    sls = [slice(h * RET_D, (h + 1) * RET_D) for h in heads]

    qbs, vbs, scores, far = [], [], [], []
    for h in heads:
        q = _rope_full(x_ref[:, sls[h]], cosv, sinv)
        k = _rope_full(x_ref[:, 512 + h * RET_D:512 + (h + 1) * RET_D], cosv, sinv) * (RET_D ** -0.5)
        qb = q.astype(BF16)
        vb = x_ref[:, 1024 + h * RET_D:1024 + (h + 1) * RET_D].astype(BF16)
        state = state_ref[h]
        scores.append(_dot_nt(qb, k.astype(BF16)))
        far.append(_dot(qb, state.astype(BF16)))
        zeta = jnp.exp(lgs[h] * (c - 1.0 - idx))
        state_ref[h] = state * math.exp(lgs[h] * c) + _dot_tn((k * zeta).astype(BF16), vb)
        qbs.append(qb)
        vbs.append(vb)

    outs = []
    for h in heads:
        xi = jnp.exp(lgs[h] * (idx + 1.0))
        outs.append(_dot((scores[h] * decay_ref[h]).astype(BF16), vbs[h]) + far[h] * xi)

    for h in heads:
        o = outs[h]
        mu = jnp.mean(o, axis=-1, keepdims=True)
        d = o - mu
        var = jnp.mean(d * d, axis=-1, keepdims=True)
        o = d * lax.rsqrt(var + EPS) * gw_ref[:, sls[h]]
        g = x_ref[:, 1536 + h * RET_D:1536 + (h + 1) * RET_D]
        o_ref[:, sls[h]] = (g * _sigmoid(g) * o).astype(BF16)


def _retention(proj, cosf, sinf, gw, batch, seq):
    t = proj.shape[0]
    rows = RET_C * RET_SUB
    nc = seq // rows
    tab = pl.BlockSpec((rows, LANES), lambda b, c: (c, 0))
    return pl.pallas_call(
        _ret_kernel,
        grid=(batch, nc),
        in_specs=[
            pl.BlockSpec((rows, 2048), lambda b, c: (b * nc + c, COL_RET // 2048)),
            tab, tab,
            pl.BlockSpec((1, 512), lambda b, c: (0, 0)),
        ],
        out_specs=pl.BlockSpec((rows, 512), lambda b, c: (b * nc + c, 0)),
        out_shape=jax.ShapeDtypeStruct((t, 512), BF16),
        scratch_shapes=[pltpu.VMEM((RET_HEADS, RET_D, RET_D), F32),
                        pltpu.VMEM((RET_HEADS, RET_C, RET_C), F32)],
        compiler_params=_cparams(("parallel", "arbitrary")),
        name="retention",
    )(proj, cosf, sinf, gw)


def _group_roll(x, shift):
    n, w = x.shape
    return pltpu.roll(x.reshape(n // SUBLANES, SUBLANES, w), shift, 1).reshape(n, w)


def _hgrn_kernel(x_ref, lbl_ref, nw_ref, o_ref, state_ref, *, layer):
    c = HG_C

    @pl.when(pl.program_id(1) == 0)
    def _():
        state_ref[...] = jnp.zeros_like(state_ref)

    logits = lbl_ref[...]
    e = jnp.exp(logits - jnp.max(logits, axis=0, keepdims=True))
    p = e / jnp.sum(e, axis=0, keepdims=True)
    lb_all = jnp.zeros((1, HG_HEADS * HG_D), F32)
    for m in range(layer + 1):
        lb_all = lb_all + p[m:m + 1, :]
    lb_all = lb_all - p[0:1, :]

    for s in range(HG_SUB):
        rows = pl.ds(s * c, c)
        _hgrn_chunk(x_ref.at[rows], lb_all, nw_ref, o_ref.at[rows], state_ref)


def _hgrn_chunk(x_ref, lb_all, nw_ref, o_ref, state_ref):
    c = HG_C
    rowv = lax.broadcasted_iota(jnp.int32, (c, LANES), 0)
    row = lax.broadcasted_iota(jnp.int32, (c, c), 0)
    col = lax.broadcasted_iota(jnp.int32, (c, c), 1)

    z = x_ref[:, 512:1024]
    ez = jnp.exp(-jnp.abs(z))
    r = 1.0 / (1.0 + ez)
    pos = z >= 0
    sig_p = jnp.where(pos, r, ez * r)
    sig_n = jnp.where(pos, ez * r, r)
    f = lb_all + (1.0 - lb_all) * sig_p
    lf = jnp.log(jnp.maximum(f, MIN_FORGET))
    kk_all = (1.0 - lb_all) * sig_n

    tri = jnp.where(col <= row, 1.0, 0.0).astype(BF16)
    lf_hi = lf.astype(BF16)
    rem = lf - lf_hi.astype(F32)
    lf_mid = rem.astype(BF16)
    lf_lo = (rem - lf_mid.astype(F32)).astype(BF16)
    b_all = _dot(tri, lf_hi) + _dot(tri, lf_mid) + _dot(tri, lf_lo)

    heads = range(HG_HEADS)
    sls = [slice(h * HG_D, (h + 1) * HG_D) for h in heads]
    qs = [x_ref[:, sls[h]] for h in heads]
    kks = [kk_all[:, sls[h]] for h in heads]
    bs = [b_all[:, sls[h]] for h in heads]
    vbs = [x_ref[:, 1024 + h * HG_D:1024 + (h + 1) * HG_D].astype(BF16) for h in heads]


    o_far = []
    for h in heads:
        state = state_ref[h]
        b_last = bs[h][c - 1:c, :]
        o_far.append(_dot_nt((qs[h] * jnp.exp(bs[h])).astype(BF16), state.astype(BF16)))
        upd = _dot_tn(vbs[h], (kks[h] * jnp.exp(b_last - bs[h])).astype(BF16))
        state_ref[h] = jnp.exp(b_last) * state + upd

    a_lvl = []
    for h in heads:
        q, kk, b = qs[h], kks[h], bs[h]
        a = None
        m = HG_NEAR
        while m < c:
            parts = []
            for blk in range(c // (2 * m)):
                lo = blk * 2 * m
                ref = b[lo + m - 1:lo + m, :]
                parts.append(b[lo:lo + 2 * m, :] - ref)
            d = parts[0] if len(parts) == 1 else jnp.concatenate(parts, axis=0)
            second = (rowv & (2 * m - 1)) >= m
            efac = jnp.exp(jnp.where(second, d, -d))
            ql = jnp.where(second, q * efac, 0.0).astype(BF16)
            kl = jnp.where(second, 0.0, kk * efac).astype(BF16)
            al = _dot_nt(ql, kl)
            if 2 * m < c:
                sft = (2 * m).bit_length() - 1
                al = jnp.where((row >> sft) == (col >> sft), al, 0.0)
            a = al if a is None else a + al
            m *= 2
        a_lvl.append(a)

    a_all = []
    for h in heads:
        q, kk, b = qs[h], kks[h], bs[h]
        a = a_lvl[h] + jnp.where(col == row, jnp.sum(q * kk, axis=-1, keepdims=True), 0.0)
        for dlt in range(1, HG_NEAR):
            ok = (rowv & (HG_NEAR - 1)) >= dlt
            diff = jnp.where(ok, b - _group_roll(b, dlt), 0.0)
            a_d = jnp.sum(q * _group_roll(kk, dlt) * jnp.exp(diff), axis=-1, keepdims=True)
            a = a + jnp.where((col == row - dlt) & ((row & (HG_NEAR - 1)) >= dlt), a_d, 0.0)
        a_all.append(a)

    for h in heads:
        o = _dot(a_all[h].astype(BF16), vbs[h]) + o_far[h]
        o = o * lax.rsqrt(jnp.mean(o * o, axis=-1, keepdims=True) + EPS)
        o = o * nw_ref[:, sls[h]]
        g = x_ref[:, 1536 + h * HG_D:1536 + (h + 1) * HG_D]
        o_ref[:, sls[h]] = (g * _sigmoid(g) * o).astype(BF16)


def _hgrn(proj, lb_logits, nw, layer, batch, seq):
    t = proj.shape[0]
    rows = HG_C * HG_SUB
    nc = seq // rows
    return pl.pallas_call(
        functools.partial(_hgrn_kernel, layer=layer),
        grid=(batch, nc),
        in_specs=[
            pl.BlockSpec((rows, 2048), lambda b, c: (b * nc + c, COL_HG // 2048)),
            pl.BlockSpec((DEPTH, 512), lambda b, c: (0, 0)),
            pl.BlockSpec((1, 512), lambda b, c: (0, 0)),
        ],
        out_specs=pl.BlockSpec((rows, 512), lambda b, c: (b * nc + c, 0)),
        out_shape=jax.ShapeDtypeStruct((t, 512), BF16),
        scratch_shapes=[pltpu.VMEM((HG_HEADS, HG_D, HG_D), F32)],
        compiler_params=_cparams(("parallel", "arbitrary")),
        name="hgrn2",
    )(proj, lb_logits, nw)


def _outproj_kernel(h_ref, oa_ref, ob_ref, oc_ref, w_ref, o_ref):
    acc = _dot(oa_ref[...], w_ref[0:1024, :])
    acc += _dot(ob_ref[...], w_ref[1024:1536, :])
    acc += _dot(oc_ref[...], w_ref[1536:2048, :])
    o_ref[...] = h_ref[...] + acc


def _outproj(h, oa, ob, oc, w, layer):
    t = h.shape[0]
    return pl.pallas_call(
        _outproj_kernel,
        grid=(t // OUT_TM,),
        in_specs=[
            pl.BlockSpec((OUT_TM, D_MODEL), lambda i: (i, 0)),
            pl.BlockSpec((OUT_TM, 1024), lambda i: (i, 0)),
            pl.BlockSpec((OUT_TM, 512), lambda i: (i, 0)),
            pl.BlockSpec((OUT_TM, 512), lambda i: (i, 0)),
            pl.BlockSpec((None, D_MODEL, D_MODEL), lambda i: (layer, 0, 0)),
        ],
        out_specs=pl.BlockSpec((OUT_TM, D_MODEL), lambda i: (i, 0)),
        out_shape=jax.ShapeDtypeStruct((t, D_MODEL), F32),
        compiler_params=_cparams(("parallel",)),
        name="outproj",
    )(h, oa, ob, oc, w)


def _rope_tables(seq):
    inv64 = ROPE_BASE ** (-jnp.arange(0, MLA_ROPE, 2, dtype=F32) / MLA_ROPE)
    ang64 = jnp.arange(seq, dtype=F32)[:, None] * inv64[None, :]
    c, s = jnp.cos(ang64), jnp.sin(ang64)
    z = jnp.zeros_like(s)
    c4 = jnp.concatenate([c, c, c, c], axis=-1)
    sa = jnp.concatenate([-s, z, -s, z], axis=-1)
    sb = jnp.concatenate([z, s, z, s], axis=-1)
    inv128 = ROPE_BASE ** (-jnp.arange(0, RET_D, 2, dtype=F32) / RET_D)
    ang128 = jnp.arange(seq, dtype=F32)[:, None] * inv128[None, :]
    cf, sf = jnp.cos(ang128), jnp.sin(ang128)
    return c4, sa, sb, jnp.concatenate([cf, cf], axis=-1), jnp.concatenate([-sf, sf], axis=-1)


def _win_kernel(wt_ref, o_ref):
    hd = MLA_NOPE + MLA_ROPE
    nq = MLA_HEADS * hd
    tk = o_ref.shape[0]

    def put(col, *row_ranges):
        pieces = [wt_ref[r0:r1, :] for r0, r1 in row_ranges]
        piece = pieces[0] if len(pieces) == 1 else jnp.concatenate(pieces, axis=0)
        o_ref[:, col:col + piece.shape[0]] = piece.T.astype(BF16)

    for h in range(MLA_HEADS):
        put(COL_QN + h * MLA_NOPE, (h * hd, h * hd + MLA_NOPE))
    for p in range(MLA_HEADS // 2):
        put(COL_QR + p * LANES, ((2 * p) * hd + MLA_NOPE, (2 * p + 1) * hd),
            ((2 * p + 1) * hd + MLA_NOPE, (2 * p + 2) * hd))
    for blk in range(KV_RANK // LANES):
        put(COL_CKV + blk * LANES, (nq + blk * LANES, nq + (blk + 1) * LANES))
    src = nq + KV_RANK + MLA_ROPE
    for blk in range((COL_KR - COL_RET) // LANES):
        put(COL_RET + blk * LANES, (src + blk * LANES, src + (blk + 1) * LANES))
    kr = jnp.concatenate([wt_ref[nq + KV_RANK:nq + KV_RANK + MLA_ROPE, :],
                          jnp.zeros((LANES - MLA_ROPE, tk), F32)], axis=0)
    o_ref[:, COL_KR:COL_KR + LANES] = kr.T.astype(BF16)
    o_ref[:, COL_KR + LANES:] = jnp.zeros((tk, D_IN_PAD - COL_KR - LANES), BF16)


def _prep_w_in(w_in):
    depth, d, n = w_in.shape
    return pl.pallas_call(
        _win_kernel,
        grid=(depth, d // WIN_TK),
        in_specs=[pl.BlockSpec((None, n, WIN_TK), lambda l, r: (l, 0, r))],
        out_specs=pl.BlockSpec((None, WIN_TK, D_IN_PAD), lambda l, r: (l, r, 0)),
        out_shape=jax.ShapeDtypeStruct((depth, d, D_IN_PAD), BF16),
        compiler_params=_cparams(("parallel", "parallel")),
        name="w_in_relayout",
    )(jnp.swapaxes(w_in, 1, 2))


def _prep_w_kv(w):
    w = w.reshape(DEPTH, KV_RANK, MLA_HEADS, MLA_NOPE + MLA_V)
    kn = w[..., :MLA_NOPE].reshape(DEPTH, KV_RANK, MLA_HEADS * MLA_NOPE)
    v = w[..., MLA_NOPE:].reshape(DEPTH, KV_RANK, MLA_HEADS * MLA_V)
    return jnp.concatenate([kn, v], axis=-1).astype(BF16)


def kernel(x, ffn1_norm, ffn1_w1, ffn1_w3, ffn1_w2, mix_norm, w_in, mla_kv_norm, mla_w_kv_b,
           mla_out_norm, ret_gn, hgrn_lb_logits, hgrn_out_norm, w_o, ffn2_norm, ffn2_w1,
           ffn2_w3, ffn2_w2, final_norm):
    batch, seq, d = x.shape
    assert d == D_MODEL and seq % ATT_TQ == 0 and seq % ATT_TK == 0 and seq % RET_C == 0
    assert ATT_TK == ATT_TQ
    t = batch * seq
    assert t % PROJ_TM == 0 and t % FFN_TM == 0
    c4, sa, sb, cosf, sinf = _rope_tables(seq)
    w_in_p = _prep_w_in(w_in)
    w_kv_p = _prep_w_kv(mla_w_kv_b)
    row = lambda a: a.reshape(1, -1)

    w_o_b = w_o.astype(BF16)
    ffn_w = (ffn1_w1[0].astype(BF16), ffn1_w3[0].astype(BF16), ffn1_w2[0].astype(BF16))
    h = x.reshape(t, d)
    for l in range(DEPTH):
        h, ffn_w = _ffn(h, row(ffn1_norm[l]), *ffn_w, nxt=(ffn2_w1, ffn2_w3, ffn2_w2, l))
        proj = _inproj(h, row(mix_norm[l]), w_in_p, l)
        kcat, vt = _kv_prep(proj, row(mla_kv_norm[l]), w_kv_p, c4, sa, sb, seq, l)
        oa = _attention(proj, c4, sa, sb, kcat, vt, mla_out_norm[l].reshape(-1, 1), batch, seq)
        ob = _retention(proj, cosf, sinf, row(ret_gn[l]), batch, seq)
        oc = _hgrn(proj, hgrn_lb_logits, row(hgrn_out_norm[l]), l, batch, seq)
        h = _outproj(h, oa, ob, oc, w_o_b, l)
        last = l == DEPTH - 1
        h, ffn_w = _ffn(h, row(ffn2_norm[l]), *ffn_w,
                        nxt=None if last else (ffn1_w1, ffn1_w3, ffn1_w2, l + 1),
                        final_w=row(final_norm) if last else None)
    return h.reshape(batch, seq, d)
```

```python
import functools
import math

import jax
import jax.numpy as jnp
from jax import lax
from jax.experimental import pallas as pl
from jax.experimental.pallas import tpu as pltpu

F32 = jnp.float32
BF16 = jnp.bfloat16

D_MODEL = 2048
DEPTH = 4
MLA_HEADS = 8
MLA_NOPE = 128
MLA_ROPE = 64
MLA_V = 128
KV_RANK = 512
RET_HEADS = 4
RET_D = 128
HG_HEADS = 4
HG_D = 128
D_FF = 5632
ROPE_BASE = 10000.0
EPS = 1e-6
MASK_VALUE = -1e30
MIN_FORGET = 1e-20

LANES = 128
SUBLANES = 8
VMEM_LIMIT = 60 * 1024 * 1024

COL_QN = 0
COL_QR = 1024
COL_CKV = 1536
COL_RET = 2048
COL_HG = 4096
COL_KR = 6144
D_IN_PAD = 6400
IN_TN = 1280

FFN_TM = 1024
FFN_TF = 512
PROJ_TM = 1024
ATT_TQ = 512
ATT_TK = 512
ATT_AHEAD = 2
ATT_ONES = 16
KV_SUB = 2
KV_TM = KV_SUB * ATT_TK
RET_C = 256
HG_C = 128
REC_ROWS = 512
HG_NEAR = 4
OUT_TM = 512
WIN_TK = 256


def _cparams(sem):
    return pltpu.CompilerParams(dimension_semantics=sem, vmem_limit_bytes=VMEM_LIMIT)


def _rms(x, w):
    return (x * lax.rsqrt(jnp.mean(x * x, axis=-1, keepdims=True) + EPS)) * w


def _sigmoid(x):
    return 1.0 / (1.0 + jnp.exp(-x))


def _dot(a, b):
    return jnp.dot(a, b, preferred_element_type=F32)


def _dot_nt(a, b):
    return lax.dot_general(a, b, (((1,), (1,)), ((), ())), preferred_element_type=F32)


def _dot_tn(a, b):
    return lax.dot_general(a, b, (((0,), (0,)), ((), ())), preferred_element_type=F32)


def _ffn_kernel(x_ref, nw_ref, w1_ref, w3_ref, w2_ref, *rest, n_f, final, convert):
    rest = list(rest)
    cast_in = [rest.pop(0) for _ in range(3)] if convert else []
    fw_ref = rest.pop(0) if final else None
    o_ref = rest.pop(0)
    cast_out = [rest.pop(0) for _ in range(3)] if convert else []
    (n_ref,) = rest
    f = pl.program_id(1)

    for src, dst in zip(cast_in, cast_out):
        dst[...] = src[...].astype(BF16)

    @pl.when(f == 0)
    def _():
        x = x_ref[...]
        n_ref[...] = _rms(x, nw_ref[...]).astype(BF16)
        o_ref[...] = x

    n = n_ref[...]
    h1 = _dot(n, w1_ref[...])
    h3 = _dot(n, w3_ref[...])
    g = (h1 * _sigmoid(h1) * h3 * 0.5).astype(BF16)
    o_ref[...] += _dot(g, w2_ref[...])

    if final:
        @pl.when(f == n_f - 1)
        def _():
            o_ref[...] = _rms(o_ref[...], fw_ref[...])


def _ffn(h, nw, w1, w3, w2, nxt=None, final_w=None):
    t = h.shape[0]
    n_i = t // FFN_TM
    n_f = D_FF // FFN_TF
    final = final_w is not None
    convert = nxt is not None
    in_specs = [
        pl.BlockSpec((FFN_TM, D_MODEL), lambda i, f: (i, 0)),
        pl.BlockSpec((1, D_MODEL), lambda i, f: (0, 0)),
        pl.BlockSpec((D_MODEL, FFN_TF), lambda i, f: (0, f)),
        pl.BlockSpec((D_MODEL, FFN_TF), lambda i, f: (0, f)),
        pl.BlockSpec((FFN_TF, D_MODEL), lambda i, f: (f, 0)),
    ]
    args = [h, nw, w1, w3, w2]
    out_specs = [pl.BlockSpec((FFN_TM, D_MODEL), lambda i, f: (i, 0))]
    out_shape = [jax.ShapeDtypeStruct((t, D_MODEL), F32)]
    if convert:
        n1, n3, n2, layer = nxt
        dr = D_MODEL // n_i
        in_specs += [
            pl.BlockSpec((None, dr, FFN_TF), lambda i, f: (layer, i, f)),
            pl.BlockSpec((None, dr, FFN_TF), lambda i, f: (layer, i, f)),
            pl.BlockSpec((None, FFN_TF, dr), lambda i, f: (layer, f, i)),
        ]
        args += [n1, n3, n2]
        out_specs += [
            pl.BlockSpec((dr, FFN_TF), lambda i, f: (i, f)),
            pl.BlockSpec((dr, FFN_TF), lambda i, f: (i, f)),
            pl.BlockSpec((FFN_TF, dr), lambda i, f: (f, i)),
        ]
        out_shape += [
            jax.ShapeDtypeStruct((D_MODEL, D_FF), BF16),
            jax.ShapeDtypeStruct((D_MODEL, D_FF), BF16),
            jax.ShapeDtypeStruct((D_FF, D_MODEL), BF16),
        ]
    if final:
        in_specs.append(pl.BlockSpec((1, D_MODEL), lambda i, f: (0, 0)))
        args.append(final_w)
    outs = pl.pallas_call(
        functools.partial(_ffn_kernel, n_f=n_f, final=final, convert=convert),
        grid=(n_i, n_f),
        in_specs=in_specs,
        out_specs=out_specs,
        out_shape=out_shape,
        scratch_shapes=[pltpu.VMEM((FFN_TM, D_MODEL), BF16)],
        compiler_params=_cparams(("parallel", "arbitrary")),
        name="ffn_final" if final else "ffn",
    )(*args)
    return outs[0], tuple(outs[1:])


def _inproj_kernel(x_ref, nw_ref, w_ref, o_ref, n_ref):
    @pl.when(pl.program_id(1) == 0)
    def _():
        n_ref[...] = _rms(x_ref[...], nw_ref[...]).astype(BF16)

    o_ref[...] = _dot(n_ref[...], w_ref[...])


def _inproj(h, nw, w, layer):
    t = h.shape[0]
    return pl.pallas_call(
        _inproj_kernel,
        grid=(t // PROJ_TM, D_IN_PAD // IN_TN),
        in_specs=[
            pl.BlockSpec((PROJ_TM, D_MODEL), lambda i, j: (i, 0)),
            pl.BlockSpec((1, D_MODEL), lambda i, j: (0, 0)),
            pl.BlockSpec((None, D_MODEL, IN_TN), lambda i, j: (layer, 0, j)),
        ],
        out_specs=pl.BlockSpec((PROJ_TM, IN_TN), lambda i, j: (i, j)),
        out_shape=jax.ShapeDtypeStruct((t, D_IN_PAD), F32),
        scratch_shapes=[pltpu.VMEM((PROJ_TM, D_MODEL), BF16)],
        compiler_params=_cparams(("parallel", "arbitrary")),
        name="inproj",
    )(h, nw, w)


def _rope_pair(p, c4, sa, sb):
    return p * c4 + pltpu.roll(p, 96, 1) * sa + pltpu.roll(p, 32, 1) * sb


def _rope_full(x, c, s):
    return x * c + pltpu.roll(x, 64, 1) * s


def _kv_kernel(ckv_ref, kr_ref, nw_ref, w_ref, c4_ref, sa_ref, sb_ref, kcat_ref, vt_ref):
    n = _rms(ckv_ref[...], nw_ref[...]).astype(BF16)
    kv = _dot(n, w_ref[...])
    kr = _rope_pair(kr_ref[...], c4_ref[...], sa_ref[...], sb_ref[...])
    kr_lo = kr.astype(BF16)
    kr_hi = pltpu.roll(kr, 64, 1).astype(BF16)
    for h in range(MLA_HEADS):
        kcat_ref[:, h * 256:h * 256 + 128] = kv[:, h * 128:(h + 1) * 128].astype(BF16)
        kcat_ref[:, h * 256 + 128:(h + 1) * 256] = kr_lo if h % 2 == 0 else kr_hi
    for s in range(KV_SUB):
        vt_ref[s] = kv[s * ATT_TK:(s + 1) * ATT_TK, MLA_HEADS * MLA_NOPE:].T.astype(BF16)


def _kv_prep(proj, nw, w, c4, sa, sb, seq, layer):
    t = proj.shape[0]
    ns = seq // KV_TM
    tab = pl.BlockSpec((KV_TM, LANES), lambda i: (i % ns, 0))
    return pl.pallas_call(
        _kv_kernel,
        grid=(t // KV_TM,),
        in_specs=[
            pl.BlockSpec((KV_TM, KV_RANK), lambda i: (i, COL_CKV // KV_RANK)),
            pl.BlockSpec((KV_TM, LANES), lambda i: (i, COL_KR // LANES)),
            pl.BlockSpec((1, KV_RANK), lambda i: (0, 0)),
            pl.BlockSpec((None, KV_RANK, 2048), lambda i: (layer, 0, 0)),
            tab, tab, tab,
        ],
        out_specs=[
            pl.BlockSpec((KV_TM, 2048), lambda i: (i, 0)),
            pl.BlockSpec((KV_SUB, 1024, ATT_TK), lambda i: (i, 0, 0)),
        ],
        out_shape=[
            jax.ShapeDtypeStruct((t, 2048), BF16),
            jax.ShapeDtypeStruct((t // ATT_TK, 1024, ATT_TK), BF16),
        ],
        compiler_params=_cparams(("parallel",)),
        name="kv_prep",
    )(proj, proj, nw, w, c4, sa, sb)


def _attn_kernel(qn_ref, qr_ref, c4_ref, sa_ref, sb_ref, k_ref, vt_ref, nw_ref, o_ref,
                 qt_s, m_s, acc_s):
    i = pl.program_id(1)
    tq, tk = ATT_TQ, ATT_TK
    scale = (MLA_NOPE + MLA_ROPE) ** -0.5 * math.log2(math.e)

    for h in range(MLA_HEADS):
        if h % 2 == 0:
            pair = qr_ref[:, (h // 2) * LANES:(h // 2 + 1) * LANES]
            roped = _rope_pair(pair, c4_ref[...], sa_ref[...], sb_ref[...]) * scale
        qn = qn_ref[:, h * LANES:(h + 1) * LANES] * scale
        qt_s[h] = jnp.concatenate([qn, roped], axis=-1).T.astype(BF16)
    m_s[...] = jnp.full(m_s.shape, MASK_VALUE, F32)
    acc_s[...] = jnp.zeros(acc_s.shape, F32)

    def tile(j, masked):
        ones = jnp.ones((ATT_ONES, tk), BF16)

        def scores(h):
            return _dot(k_ref[j, :, h * 256:(h + 1) * 256], qt_s[h])

        pending = [scores(h) for h in range(ATT_AHEAD)]
        for h in range(MLA_HEADS):
            st = pending.pop(0)
            if h + ATT_AHEAD < MLA_HEADS:
                pending.append(scores(h + ATT_AHEAD))
            if masked:
                key = lax.broadcasted_iota(jnp.int32, (tk, tq), 0)
                qry = lax.broadcasted_iota(jnp.int32, (tk, tq), 1)
                st = jnp.where(key <= qry, st, MASK_VALUE)
            m_old = m_s[h]
            m_new = jnp.maximum(m_old, jnp.max(st, axis=0, keepdims=True))
            p = jnp.exp2(st - m_new).astype(BF16)
            alpha = jnp.exp2(m_old - m_new)
            vext = jnp.concatenate([vt_ref[j, h * LANES:(h + 1) * LANES, :], ones], axis=0)
            acc_s[h] = alpha * acc_s[h] + _dot(vext, p)
            m_s[h] = m_new

    def body(j, carry):
        tile(j, False)
        return carry

    lax.fori_loop(0, i, body, 0)
    tile(i, True)

    for h in range(MLA_HEADS):
        acc = acc_s[h]
        ot = acc[:MLA_V, :] / acc[MLA_V:MLA_V + 1, :]
        ot = ot * lax.rsqrt(jnp.mean(ot * ot, axis=0, keepdims=True) + EPS)
        ot = ot * nw_ref[h * LANES:(h + 1) * LANES, :]
        o_ref[:, h * LANES:(h + 1) * LANES] = ot.T.astype(BF16)


def _attention(proj, c4, sa, sb, kcat, vt, nw_col, batch, seq):
    t = proj.shape[0]
    nq = seq // ATT_TQ
    nk = seq // ATT_TK
    tab = pl.BlockSpec((ATT_TQ, LANES), lambda b, i: (i, 0))
    return pl.pallas_call(
        _attn_kernel,
        grid=(batch, nq),
        in_specs=[
            pl.BlockSpec((ATT_TQ, 1024), lambda b, i: (b * nq + i, COL_QN // 1024)),
            pl.BlockSpec((ATT_TQ, 512), lambda b, i: (b * nq + i, COL_QR // 512)),
            tab, tab, tab,
            pl.BlockSpec((nk, ATT_TK, 2048), lambda b, i: (b, 0, 0)),
            pl.BlockSpec((nk, 1024, ATT_TK), lambda b, i: (b, 0, 0)),
            pl.BlockSpec((1024, 1), lambda b, i: (0, 0)),
        ],
        out_specs=pl.BlockSpec((ATT_TQ, 1024), lambda b, i: (b * nq + i, 0)),
        out_shape=jax.ShapeDtypeStruct((t, 1024), BF16),
        scratch_shapes=[
            pltpu.VMEM((MLA_HEADS, 256, ATT_TQ), BF16),
            pltpu.VMEM((MLA_HEADS, 1, ATT_TQ), F32),
            pltpu.VMEM((MLA_HEADS, MLA_V + ATT_ONES, ATT_TQ), F32),
        ],
        compiler_params=_cparams(("parallel", "arbitrary")),
        name="mla_attention",
    )(proj, proj, c4, sa, sb, kcat.reshape(t // ATT_TK, ATT_TK, 2048), vt, nw_col)


def _ret_chunk(x_ref, c_ref, s_ref, gw_ref, o_ref, state_ref, decay_ref, lgs):
    c = RET_C
    idx = lax.broadcasted_iota(jnp.int32, (c, 1), 0).astype(F32)
    cosv = c_ref[...]
    sinv = s_ref[...]
    heads = range(RET_HEADS)
    sls = [slice(h * RET_D, (h + 1) * RET_D) for h in heads]

    qbs, vbs, scores, far = [], [], [], []
    for h in heads:
        q = _rope_full(x_ref[:, sls[h]], cosv, sinv)
        k = _rope_full(x_ref[:, 512 + h * RET_D:512 + (h + 1) * RET_D], cosv, sinv) * (RET_D ** -0.5)
        qb = q.astype(BF16)
        vb = x_ref[:, 1024 + h * RET_D:1024 + (h + 1) * RET_D].astype(BF16)
        state = state_ref[h]
        scores.append(_dot_nt(qb, k.astype(BF16)))
        far.append(_dot(qb, state.astype(BF16)))
        zeta = jnp.exp(lgs[h] * (c - 1.0 - idx))
        state_ref[h] = state * math.exp(lgs[h] * c) + _dot_tn((k * zeta).astype(BF16), vb)
        qbs.append(qb)
        vbs.append(vb)

    outs = []
    for h in heads:
        xi = jnp.exp(lgs[h] * (idx + 1.0))
        outs.append(_dot((scores[h] * decay_ref[h]).astype(BF16), vbs[h]) + far[h] * xi)

    for h in heads:
        o = outs[h]
        mu = jnp.mean(o, axis=-1, keepdims=True)
        d = o - mu
        var = jnp.mean(d * d, axis=-1, keepdims=True)
        o = d * lax.rsqrt(var + EPS) * gw_ref[:, sls[h]]
        g = x_ref[:, 1536 + h * RET_D:1536 + (h + 1) * RET_D]
        o_ref[:, sls[h]] = (g * _sigmoid(g) * o).astype(BF16)


def _group_roll(x, shift):
    n, w = x.shape
    return pltpu.roll(x.reshape(n // SUBLANES, SUBLANES, w), shift, 1).reshape(n, w)


def _rec_kernel(xr_ref, xh_ref, c_ref, s_ref, gw_ref, lbl_ref, nw_ref, ob_ref, oc_ref,
                rstate_ref, decay_ref, hstate_ref, *, layer):
    lgs = [math.log1p(-(2.0 ** (-5.0 - h))) for h in range(RET_HEADS)]

    @pl.when(pl.program_id(1) == 0)
    def _():
        hstate_ref[...] = jnp.zeros_like(hstate_ref)
        rstate_ref[...] = jnp.zeros_like(rstate_ref)
        row = lax.broadcasted_iota(jnp.int32, (RET_C, RET_C), 0)
        col = lax.broadcasted_iota(jnp.int32, (RET_C, RET_C), 1)
        rel = (row - col).astype(F32)
        for h in range(RET_HEADS):
            decay_ref[h] = jnp.where(rel >= 0, jnp.exp(lgs[h] * jnp.maximum(rel, 0.0)), 0.0)

    logits = lbl_ref[...]
    e = jnp.exp(logits - jnp.max(logits, axis=0, keepdims=True))
    p = e / jnp.sum(e, axis=0, keepdims=True)
    lb_all = jnp.zeros((1, HG_HEADS * HG_D), F32)
    for m in range(layer + 1):
        lb_all = lb_all + p[m:m + 1, :]
    lb_all = lb_all - p[0:1, :]

    hg_per_ret = RET_C // HG_C
    for s in range(REC_ROWS // RET_C):
        rows = pl.ds(s * RET_C, RET_C)
        _ret_chunk(xr_ref.at[rows], c_ref.at[rows], s_ref.at[rows], gw_ref, ob_ref.at[rows],
                   rstate_ref, decay_ref, lgs)
        for u in range(hg_per_ret):
            rows = pl.ds((s * hg_per_ret + u) * HG_C, HG_C)
            _hgrn_chunk(xh_ref.at[rows], lb_all, nw_ref, oc_ref.at[rows], hstate_ref)


def _recurrent(proj, cosf, sinf, gw, lb_logits, nw, layer, batch, seq):
    t = proj.shape[0]
    nc = seq // REC_ROWS
    tab = pl.BlockSpec((REC_ROWS, LANES), lambda b, c: (c, 0))
    out = pl.BlockSpec((REC_ROWS, 512), lambda b, c: (b * nc + c, 0))
    return pl.pallas_call(
        functools.partial(_rec_kernel, layer=layer),
        grid=(batch, nc),
        in_specs=[
            pl.BlockSpec((REC_ROWS, 2048), lambda b, c: (b * nc + c, COL_RET // 2048)),
            pl.BlockSpec((REC_ROWS, 2048), lambda b, c: (b * nc + c, COL_HG // 2048)),
            tab, tab,
            pl.BlockSpec((1, 512), lambda b, c: (0, 0)),
            pl.BlockSpec((DEPTH, 512), lambda b, c: (0, 0)),
            pl.BlockSpec((1, 512), lambda b, c: (0, 0)),
        ],
        out_specs=[out, out],
        out_shape=[jax.ShapeDtypeStruct((t, 512), BF16), jax.ShapeDtypeStruct((t, 512), BF16)],
        scratch_shapes=[pltpu.VMEM((RET_HEADS, RET_D, RET_D), F32),
                        pltpu.VMEM((RET_HEADS, RET_C, RET_C), F32),
                        pltpu.VMEM((HG_HEADS, HG_D, HG_D), F32)],
        compiler_params=_cparams(("parallel", "arbitrary")),
        name="retention_hgrn2",
    )(proj, proj, cosf, sinf, gw, lb_logits, nw)


def _hgrn_chunk(x_ref, lb_all, nw_ref, o_ref, state_ref):
    c = HG_C
    rowv = lax.broadcasted_iota(jnp.int32, (c, LANES), 0)
    row = lax.broadcasted_iota(jnp.int32, (c, c), 0)
    col = lax.broadcasted_iota(jnp.int32, (c, c), 1)

    z = x_ref[:, 512:1024]
    ez = jnp.exp(-jnp.abs(z))
    r = 1.0 / (1.0 + ez)
    pos = z >= 0
    sig_p = jnp.where(pos, r, ez * r)
    sig_n = jnp.where(pos, ez * r, r)
    f = lb_all + (1.0 - lb_all) * sig_p
    lf = jnp.log(jnp.maximum(f, MIN_FORGET))
    kk_all = (1.0 - lb_all) * sig_n

    tri = jnp.where(col <= row, 1.0, 0.0).astype(BF16)
    lf_hi = lf.astype(BF16)
    rem = lf - lf_hi.astype(F32)
    lf_mid = rem.astype(BF16)
    lf_lo = (rem - lf_mid.astype(F32)).astype(BF16)
    b_all = _dot(tri, lf_hi) + _dot(tri, lf_mid) + _dot(tri, lf_lo)

    heads = range(HG_HEADS)
    sls = [slice(h * HG_D, (h + 1) * HG_D) for h in heads]
    qs = [x_ref[:, sls[h]] for h in heads]
    kks = [kk_all[:, sls[h]] for h in heads]
    bs = [b_all[:, sls[h]] for h in heads]
    vbs = [x_ref[:, 1024 + h * HG_D:1024 + (h + 1) * HG_D].astype(BF16) for h in heads]


    o_far = []
    for h in heads:
        state = state_ref[h]
        b_last = bs[h][c - 1:c, :]
        o_far.append(_dot_nt((qs[h] * jnp.exp(bs[h])).astype(BF16), state.astype(BF16)))
        upd = _dot_tn(vbs[h], (kks[h] * jnp.exp(b_last - bs[h])).astype(BF16))
        state_ref[h] = jnp.exp(b_last) * state + upd

    a_lvl = []
    for h in heads:
        q, kk, b = qs[h], kks[h], bs[h]
        a = None
        m = HG_NEAR
        while m < c:
            parts = []
            for blk in range(c // (2 * m)):
                lo = blk * 2 * m
                ref = b[lo + m - 1:lo + m, :]
                parts.append(b[lo:lo + 2 * m, :] - ref)
            d = parts[0] if len(parts) == 1 else jnp.concatenate(parts, axis=0)
            second = (rowv & (2 * m - 1)) >= m
            efac = jnp.exp(jnp.where(second, d, -d))
            ql = jnp.where(second, q * efac, 0.0).astype(BF16)
            kl = jnp.where(second, 0.0, kk * efac).astype(BF16)
            al = _dot_nt(ql, kl)
            if 2 * m < c:
                sft = (2 * m).bit_length() - 1
                al = jnp.where((row >> sft) == (col >> sft), al, 0.0)
            a = al if a is None else a + al
            m *= 2
        a_lvl.append(a)

    a_all = []
    for h in heads:
        q, kk, b = qs[h], kks[h], bs[h]
        a = a_lvl[h] + jnp.where(col == row, jnp.sum(q * kk, axis=-1, keepdims=True), 0.0)
        for dlt in range(1, HG_NEAR):
            ok = (rowv & (HG_NEAR - 1)) >= dlt
            diff = jnp.where(ok, b - _group_roll(b, dlt), 0.0)
            a_d = jnp.sum(q * _group_roll(kk, dlt) * jnp.exp(diff), axis=-1, keepdims=True)
            a = a + jnp.where((col == row - dlt) & ((row & (HG_NEAR - 1)) >= dlt), a_d, 0.0)
        a_all.append(a)

    for h in heads:
        o = _dot(a_all[h].astype(BF16), vbs[h]) + o_far[h]
        o = o * lax.rsqrt(jnp.mean(o * o, axis=-1, keepdims=True) + EPS)
        o = o * nw_ref[:, sls[h]]
        g = x_ref[:, 1536 + h * HG_D:1536 + (h + 1) * HG_D]
        o_ref[:, sls[h]] = (g * _sigmoid(g) * o).astype(BF16)


def _outproj_kernel(h_ref, oa_ref, ob_ref, oc_ref, w_ref, o_ref):
    acc = _dot(oa_ref[...], w_ref[0:1024, :])
    acc += _dot(ob_ref[...], w_ref[1024:1536, :])
    acc += _dot(oc_ref[...], w_ref[1536:2048, :])
    o_ref[...] = h_ref[...] + acc


def _outproj(h, oa, ob, oc, w, layer):
    t = h.shape[0]
    return pl.pallas_call(
        _outproj_kernel,
        grid=(t // OUT_TM,),
        in_specs=[
            pl.BlockSpec((OUT_TM, D_MODEL), lambda i: (i, 0)),
            pl.BlockSpec((OUT_TM, 1024), lambda i: (i, 0)),
            pl.BlockSpec((OUT_TM, 512), lambda i: (i, 0)),
            pl.BlockSpec((OUT_TM, 512), lambda i: (i, 0)),
            pl.BlockSpec((None, D_MODEL, D_MODEL), lambda i: (layer, 0, 0)),
        ],
        out_specs=pl.BlockSpec((OUT_TM, D_MODEL), lambda i: (i, 0)),
        out_shape=jax.ShapeDtypeStruct((t, D_MODEL), F32),
        compiler_params=_cparams(("parallel",)),
        name="outproj",
    )(h, oa, ob, oc, w)


def _rope_tables(seq):
    inv64 = ROPE_BASE ** (-jnp.arange(0, MLA_ROPE, 2, dtype=F32) / MLA_ROPE)
    ang64 = jnp.arange(seq, dtype=F32)[:, None] * inv64[None, :]
    c, s = jnp.cos(ang64), jnp.sin(ang64)
    z = jnp.zeros_like(s)
    c4 = jnp.concatenate([c, c, c, c], axis=-1)
    sa = jnp.concatenate([-s, z, -s, z], axis=-1)
    sb = jnp.concatenate([z, s, z, s], axis=-1)
    inv128 = ROPE_BASE ** (-jnp.arange(0, RET_D, 2, dtype=F32) / RET_D)
    ang128 = jnp.arange(seq, dtype=F32)[:, None] * inv128[None, :]
    cf, sf = jnp.cos(ang128), jnp.sin(ang128)
    return c4, sa, sb, jnp.concatenate([cf, cf], axis=-1), jnp.concatenate([-sf, sf], axis=-1)


def _win_kernel(wt_ref, o_ref):
    hd = MLA_NOPE + MLA_ROPE
    nq = MLA_HEADS * hd
    tk = o_ref.shape[0]

    def put(col, *row_ranges):
        pieces = [wt_ref[r0:r1, :] for r0, r1 in row_ranges]
        piece = pieces[0] if len(pieces) == 1 else jnp.concatenate(pieces, axis=0)
        o_ref[:, col:col + piece.shape[0]] = piece.T.astype(BF16)

    for h in range(MLA_HEADS):
        put(COL_QN + h * MLA_NOPE, (h * hd, h * hd + MLA_NOPE))
    for p in range(MLA_HEADS // 2):
        put(COL_QR + p * LANES, ((2 * p) * hd + MLA_NOPE, (2 * p + 1) * hd),
            ((2 * p + 1) * hd + MLA_NOPE, (2 * p + 2) * hd))
    for blk in range(KV_RANK // LANES):
        put(COL_CKV + blk * LANES, (nq + blk * LANES, nq + (blk + 1) * LANES))
    src = nq + KV_RANK + MLA_ROPE
    for blk in range((COL_KR - COL_RET) // LANES):
        put(COL_RET + blk * LANES, (src + blk * LANES, src + (blk + 1) * LANES))
    kr = jnp.concatenate([wt_ref[nq + KV_RANK:nq + KV_RANK + MLA_ROPE, :],
                          jnp.zeros((LANES - MLA_ROPE, tk), F32)], axis=0)
    o_ref[:, COL_KR:COL_KR + LANES] = kr.T.astype(BF16)
    o_ref[:, COL_KR + LANES:] = jnp.zeros((tk, D_IN_PAD - COL_KR - LANES), BF16)


def _prep_w_in(w_in):
    depth, d, n = w_in.shape
    return pl.pallas_call(
        _win_kernel,
        grid=(depth, d // WIN_TK),
        in_specs=[pl.BlockSpec((None, n, WIN_TK), lambda l, r: (l, 0, r))],
        out_specs=pl.BlockSpec((None, WIN_TK, D_IN_PAD), lambda l, r: (l, r, 0)),
        out_shape=jax.ShapeDtypeStruct((depth, d, D_IN_PAD), BF16),
        compiler_params=_cparams(("parallel", "parallel")),
        name="w_in_relayout",
    )(jnp.swapaxes(w_in, 1, 2))


def _prep_w_kv(w):
    w = w.reshape(DEPTH, KV_RANK, MLA_HEADS, MLA_NOPE + MLA_V)
    kn = w[..., :MLA_NOPE].reshape(DEPTH, KV_RANK, MLA_HEADS * MLA_NOPE)
    v = w[..., MLA_NOPE:].reshape(DEPTH, KV_RANK, MLA_HEADS * MLA_V)
    return jnp.concatenate([kn, v], axis=-1).astype(BF16)


def kernel(x, ffn1_norm, ffn1_w1, ffn1_w3, ffn1_w2, mix_norm, w_in, mla_kv_norm, mla_w_kv_b,
           mla_out_norm, ret_gn, hgrn_lb_logits, hgrn_out_norm, w_o, ffn2_norm, ffn2_w1,
           ffn2_w3, ffn2_w2, final_norm):
    batch, seq, d = x.shape
    assert d == D_MODEL and seq % ATT_TQ == 0 and seq % ATT_TK == 0 and seq % REC_ROWS == 0
    assert REC_ROWS % RET_C == 0 and RET_C % HG_C == 0
    assert ATT_TK == ATT_TQ
    t = batch * seq
    assert t % PROJ_TM == 0 and t % FFN_TM == 0
    c4, sa, sb, cosf, sinf = _rope_tables(seq)
    w_in_p = _prep_w_in(w_in)
    w_kv_p = _prep_w_kv(mla_w_kv_b)
    row = lambda a: a.reshape(1, -1)

    w_o_b = w_o.astype(BF16)
    ffn_w = (ffn1_w1[0].astype(BF16), ffn1_w3[0].astype(BF16), ffn1_w2[0].astype(BF16))
    h = x.reshape(t, d)
    for l in range(DEPTH):
        h, ffn_w = _ffn(h, row(ffn1_norm[l]), *ffn_w, nxt=(ffn2_w1, ffn2_w3, ffn2_w2, l))
        proj = _inproj(h, row(mix_norm[l]), w_in_p, l)
        kcat, vt = _kv_prep(proj, row(mla_kv_norm[l]), w_kv_p, c4, sa, sb, seq, l)
        oa = _attention(proj, c4, sa, sb, kcat, vt, mla_out_norm[l].reshape(-1, 1), batch, seq)
        ob, oc = _recurrent(proj, cosf, sinf, row(ret_gn[l]), hgrn_lb_logits,
                            row(hgrn_out_norm[l]), l, batch, seq)
        h = _outproj(h, oa, ob, oc, w_o_b, l)
        last = l == DEPTH - 1
        h, ffn_w = _ffn(h, row(ffn2_norm[l]), *ffn_w,
                        nxt=None if last else (ffn1_w1, ffn1_w3, ffn1_w2, l + 1),
                        final_w=row(final_norm) if last else None)
    return h.reshape(batch, seq, d)
```

```python
import functools
import math

import jax
import jax.numpy as jnp
from jax import lax
from jax.experimental import pallas as pl
from jax.experimental.pallas import tpu as pltpu

F32 = jnp.float32
BF16 = jnp.bfloat16

D_MODEL = 2048
DEPTH = 4
MLA_HEADS = 8
MLA_NOPE = 128
MLA_ROPE = 64
MLA_V = 128
KV_RANK = 512
RET_HEADS = 4
RET_D = 128
HG_HEADS = 4
HG_D = 128
D_FF = 5632
ROPE_BASE = 10000.0
EPS = 1e-6
MASK_VALUE = -1e30
MIN_FORGET = 1e-20

LANES = 128
SUBLANES = 8
VMEM_LIMIT = 60 * 1024 * 1024

COL_QN = 0
COL_QR = 1024
COL_CKV = 1536
COL_RET = 2048
COL_HG = 4096
COL_KR = 6144
D_IN_PAD = 6400
IN_TN = 1280

FFN_TM = 1024
FFN_TF = 512
PROJ_TM = 1024
ATT_TQ = 512
ATT_TK = 512
ATT_AHEAD = 2
ATT_ONES = 16
RET_C = 256
HG_C = 128
REC_ROWS = 512
HG_NEAR = 4
OUT_TM = 512
WIN_TK = 256


def _cparams(sem):
    return pltpu.CompilerParams(dimension_semantics=sem, vmem_limit_bytes=VMEM_LIMIT)


def _rms(x, w):
    return (x * lax.rsqrt(jnp.mean(x * x, axis=-1, keepdims=True) + EPS)) * w


def _sigmoid(x):
    return 1.0 / (1.0 + jnp.exp(-x))


def _dot(a, b):
    return jnp.dot(a, b, preferred_element_type=F32)


def _dot_nt(a, b):
    return lax.dot_general(a, b, (((1,), (1,)), ((), ())), preferred_element_type=F32)


def _dot_tn(a, b):
    return lax.dot_general(a, b, (((0,), (0,)), ((), ())), preferred_element_type=F32)


def _ffn_kernel(x_ref, nw_ref, w1_ref, w3_ref, w2_ref, *rest, n_f, final, convert):
    rest = list(rest)
    cast_in = [rest.pop(0) for _ in range(3)] if convert else []
    fw_ref = rest.pop(0) if final else None
    o_ref = rest.pop(0)
    cast_out = [rest.pop(0) for _ in range(3)] if convert else []
    (n_ref,) = rest
    f = pl.program_id(1)

    for src, dst in zip(cast_in, cast_out):
        dst[...] = src[...].astype(BF16)

    @pl.when(f == 0)
    def _():
        x = x_ref[...]
        n_ref[...] = _rms(x, nw_ref[...]).astype(BF16)
        o_ref[...] = x

    n = n_ref[...]
    h1 = _dot(n, w1_ref[...])
    h3 = _dot(n, w3_ref[...])
    g = (h1 * _sigmoid(h1) * h3 * 0.5).astype(BF16)
    o_ref[...] += _dot(g, w2_ref[...])

    if final:
        @pl.when(f == n_f - 1)
        def _():
            o_ref[...] = _rms(o_ref[...], fw_ref[...])


def _ffn(h, nw, w1, w3, w2, nxt=None, final_w=None):
    t = h.shape[0]
    n_i = t // FFN_TM
    n_f = D_FF // FFN_TF
    final = final_w is not None
    convert = nxt is not None
    in_specs = [
        pl.BlockSpec((FFN_TM, D_MODEL), lambda i, f: (i, 0)),
        pl.BlockSpec((1, D_MODEL), lambda i, f: (0, 0)),
        pl.BlockSpec((D_MODEL, FFN_TF), lambda i, f: (0, f)),
        pl.BlockSpec((D_MODEL, FFN_TF), lambda i, f: (0, f)),
        pl.BlockSpec((FFN_TF, D_MODEL), lambda i, f: (f, 0)),
    ]
    args = [h, nw, w1, w3, w2]
    out_specs = [pl.BlockSpec((FFN_TM, D_MODEL), lambda i, f: (i, 0))]
    out_shape = [jax.ShapeDtypeStruct((t, D_MODEL), F32)]
    if convert:
        n1, n3, n2, layer = nxt
        dr = D_MODEL // n_i
        in_specs += [
            pl.BlockSpec((None, dr, FFN_TF), lambda i, f: (layer, i, f)),
            pl.BlockSpec((None, dr, FFN_TF), lambda i, f: (layer, i, f)),
            pl.BlockSpec((None, FFN_TF, dr), lambda i, f: (layer, f, i)),
        ]
        args += [n1, n3, n2]
        out_specs += [
            pl.BlockSpec((dr, FFN_TF), lambda i, f: (i, f)),
            pl.BlockSpec((dr, FFN_TF), lambda i, f: (i, f)),
            pl.BlockSpec((FFN_TF, dr), lambda i, f: (f, i)),
        ]
        out_shape += [
            jax.ShapeDtypeStruct((D_MODEL, D_FF), BF16),
            jax.ShapeDtypeStruct((D_MODEL, D_FF), BF16),
            jax.ShapeDtypeStruct((D_FF, D_MODEL), BF16),
        ]
    if final:
        in_specs.append(pl.BlockSpec((1, D_MODEL), lambda i, f: (0, 0)))
        args.append(final_w)
    outs = pl.pallas_call(
        functools.partial(_ffn_kernel, n_f=n_f, final=final, convert=convert),
        grid=(n_i, n_f),
        in_specs=in_specs,
        out_specs=out_specs,
        out_shape=out_shape,
        scratch_shapes=[pltpu.VMEM((FFN_TM, D_MODEL), BF16)],
        compiler_params=_cparams(("parallel", "arbitrary")),
        name="ffn_final" if final else "ffn",
    )(*args)
    return outs[0], tuple(outs[1:])


def _inproj_kernel(x_ref, nw_ref, w_ref, o_ref, n_ref):
    @pl.when(pl.program_id(1) == 0)
    def _():
        n_ref[...] = _rms(x_ref[...], nw_ref[...]).astype(BF16)

    o_ref[...] = _dot(n_ref[...], w_ref[...])


def _inproj(h, nw, w, layer):
    t = h.shape[0]
    return pl.pallas_call(
        _inproj_kernel,
        grid=(t // PROJ_TM, D_IN_PAD // IN_TN),
        in_specs=[
            pl.BlockSpec((PROJ_TM, D_MODEL), lambda i, j: (i, 0)),
            pl.BlockSpec((1, D_MODEL), lambda i, j: (0, 0)),
            pl.BlockSpec((None, D_MODEL, IN_TN), lambda i, j: (layer, 0, j)),
        ],
        out_specs=pl.BlockSpec((PROJ_TM, IN_TN), lambda i, j: (i, j)),
        out_shape=jax.ShapeDtypeStruct((t, D_IN_PAD), F32),
        scratch_shapes=[pltpu.VMEM((PROJ_TM, D_MODEL), BF16)],
        compiler_params=_cparams(("parallel", "arbitrary")),
        name="inproj",
    )(h, nw, w)


def _rope_pair(p, c4, sa, sb):
    return p * c4 + pltpu.roll(p, 96, 1) * sa + pltpu.roll(p, 32, 1) * sb


def _rope_full(x, c, s):
    return x * c + pltpu.roll(x, 64, 1) * s


def _kv_rows(r0, rn, ckv_ref, kr_ref, nw_ref, w_ref, c4_ref, sa_ref, sb_ref, kcat_ref, vt_ref):
    rows = pl.ds(r0, rn)
    n = _rms(ckv_ref[rows, :], nw_ref[...]).astype(BF16)
    kv = _dot(n, w_ref[...])
    kr = _rope_pair(kr_ref[rows, :], c4_ref[rows, :], sa_ref[rows, :], sb_ref[rows, :])
    kr_lo = kr.astype(BF16)
    kr_hi = pltpu.roll(kr, 64, 1).astype(BF16)
    for h in range(MLA_HEADS):
        kcat_ref[rows, h * 256:h * 256 + 128] = kv[:, h * 128:(h + 1) * 128].astype(BF16)
        kcat_ref[rows, h * 256 + 128:(h + 1) * 256] = kr_lo if h % 2 == 0 else kr_hi
    tile, off = divmod(r0, ATT_TK)
    vt_ref[tile, :, off:off + rn] = kv[:, MLA_HEADS * MLA_NOPE:].T.astype(BF16)


def _attn_kernel(qn_ref, qr_ref, c4_ref, sa_ref, sb_ref, k_ref, vt_ref, nw_ref, o_ref,
                 qt_s, m_s, acc_s):
    i = pl.program_id(1)
    tq, tk = ATT_TQ, ATT_TK
    scale = (MLA_NOPE + MLA_ROPE) ** -0.5 * math.log2(math.e)

    for h in range(MLA_HEADS):
        if h % 2 == 0:
            pair = qr_ref[:, (h // 2) * LANES:(h // 2 + 1) * LANES]
            roped = _rope_pair(pair, c4_ref[...], sa_ref[...], sb_ref[...]) * scale
        qn = qn_ref[:, h * LANES:(h + 1) * LANES] * scale
        qt_s[h] = jnp.concatenate([qn, roped], axis=-1).T.astype(BF16)
    m_s[...] = jnp.full(m_s.shape, MASK_VALUE, F32)
    acc_s[...] = jnp.zeros(acc_s.shape, F32)

    def tile(j, masked):
        ones = jnp.ones((ATT_ONES, tk), BF16)

        def scores(h):
            return _dot(k_ref[j, :, h * 256:(h + 1) * 256], qt_s[h])

        pending = [scores(h) for h in range(ATT_AHEAD)]
        for h in range(MLA_HEADS):
            st = pending.pop(0)
            if h + ATT_AHEAD < MLA_HEADS:
                pending.append(scores(h + ATT_AHEAD))
            if masked:
                key = lax.broadcasted_iota(jnp.int32, (tk, tq), 0)
                qry = lax.broadcasted_iota(jnp.int32, (tk, tq), 1)
                st = jnp.where(key <= qry, st, MASK_VALUE)
            m_old = m_s[h]
            m_new = jnp.maximum(m_old, jnp.max(st, axis=0, keepdims=True))
            p = jnp.exp2(st - m_new).astype(BF16)
            alpha = jnp.exp2(m_old - m_new)
            vext = jnp.concatenate([vt_ref[j, h * LANES:(h + 1) * LANES, :], ones], axis=0)
            acc_s[h] = alpha * acc_s[h] + _dot(vext, p)
            m_s[h] = m_new

    def body(j, carry):
        tile(j, False)
        return carry

    lax.fori_loop(0, i, body, 0)
    tile(i, True)

    for h in range(MLA_HEADS):
        acc = acc_s[h]
        ot = acc[:MLA_V, :] / acc[MLA_V:MLA_V + 1, :]
        ot = ot * lax.rsqrt(jnp.mean(ot * ot, axis=0, keepdims=True) + EPS)
        ot = ot * nw_ref[h * LANES:(h + 1) * LANES, :]
        o_ref[:, h * LANES:(h + 1) * LANES] = ot.T.astype(BF16)


def _attention(proj, c4, sa, sb, kcat, vt, nw_col, batch, seq):
    t = proj.shape[0]
    nq = seq // ATT_TQ
    nk = seq // ATT_TK
    tab = pl.BlockSpec((ATT_TQ, LANES), lambda b, i: (i, 0))
    return pl.pallas_call(
        _attn_kernel,
        grid=(batch, nq),
        in_specs=[
            pl.BlockSpec((ATT_TQ, 1024), lambda b, i: (b * nq + i, COL_QN // 1024)),
            pl.BlockSpec((ATT_TQ, 512), lambda b, i: (b * nq + i, COL_QR // 512)),
            tab, tab, tab,
            pl.BlockSpec((nk, ATT_TK, 2048), lambda b, i: (b, 0, 0)),
            pl.BlockSpec((nk, 1024, ATT_TK), lambda b, i: (b, 0, 0)),
            pl.BlockSpec((1024, 1), lambda b, i: (0, 0)),
        ],
        out_specs=pl.BlockSpec((ATT_TQ, 1024), lambda b, i: (b * nq + i, 0)),
        out_shape=jax.ShapeDtypeStruct((t, 1024), BF16),
        scratch_shapes=[
            pltpu.VMEM((MLA_HEADS, 256, ATT_TQ), BF16),
            pltpu.VMEM((MLA_HEADS, 1, ATT_TQ), F32),
            pltpu.VMEM((MLA_HEADS, MLA_V + ATT_ONES, ATT_TQ), F32),
        ],
        compiler_params=_cparams(("parallel", "arbitrary")),
        name="mla_attention",
    )(proj, proj, c4, sa, sb, kcat.reshape(t // ATT_TK, ATT_TK, 2048), vt, nw_col)


def _ret_chunk(x_ref, c_ref, s_ref, gw_ref, o_ref, state_ref, decay_ref, lgs):
    c = RET_C
    idx = lax.broadcasted_iota(jnp.int32, (c, 1), 0).astype(F32)
    cosv = c_ref[...]
    sinv = s_ref[...]
    heads = range(RET_HEADS)
    sls = [slice(h * RET_D, (h + 1) * RET_D) for h in heads]

    qbs, vbs, scores, far = [], [], [], []
    for h in heads:
        q = _rope_full(x_ref[:, sls[h]], cosv, sinv)
        k = _rope_full(x_ref[:, 512 + h * RET_D:512 + (h + 1) * RET_D], cosv, sinv) * (RET_D ** -0.5)
        qb = q.astype(BF16)
        vb = x_ref[:, 1024 + h * RET_D:1024 + (h + 1) * RET_D].astype(BF16)
        state = state_ref[h]
        scores.append(_dot_nt(qb, k.astype(BF16)))
        far.append(_dot(qb, state.astype(BF16)))
        zeta = jnp.exp(lgs[h] * (c - 1.0 - idx))
        state_ref[h] = state * math.exp(lgs[h] * c) + _dot_tn((k * zeta).astype(BF16), vb)
        qbs.append(qb)
        vbs.append(vb)

    outs = []
    for h in heads:
        xi = jnp.exp(lgs[h] * (idx + 1.0))
        outs.append(_dot((scores[h] * decay_ref[h]).astype(BF16), vbs[h]) + far[h] * xi)

    for h in heads:
        o = outs[h]
        mu = jnp.mean(o, axis=-1, keepdims=True)
        d = o - mu
        var = jnp.mean(d * d, axis=-1, keepdims=True)
        o = d * lax.rsqrt(var + EPS) * gw_ref[:, sls[h]]
        g = x_ref[:, 1536 + h * RET_D:1536 + (h + 1) * RET_D]
        o_ref[:, sls[h]] = (g * _sigmoid(g) * o).astype(BF16)


def _group_roll(x, shift):
    n, w = x.shape
    return pltpu.roll(x.reshape(n // SUBLANES, SUBLANES, w), shift, 1).reshape(n, w)


def _rec_kernel(xr_ref, xh_ref, c_ref, s_ref, gw_ref, lbl_ref, nw_ref,
                ckv_ref, kr_ref, kvw_ref, wkv_ref, c4_ref, sa_ref, sb_ref,
                ob_ref, oc_ref, kcat_ref, vt_ref,
                rstate_ref, decay_ref, hstate_ref, *, layer):
    lgs = [math.log1p(-(2.0 ** (-5.0 - h))) for h in range(RET_HEADS)]

    @pl.when(pl.program_id(1) == 0)
    def _():
        hstate_ref[...] = jnp.zeros_like(hstate_ref)
        rstate_ref[...] = jnp.zeros_like(rstate_ref)
        row = lax.broadcasted_iota(jnp.int32, (RET_C, RET_C), 0)
        col = lax.broadcasted_iota(jnp.int32, (RET_C, RET_C), 1)
        rel = (row - col).astype(F32)
        for h in range(RET_HEADS):
            decay_ref[h] = jnp.where(rel >= 0, jnp.exp(lgs[h] * jnp.maximum(rel, 0.0)), 0.0)

    logits = lbl_ref[...]
    e = jnp.exp(logits - jnp.max(logits, axis=0, keepdims=True))
    p = e / jnp.sum(e, axis=0, keepdims=True)
    lb_all = jnp.zeros((1, HG_HEADS * HG_D), F32)
    for m in range(layer + 1):
        lb_all = lb_all + p[m:m + 1, :]
    lb_all = lb_all - p[0:1, :]

    hg_per_ret = RET_C // HG_C
    for s in range(REC_ROWS // RET_C):
        rows = pl.ds(s * RET_C, RET_C)
        _ret_chunk(xr_ref.at[rows], c_ref.at[rows], s_ref.at[rows], gw_ref, ob_ref.at[rows],
                   rstate_ref, decay_ref, lgs)
        _kv_rows(s * RET_C, RET_C, ckv_ref, kr_ref, kvw_ref, wkv_ref, c4_ref, sa_ref, sb_ref,
                 kcat_ref, vt_ref)
        for u in range(hg_per_ret):
            rows = pl.ds((s * hg_per_ret + u) * HG_C, HG_C)
            _hgrn_chunk(xh_ref.at[rows], lb_all, nw_ref, oc_ref.at[rows], hstate_ref)


def _recurrent(proj, cosf, sinf, gw, lb_logits, nw, kv_nw, w_kv, c4, sa, sb, layer, batch, seq):
    t = proj.shape[0]
    nc = seq // REC_ROWS
    n_vt = REC_ROWS // ATT_TK
    tab = pl.BlockSpec((REC_ROWS, LANES), lambda b, c: (c, 0))
    out = pl.BlockSpec((REC_ROWS, 512), lambda b, c: (b * nc + c, 0))
    return pl.pallas_call(
        functools.partial(_rec_kernel, layer=layer),
        grid=(batch, nc),
        in_specs=[
            pl.BlockSpec((REC_ROWS, 2048), lambda b, c: (b * nc + c, COL_RET // 2048)),
            pl.BlockSpec((REC_ROWS, 2048), lambda b, c: (b * nc + c, COL_HG // 2048)),
            tab, tab,
            pl.BlockSpec((1, 512), lambda b, c: (0, 0)),
            pl.BlockSpec((DEPTH, 512), lambda b, c: (0, 0)),
            pl.BlockSpec((1, 512), lambda b, c: (0, 0)),
            pl.BlockSpec((REC_ROWS, KV_RANK), lambda b, c: (b * nc + c, COL_CKV // KV_RANK)),
            pl.BlockSpec((REC_ROWS, LANES), lambda b, c: (b * nc + c, COL_KR // LANES)),
            pl.BlockSpec((1, KV_RANK), lambda b, c: (0, 0)),
            pl.BlockSpec((None, KV_RANK, 2048), lambda b, c: (layer, 0, 0)),
            tab, tab, tab,
        ],
        out_specs=[
            out, out,
            pl.BlockSpec((REC_ROWS, 2048), lambda b, c: (b * nc + c, 0)),
            pl.BlockSpec((n_vt, 1024, ATT_TK), lambda b, c: (b * nc + c, 0, 0)),
        ],
        out_shape=[
            jax.ShapeDtypeStruct((t, 512), BF16),
            jax.ShapeDtypeStruct((t, 512), BF16),
            jax.ShapeDtypeStruct((t, 2048), BF16),
            jax.ShapeDtypeStruct((t // ATT_TK, 1024, ATT_TK), BF16),
        ],
        scratch_shapes=[pltpu.VMEM((RET_HEADS, RET_D, RET_D), F32),
                        pltpu.VMEM((RET_HEADS, RET_C, RET_C), F32),
                        pltpu.VMEM((HG_HEADS, HG_D, HG_D), F32)],
        compiler_params=_cparams(("parallel", "arbitrary")),
        name="kv_retention_hgrn2",
    )(proj, proj, cosf, sinf, gw, lb_logits, nw, proj, proj, kv_nw, w_kv, c4, sa, sb)


def _hgrn_chunk(x_ref, lb_all, nw_ref, o_ref, state_ref):
    c = HG_C
    rowv = lax.broadcasted_iota(jnp.int32, (c, LANES), 0)
    row = lax.broadcasted_iota(jnp.int32, (c, c), 0)
    col = lax.broadcasted_iota(jnp.int32, (c, c), 1)

    z = x_ref[:, 512:1024]
    ez = jnp.exp(-jnp.abs(z))
    r = 1.0 / (1.0 + ez)
    pos = z >= 0
    sig_p = jnp.where(pos, r, ez * r)
    sig_n = jnp.where(pos, ez * r, r)
    f = lb_all + (1.0 - lb_all) * sig_p
    lf = jnp.log(jnp.maximum(f, MIN_FORGET))
    kk_all = (1.0 - lb_all) * sig_n

    tri = jnp.where(col <= row, 1.0, 0.0).astype(BF16)
    lf_hi = lf.astype(BF16)
    rem = lf - lf_hi.astype(F32)
    lf_mid = rem.astype(BF16)
    lf_lo = (rem - lf_mid.astype(F32)).astype(BF16)
    b_all = _dot(tri, lf_hi) + _dot(tri, lf_mid) + _dot(tri, lf_lo)

    heads = range(HG_HEADS)
    sls = [slice(h * HG_D, (h + 1) * HG_D) for h in heads]
    qs = [x_ref[:, sls[h]] for h in heads]
    kks = [kk_all[:, sls[h]] for h in heads]
    bs = [b_all[:, sls[h]] for h in heads]
    vbs = [x_ref[:, 1024 + h * HG_D:1024 + (h + 1) * HG_D].astype(BF16) for h in heads]


    o_far = []
    for h in heads:
        state = state_ref[h]
        b_last = bs[h][c - 1:c, :]
        o_far.append(_dot_nt((qs[h] * jnp.exp(bs[h])).astype(BF16), state.astype(BF16)))
        upd = _dot_tn(vbs[h], (kks[h] * jnp.exp(b_last - bs[h])).astype(BF16))
        state_ref[h] = jnp.exp(b_last) * state + upd

    a_lvl = []
    for h in heads:
        q, kk, b = qs[h], kks[h], bs[h]
        a = None
        m = HG_NEAR
        while m < c:
            parts = []
            for blk in range(c // (2 * m)):
                lo = blk * 2 * m
                ref = b[lo + m - 1:lo + m, :]
                parts.append(b[lo:lo + 2 * m, :] - ref)
            d = parts[0] if len(parts) == 1 else jnp.concatenate(parts, axis=0)
            second = (rowv & (2 * m - 1)) >= m
            efac = jnp.exp(jnp.where(second, d, -d))
            ql = jnp.where(second, q * efac, 0.0).astype(BF16)
            kl = jnp.where(second, 0.0, kk * efac).astype(BF16)
            al = _dot_nt(ql, kl)
            if 2 * m < c:
                sft = (2 * m).bit_length() - 1
                al = jnp.where((row >> sft) == (col >> sft), al, 0.0)
            a = al if a is None else a + al
            m *= 2
        a_lvl.append(a)

    a_all = []
    for h in heads:
        q, kk, b = qs[h], kks[h], bs[h]
        a = a_lvl[h] + jnp.where(col == row, jnp.sum(q * kk, axis=-1, keepdims=True), 0.0)
        for dlt in range(1, HG_NEAR):
            ok = (rowv & (HG_NEAR - 1)) >= dlt
            diff = jnp.where(ok, b - _group_roll(b, dlt), 0.0)
            a_d = jnp.sum(q * _group_roll(kk, dlt) * jnp.exp(diff), axis=-1, keepdims=True)
            a = a + jnp.where((col == row - dlt) & ((row & (HG_NEAR - 1)) >= dlt), a_d, 0.0)
        a_all.append(a)

    for h in heads:
        o = _dot(a_all[h].astype(BF16), vbs[h]) + o_far[h]
        o = o * lax.rsqrt(jnp.mean(o * o, axis=-1, keepdims=True) + EPS)
        o = o * nw_ref[:, sls[h]]
        g = x_ref[:, 1536 + h * HG_D:1536 + (h + 1) * HG_D]
        o_ref[:, sls[h]] = (g * _sigmoid(g) * o).astype(BF16)


def _outproj_kernel(h_ref, oa_ref, ob_ref, oc_ref, w_ref, o_ref):
    acc = _dot(oa_ref[...], w_ref[0:1024, :])
    acc += _dot(ob_ref[...], w_ref[1024:1536, :])
    acc += _dot(oc_ref[...], w_ref[1536:2048, :])
    o_ref[...] = h_ref[...] + acc


def _outproj(h, oa, ob, oc, w, layer):
    t = h.shape[0]
    return pl.pallas_call(
        _outproj_kernel,
        grid=(t // OUT_TM,),
        in_specs=[
            pl.BlockSpec((OUT_TM, D_MODEL), lambda i: (i, 0)),
            pl.BlockSpec((OUT_TM, 1024), lambda i: (i, 0)),
            pl.BlockSpec((OUT_TM, 512), lambda i: (i, 0)),
            pl.BlockSpec((OUT_TM, 512), lambda i: (i, 0)),
            pl.BlockSpec((None, D_MODEL, D_MODEL), lambda i: (layer, 0, 0)),
        ],
        out_specs=pl.BlockSpec((OUT_TM, D_MODEL), lambda i: (i, 0)),
        out_shape=jax.ShapeDtypeStruct((t, D_MODEL), F32),
        compiler_params=_cparams(("parallel",)),
        name="outproj",
    )(h, oa, ob, oc, w)


def _rope_tables(seq):
    inv64 = ROPE_BASE ** (-jnp.arange(0, MLA_ROPE, 2, dtype=F32) / MLA_ROPE)
    ang64 = jnp.arange(seq, dtype=F32)[:, None] * inv64[None, :]
    c, s = jnp.cos(ang64), jnp.sin(ang64)
    z = jnp.zeros_like(s)
    c4 = jnp.concatenate([c, c, c, c], axis=-1)
    sa = jnp.concatenate([-s, z, -s, z], axis=-1)
    sb = jnp.concatenate([z, s, z, s], axis=-1)
    inv128 = ROPE_BASE ** (-jnp.arange(0, RET_D, 2, dtype=F32) / RET_D)
    ang128 = jnp.arange(seq, dtype=F32)[:, None] * inv128[None, :]
    cf, sf = jnp.cos(ang128), jnp.sin(ang128)
    return c4, sa, sb, jnp.concatenate([cf, cf], axis=-1), jnp.concatenate([-sf, sf], axis=-1)


def _win_kernel(wt_ref, o_ref):
    hd = MLA_NOPE + MLA_ROPE
    nq = MLA_HEADS * hd
    tk = o_ref.shape[0]

    def put(col, *row_ranges):
        pieces = [wt_ref[r0:r1, :] for r0, r1 in row_ranges]
        piece = pieces[0] if len(pieces) == 1 else jnp.concatenate(pieces, axis=0)
        o_ref[:, col:col + piece.shape[0]] = piece.T.astype(BF16)

    for h in range(MLA_HEADS):
        put(COL_QN + h * MLA_NOPE, (h * hd, h * hd + MLA_NOPE))
    for p in range(MLA_HEADS // 2):
        put(COL_QR + p * LANES, ((2 * p) * hd + MLA_NOPE, (2 * p + 1) * hd),
            ((2 * p + 1) * hd + MLA_NOPE, (2 * p + 2) * hd))
    for blk in range(KV_RANK // LANES):
        put(COL_CKV + blk * LANES, (nq + blk * LANES, nq + (blk + 1) * LANES))
    src = nq + KV_RANK + MLA_ROPE
    for blk in range((COL_KR - COL_RET) // LANES):
        put(COL_RET + blk * LANES, (src + blk * LANES, src + (blk + 1) * LANES))
    kr = jnp.concatenate([wt_ref[nq + KV_RANK:nq + KV_RANK + MLA_ROPE, :],
                          jnp.zeros((LANES - MLA_ROPE, tk), F32)], axis=0)
    o_ref[:, COL_KR:COL_KR + LANES] = kr.T.astype(BF16)
    o_ref[:, COL_KR + LANES:] = jnp.zeros((tk, D_IN_PAD - COL_KR - LANES), BF16)


def _prep_w_in(w_in):
    depth, d, n = w_in.shape
    return pl.pallas_call(
        _win_kernel,
        grid=(depth, d // WIN_TK),
        in_specs=[pl.BlockSpec((None, n, WIN_TK), lambda l, r: (l, 0, r))],
        out_specs=pl.BlockSpec((None, WIN_TK, D_IN_PAD), lambda l, r: (l, r, 0)),
        out_shape=jax.ShapeDtypeStruct((depth, d, D_IN_PAD), BF16),
        compiler_params=_cparams(("parallel", "parallel")),
        name="w_in_relayout",
    )(jnp.swapaxes(w_in, 1, 2))


def _prep_w_kv(w):
    w = w.reshape(DEPTH, KV_RANK, MLA_HEADS, MLA_NOPE + MLA_V)
    kn = w[..., :MLA_NOPE].reshape(DEPTH, KV_RANK, MLA_HEADS * MLA_NOPE)
    v = w[..., MLA_NOPE:].reshape(DEPTH, KV_RANK, MLA_HEADS * MLA_V)
    return jnp.concatenate([kn, v], axis=-1).astype(BF16)


def kernel(x, ffn1_norm, ffn1_w1, ffn1_w3, ffn1_w2, mix_norm, w_in, mla_kv_norm, mla_w_kv_b,
           mla_out_norm, ret_gn, hgrn_lb_logits, hgrn_out_norm, w_o, ffn2_norm, ffn2_w1,
           ffn2_w3, ffn2_w2, final_norm):
    batch, seq, d = x.shape
    assert d == D_MODEL and seq % ATT_TQ == 0 and seq % ATT_TK == 0 and seq % REC_ROWS == 0
    assert REC_ROWS % RET_C == 0 and RET_C % HG_C == 0 and REC_ROWS % ATT_TK == 0
    assert ATT_TK == ATT_TQ
    t = batch * seq
    assert t % PROJ_TM == 0 and t % FFN_TM == 0
    c4, sa, sb, cosf, sinf = _rope_tables(seq)
    w_in_p = _prep_w_in(w_in)
    w_kv_p = _prep_w_kv(mla_w_kv_b)
    row = lambda a: a.reshape(1, -1)

    w_o_b = w_o.astype(BF16)
    ffn_w = (ffn1_w1[0].astype(BF16), ffn1_w3[0].astype(BF16), ffn1_w2[0].astype(BF16))
    h = x.reshape(t, d)
    for l in range(DEPTH):
        h, ffn_w = _ffn(h, row(ffn1_norm[l]), *ffn_w, nxt=(ffn2_w1, ffn2_w3, ffn2_w2, l))
        proj = _inproj(h, row(mix_norm[l]), w_in_p, l)
        ob, oc, kcat, vt = _recurrent(proj, cosf, sinf, row(ret_gn[l]), hgrn_lb_logits,
                                      row(hgrn_out_norm[l]), row(mla_kv_norm[l]), w_kv_p,
                                      c4, sa, sb, l, batch, seq)
        oa = _attention(proj, c4, sa, sb, kcat, vt, mla_out_norm[l].reshape(-1, 1), batch, seq)
        h = _outproj(h, oa, ob, oc, w_o_b, l)
        last = l == DEPTH - 1
        h, ffn_w = _ffn(h, row(ffn2_norm[l]), *ffn_w,
                        nxt=None if last else (ffn1_w1, ffn1_w3, ffn1_w2, l + 1),
                        final_w=row(final_norm) if last else None)
    return h.reshape(batch, seq, d)
```

```python
import functools
import math

import jax
import jax.numpy as jnp
from jax import lax
from jax.experimental import pallas as pl
from jax.experimental.pallas import tpu as pltpu

F32 = jnp.float32
BF16 = jnp.bfloat16

D_MODEL = 2048
DEPTH = 4
MLA_HEADS = 8
MLA_NOPE = 128
MLA_ROPE = 64
MLA_V = 128
KV_RANK = 512
RET_HEADS = 4
RET_D = 128
HG_HEADS = 4
HG_D = 128
D_FF = 5632
ROPE_BASE = 10000.0
EPS = 1e-6
MASK_VALUE = -1e30
MIN_FORGET = 1e-20

LANES = 128
SUBLANES = 8
VMEM_LIMIT = 60 * 1024 * 1024

MLA_QN = MLA_HEADS * MLA_NOPE
MLA_QR = MLA_HEADS * MLA_ROPE
MLA_OUT = MLA_HEADS * MLA_V
KV_W = MLA_HEADS * (MLA_NOPE + MLA_V)
KCAT_HEAD = 2 * LANES
KCAT_W = MLA_HEADS * KCAT_HEAD
RET_OUT = RET_HEADS * RET_D
HG_OUT = HG_HEADS * HG_D
MIX_W = 4 * RET_OUT
assert RET_OUT == HG_OUT

COL_QN = 0
COL_QR = COL_QN + MLA_QN
COL_CKV = COL_QR + MLA_QR
COL_RET = COL_CKV + KV_RANK
COL_HG = COL_RET + MIX_W
COL_KR = COL_HG + MIX_W
IN_TN = 1280
D_IN_PAD = 5 * IN_TN
assert D_IN_PAD >= COL_KR + LANES

FFN_TM = 1024
FFN_TF = 512
PROJ_TM = 1024
ATT_TQ = 512
ATT_TK = 512
ATT_AHEAD = 2
ATT_ONES = 16
RET_C = 256
HG_C = 128
REC_ROWS = 512
HG_NEAR = 4
OUT_TM = 512
WIN_TK = 256


def _cparams(sem):
    return pltpu.CompilerParams(dimension_semantics=sem, vmem_limit_bytes=VMEM_LIMIT)


def _rms(x, w):
    return (x * lax.rsqrt(jnp.mean(x * x, axis=-1, keepdims=True) + EPS)) * w


def _sigmoid(x):
    return 1.0 / (1.0 + jnp.exp(-x))


def _dot(a, b):
    return jnp.dot(a, b, preferred_element_type=F32)


def _dot_nt(a, b):
    return lax.dot_general(a, b, (((1,), (1,)), ((), ())), preferred_element_type=F32)


def _dot_tn(a, b):
    return lax.dot_general(a, b, (((0,), (0,)), ((), ())), preferred_element_type=F32)


def _ffn_kernel(x_ref, nw_ref, w1_ref, w3_ref, w2_ref, *rest, n_f, final, convert):
    rest = list(rest)
    cast_in = [rest.pop(0) for _ in range(3)] if convert else []
    fw_ref = rest.pop(0) if final else None
    o_ref = rest.pop(0)
    cast_out = [rest.pop(0) for _ in range(3)] if convert else []
    (n_ref,) = rest
    f = pl.program_id(1)

    for src, dst in zip(cast_in, cast_out):
        dst[...] = src[...].astype(BF16)

    @pl.when(f == 0)
    def _():
        x = x_ref[...]
        n_ref[...] = _rms(x, nw_ref[...]).astype(BF16)
        o_ref[...] = x

    n = n_ref[...]
    h1 = _dot(n, w1_ref[...])
    h3 = _dot(n, w3_ref[...])
    g = (h1 * _sigmoid(h1) * h3 * 0.5).astype(BF16)
    o_ref[...] += _dot(g, w2_ref[...])

    if final:
        @pl.when(f == n_f - 1)
        def _():
            o_ref[...] = _rms(o_ref[...], fw_ref[...])


def _ffn(h, nw, w1, w3, w2, nxt=None, final_w=None):
    t = h.shape[0]
    n_i = t // FFN_TM
    n_f = D_FF // FFN_TF
    final = final_w is not None
    convert = nxt is not None
    in_specs = [
        pl.BlockSpec((FFN_TM, D_MODEL), lambda i, f: (i, 0)),
        pl.BlockSpec((1, D_MODEL), lambda i, f: (0, 0)),
        pl.BlockSpec((D_MODEL, FFN_TF), lambda i, f: (0, f)),
        pl.BlockSpec((D_MODEL, FFN_TF), lambda i, f: (0, f)),
        pl.BlockSpec((FFN_TF, D_MODEL), lambda i, f: (f, 0)),
    ]
    args = [h, nw, w1, w3, w2]
    out_specs = [pl.BlockSpec((FFN_TM, D_MODEL), lambda i, f: (i, 0))]
    out_shape = [jax.ShapeDtypeStruct((t, D_MODEL), F32)]
    if convert:
        n1, n3, n2, layer = nxt
        dr = D_MODEL // n_i
        in_specs += [
            pl.BlockSpec((None, dr, FFN_TF), lambda i, f: (layer, i, f)),
            pl.BlockSpec((None, dr, FFN_TF), lambda i, f: (layer, i, f)),
            pl.BlockSpec((None, FFN_TF, dr), lambda i, f: (layer, f, i)),
        ]
        args += [n1, n3, n2]
        out_specs += [
            pl.BlockSpec((dr, FFN_TF), lambda i, f: (i, f)),
            pl.BlockSpec((dr, FFN_TF), lambda i, f: (i, f)),
            pl.BlockSpec((FFN_TF, dr), lambda i, f: (f, i)),
        ]
        out_shape += [
            jax.ShapeDtypeStruct((D_MODEL, D_FF), BF16),
            jax.ShapeDtypeStruct((D_MODEL, D_FF), BF16),
            jax.ShapeDtypeStruct((D_FF, D_MODEL), BF16),
        ]
    if final:
        in_specs.append(pl.BlockSpec((1, D_MODEL), lambda i, f: (0, 0)))
        args.append(final_w)
    outs = pl.pallas_call(
        functools.partial(_ffn_kernel, n_f=n_f, final=final, convert=convert),
        grid=(n_i, n_f),
        in_specs=in_specs,
        out_specs=out_specs,
        out_shape=out_shape,
        scratch_shapes=[pltpu.VMEM((FFN_TM, D_MODEL), BF16)],
        compiler_params=_cparams(("parallel", "arbitrary")),
        name="ffn_final" if final else "ffn",
    )(*args)
    return outs[0], tuple(outs[1:])


def _inproj_kernel(x_ref, nw_ref, w_ref, o_ref, n_ref):
    @pl.when(pl.program_id(1) == 0)
    def _():
        n_ref[...] = _rms(x_ref[...], nw_ref[...]).astype(BF16)

    o_ref[...] = _dot(n_ref[...], w_ref[...])


def _inproj(h, nw, w, layer):
    t = h.shape[0]
    return pl.pallas_call(
        _inproj_kernel,
        grid=(t // PROJ_TM, D_IN_PAD // IN_TN),
        in_specs=[
            pl.BlockSpec((PROJ_TM, D_MODEL), lambda i, j: (i, 0)),
            pl.BlockSpec((1, D_MODEL), lambda i, j: (0, 0)),
            pl.BlockSpec((None, D_MODEL, IN_TN), lambda i, j: (layer, 0, j)),
        ],
        out_specs=pl.BlockSpec((PROJ_TM, IN_TN), lambda i, j: (i, j)),
        out_shape=jax.ShapeDtypeStruct((t, D_IN_PAD), F32),
        scratch_shapes=[pltpu.VMEM((PROJ_TM, D_MODEL), BF16)],
        compiler_params=_cparams(("parallel", "arbitrary")),
        name="inproj",
    )(h, nw, w)


def _rope_pair(p, c4, sa, sb):
    return p * c4 + pltpu.roll(p, 96, 1) * sa + pltpu.roll(p, 32, 1) * sb


def _rope_full(x, c, s):
    return x * c + pltpu.roll(x, 64, 1) * s


def _kv_rows(r0, rn, ckv_ref, kr_ref, nw_ref, w_ref, c4_ref, sa_ref, sb_ref, kcat_ref, vt_ref):
    rows = pl.ds(r0, rn)
    n = _rms(ckv_ref[rows, :], nw_ref[...]).astype(BF16)
    kv = _dot(n, w_ref[...])
    kr = _rope_pair(kr_ref[rows, :], c4_ref[rows, :], sa_ref[rows, :], sb_ref[rows, :])
    kr_lo = kr.astype(BF16)
    kr_hi = pltpu.roll(kr, MLA_ROPE, 1).astype(BF16)
    for h in range(MLA_HEADS):
        k0 = h * KCAT_HEAD
        kcat_ref[rows, k0:k0 + MLA_NOPE] = kv[:, h * MLA_NOPE:(h + 1) * MLA_NOPE].astype(BF16)
        kcat_ref[rows, k0 + MLA_NOPE:k0 + KCAT_HEAD] = kr_lo if h % 2 == 0 else kr_hi
    tile, off = divmod(r0, ATT_TK)
    vt_ref[tile, :, off:off + rn] = kv[:, MLA_QN:].T.astype(BF16)


def _attn_kernel(qn_ref, qr_ref, c4_ref, sa_ref, sb_ref, k_ref, vt_ref, nw_ref, o_ref,
                 qt_s, m_s, acc_s):
    i = pl.program_id(1)
    tq, tk = ATT_TQ, ATT_TK
    scale = (MLA_NOPE + MLA_ROPE) ** -0.5 * math.log2(math.e)

    for h in range(MLA_HEADS):
        if h % 2 == 0:
            pair = qr_ref[:, (h // 2) * LANES:(h // 2 + 1) * LANES]
            roped = _rope_pair(pair, c4_ref[...], sa_ref[...], sb_ref[...]) * scale
        qn = qn_ref[:, h * LANES:(h + 1) * LANES] * scale
        qt_s[h] = jnp.concatenate([qn, roped], axis=-1).T.astype(BF16)
    m_s[...] = jnp.full(m_s.shape, MASK_VALUE, F32)
    acc_s[...] = jnp.zeros(acc_s.shape, F32)

    def tile(j, masked):
        ones = jnp.ones((ATT_ONES, tk), BF16)

        def scores(h):
            return _dot(k_ref[j, :, h * KCAT_HEAD:(h + 1) * KCAT_HEAD], qt_s[h])

        pending = [scores(h) for h in range(ATT_AHEAD)]
        for h in range(MLA_HEADS):
            st = pending.pop(0)
            if h + ATT_AHEAD < MLA_HEADS:
                pending.append(scores(h + ATT_AHEAD))
            if masked:
                key = lax.broadcasted_iota(jnp.int32, (tk, tq), 0)
                qry = lax.broadcasted_iota(jnp.int32, (tk, tq), 1)
                st = jnp.where(key <= qry, st, MASK_VALUE)
            m_old = m_s[h]
            m_new = jnp.maximum(m_old, jnp.max(st, axis=0, keepdims=True))
            p = jnp.exp2(st - m_new).astype(BF16)
            alpha = jnp.exp2(m_old - m_new)
            vext = jnp.concatenate([vt_ref[j, h * MLA_V:(h + 1) * MLA_V, :], ones], axis=0)
            acc_s[h] = alpha * acc_s[h] + _dot(vext, p)
            m_s[h] = m_new

    def body(j, carry):
        tile(j, False)
        return carry

    lax.fori_loop(0, i, body, 0)
    tile(i, True)

    for h in range(MLA_HEADS):
        acc = acc_s[h]
        ot = acc[:MLA_V, :] / acc[MLA_V:MLA_V + 1, :]
        ot = ot * lax.rsqrt(jnp.mean(ot * ot, axis=0, keepdims=True) + EPS)
        ot = ot * nw_ref[h * MLA_V:(h + 1) * MLA_V, :]
        o_ref[:, h * MLA_V:(h + 1) * MLA_V] = ot.T.astype(BF16)


def _attention(proj, c4, sa, sb, kcat, vt, nw_col, batch, seq):
    t = proj.shape[0]
    nq = seq // ATT_TQ
    nk = seq // ATT_TK
    tab = pl.BlockSpec((ATT_TQ, LANES), lambda b, i: (i, 0))
    return pl.pallas_call(
        _attn_kernel,
        grid=(batch, nq),
        in_specs=[
            pl.BlockSpec((ATT_TQ, MLA_QN), lambda b, i: (b * nq + i, COL_QN // MLA_QN)),
            pl.BlockSpec((ATT_TQ, MLA_QR), lambda b, i: (b * nq + i, COL_QR // MLA_QR)),
            tab, tab, tab,
            pl.BlockSpec((nk, ATT_TK, KCAT_W), lambda b, i: (b, 0, 0)),
            pl.BlockSpec((nk, MLA_OUT, ATT_TK), lambda b, i: (b, 0, 0)),
            pl.BlockSpec((MLA_OUT, 1), lambda b, i: (0, 0)),
        ],
        out_specs=pl.BlockSpec((ATT_TQ, MLA_OUT), lambda b, i: (b * nq + i, 0)),
        out_shape=jax.ShapeDtypeStruct((t, MLA_OUT), BF16),
        scratch_shapes=[
            pltpu.VMEM((MLA_HEADS, KCAT_HEAD, ATT_TQ), BF16),
            pltpu.VMEM((MLA_HEADS, 1, ATT_TQ), F32),
            pltpu.VMEM((MLA_HEADS, MLA_V + ATT_ONES, ATT_TQ), F32),
        ],
        compiler_params=_cparams(("parallel", "arbitrary")),
        name="mla_attention",
    )(proj, proj, c4, sa, sb, kcat.reshape(t // ATT_TK, ATT_TK, KCAT_W), vt, nw_col)


def _ret_chunk(x_ref, c_ref, s_ref, gw_ref, o_ref, state_ref, decay_ref, lgs):
    c = RET_C
    idx = lax.broadcasted_iota(jnp.int32, (c, 1), 0).astype(F32)
    cosv = c_ref[...]
    sinv = s_ref[...]
    heads = range(RET_HEADS)
    sls = [slice(h * RET_D, (h + 1) * RET_D) for h in heads]

    qbs, vbs, scores, far = [], [], [], []
    for h in heads:
        q = _rope_full(x_ref[:, sls[h]], cosv, sinv)
        k = _rope_full(x_ref[:, RET_OUT + h * RET_D:RET_OUT + (h + 1) * RET_D], cosv, sinv)
        k = k * (RET_D ** -0.5)
        qb = q.astype(BF16)
        vb = x_ref[:, 2 * RET_OUT + h * RET_D:2 * RET_OUT + (h + 1) * RET_D].astype(BF16)
        state = state_ref[h]
        scores.append(_dot_nt(qb, k.astype(BF16)))
        far.append(_dot(qb, state.astype(BF16)))
        zeta = jnp.exp(lgs[h] * (c - 1.0 - idx))
        state_ref[h] = state * math.exp(lgs[h] * c) + _dot_tn((k * zeta).astype(BF16), vb)
        qbs.append(qb)
        vbs.append(vb)

    outs = []
    for h in heads:
        xi = jnp.exp(lgs[h] * (idx + 1.0))
        outs.append(_dot((scores[h] * decay_ref[h]).astype(BF16), vbs[h]) + far[h] * xi)

    for h in heads:
        o = outs[h]
        mu = jnp.mean(o, axis=-1, keepdims=True)
        d = o - mu
        var = jnp.mean(d * d, axis=-1, keepdims=True)
        o = d * lax.rsqrt(var + EPS) * gw_ref[:, sls[h]]
        g = x_ref[:, 3 * RET_OUT + h * RET_D:3 * RET_OUT + (h + 1) * RET_D]
        o_ref[:, sls[h]] = (g * _sigmoid(g) * o).astype(BF16)


def _group_roll(x, shift):
    n, w = x.shape
    return pltpu.roll(x.reshape(n // SUBLANES, SUBLANES, w), shift, 1).reshape(n, w)


def _rec_kernel(xr_ref, xh_ref, c_ref, s_ref, gw_ref, lbl_ref, nw_ref,
                ckv_ref, kr_ref, kvw_ref, wkv_ref, c4_ref, sa_ref, sb_ref,
                ob_ref, oc_ref, kcat_ref, vt_ref,
                rstate_ref, decay_ref, hstate_ref, *, layer):
    lgs = [math.log1p(-(2.0 ** (-5.0 - h))) for h in range(RET_HEADS)]

    @pl.when(pl.program_id(1) == 0)
    def _():
        hstate_ref[...] = jnp.zeros_like(hstate_ref)
        rstate_ref[...] = jnp.zeros_like(rstate_ref)
        row = lax.broadcasted_iota(jnp.int32, (RET_C, RET_C), 0)
        col = lax.broadcasted_iota(jnp.int32, (RET_C, RET_C), 1)
        rel = (row - col).astype(F32)
        for h in range(RET_HEADS):
            decay_ref[h] = jnp.where(rel >= 0, jnp.exp(lgs[h] * jnp.maximum(rel, 0.0)), 0.0)

    logits = lbl_ref[...]
    e = jnp.exp(logits - jnp.max(logits, axis=0, keepdims=True))
    p = e / jnp.sum(e, axis=0, keepdims=True)
    lb_all = jnp.zeros((1, HG_HEADS * HG_D), F32)
    for m in range(layer + 1):
        lb_all = lb_all + p[m:m + 1, :]
    lb_all = lb_all - p[0:1, :]

    hg_per_ret = RET_C // HG_C
    for s in range(REC_ROWS // RET_C):
        rows = pl.ds(s * RET_C, RET_C)
        _ret_chunk(xr_ref.at[rows], c_ref.at[rows], s_ref.at[rows], gw_ref, ob_ref.at[rows],
                   rstate_ref, decay_ref, lgs)
        _kv_rows(s * RET_C, RET_C, ckv_ref, kr_ref, kvw_ref, wkv_ref, c4_ref, sa_ref, sb_ref,
                 kcat_ref, vt_ref)
        for u in range(hg_per_ret):
            rows = pl.ds((s * hg_per_ret + u) * HG_C, HG_C)
            _hgrn_chunk(xh_ref.at[rows], lb_all, nw_ref, oc_ref.at[rows], hstate_ref)


def _recurrent(proj, cosf, sinf, gw, lb_logits, nw, kv_nw, w_kv, c4, sa, sb, layer, batch, seq):
    t = proj.shape[0]
    nc = seq // REC_ROWS
    n_vt = REC_ROWS // ATT_TK
    tab = pl.BlockSpec((REC_ROWS, LANES), lambda b, c: (c, 0))
    out = pl.BlockSpec((REC_ROWS, RET_OUT), lambda b, c: (b * nc + c, 0))
    return pl.pallas_call(
        functools.partial(_rec_kernel, layer=layer),
        grid=(batch, nc),
        in_specs=[
            pl.BlockSpec((REC_ROWS, MIX_W), lambda b, c: (b * nc + c, COL_RET // MIX_W)),
            pl.BlockSpec((REC_ROWS, MIX_W), lambda b, c: (b * nc + c, COL_HG // MIX_W)),
            tab, tab,
            pl.BlockSpec((1, RET_OUT), lambda b, c: (0, 0)),
            pl.BlockSpec((DEPTH, HG_OUT), lambda b, c: (0, 0)),
            pl.BlockSpec((1, HG_OUT), lambda b, c: (0, 0)),
            pl.BlockSpec((REC_ROWS, KV_RANK), lambda b, c: (b * nc + c, COL_CKV // KV_RANK)),
            pl.BlockSpec((REC_ROWS, LANES), lambda b, c: (b * nc + c, COL_KR // LANES)),
            pl.BlockSpec((1, KV_RANK), lambda b, c: (0, 0)),
            pl.BlockSpec((None, KV_RANK, KV_W), lambda b, c: (layer, 0, 0)),
            tab, tab, tab,
        ],
        out_specs=[
            out, out,
            pl.BlockSpec((REC_ROWS, KCAT_W), lambda b, c: (b * nc + c, 0)),
            pl.BlockSpec((n_vt, MLA_OUT, ATT_TK), lambda b, c: (b * nc + c, 0, 0)),
        ],
        out_shape=[
            jax.ShapeDtypeStruct((t, RET_OUT), BF16),
            jax.ShapeDtypeStruct((t, HG_OUT), BF16),
            jax.ShapeDtypeStruct((t, KCAT_W), BF16),
            jax.ShapeDtypeStruct((t // ATT_TK, MLA_OUT, ATT_TK), BF16),
        ],
        scratch_shapes=[pltpu.VMEM((RET_HEADS, RET_D, RET_D), F32),
                        pltpu.VMEM((RET_HEADS, RET_C, RET_C), F32),
                        pltpu.VMEM((HG_HEADS, HG_D, HG_D), F32)],
        compiler_params=_cparams(("parallel", "arbitrary")),
        name="kv_retention_hgrn2",
    )(proj, proj, cosf, sinf, gw, lb_logits, nw, proj, proj, kv_nw, w_kv, c4, sa, sb)


def _hgrn_chunk(x_ref, lb_all, nw_ref, o_ref, state_ref):
    c = HG_C
    rowv = lax.broadcasted_iota(jnp.int32, (c, LANES), 0)
    row = lax.broadcasted_iota(jnp.int32, (c, c), 0)
    col = lax.broadcasted_iota(jnp.int32, (c, c), 1)

    z = x_ref[:, HG_OUT:2 * HG_OUT]
    ez = jnp.exp(-jnp.abs(z))
    r = 1.0 / (1.0 + ez)
    pos = z >= 0
    sig_p = jnp.where(pos, r, ez * r)
    sig_n = jnp.where(pos, ez * r, r)
    f = lb_all + (1.0 - lb_all) * sig_p
    lf = jnp.log(jnp.maximum(f, MIN_FORGET))
    kk_all = (1.0 - lb_all) * sig_n

    tri = jnp.where(col <= row, 1.0, 0.0).astype(BF16)
    lf_hi = lf.astype(BF16)
    rem = lf - lf_hi.astype(F32)
    lf_mid = rem.astype(BF16)
    lf_lo = (rem - lf_mid.astype(F32)).astype(BF16)
    b_all = _dot(tri, lf_hi) + _dot(tri, lf_mid) + _dot(tri, lf_lo)

    heads = range(HG_HEADS)
    sls = [slice(h * HG_D, (h + 1) * HG_D) for h in heads]
    qs = [x_ref[:, sls[h]] for h in heads]
    kks = [kk_all[:, sls[h]] for h in heads]
    bs = [b_all[:, sls[h]] for h in heads]
    vbs = [x_ref[:, 2 * HG_OUT + h * HG_D:2 * HG_OUT + (h + 1) * HG_D].astype(BF16) for h in heads]


    o_far = []
    for h in heads:
        state = state_ref[h]
        b_last = bs[h][c - 1:c, :]
        o_far.append(_dot_nt((qs[h] * jnp.exp(bs[h])).astype(BF16), state.astype(BF16)))
        upd = _dot_tn(vbs[h], (kks[h] * jnp.exp(b_last - bs[h])).astype(BF16))
        state_ref[h] = jnp.exp(b_last) * state + upd

    a_lvl = []
    for h in heads:
        q, kk, b = qs[h], kks[h], bs[h]
        a = None
        m = HG_NEAR
        while m < c:
            parts = []
            for blk in range(c // (2 * m)):
                lo = blk * 2 * m
                ref = b[lo + m - 1:lo + m, :]
                parts.append(b[lo:lo + 2 * m, :] - ref)
            d = parts[0] if len(parts) == 1 else jnp.concatenate(parts, axis=0)
            second = (rowv & (2 * m - 1)) >= m
            efac = jnp.exp(jnp.where(second, d, -d))
            ql = jnp.where(second, q * efac, 0.0).astype(BF16)
            kl = jnp.where(second, 0.0, kk * efac).astype(BF16)
            al = _dot_nt(ql, kl)
            if 2 * m < c:
                sft = (2 * m).bit_length() - 1
                al = jnp.where((row >> sft) == (col >> sft), al, 0.0)
            a = al if a is None else a + al
            m *= 2
        a_lvl.append(a)

    a_all = []
    for h in heads:
        q, kk, b = qs[h], kks[h], bs[h]
        a = a_lvl[h] + jnp.where(col == row, jnp.sum(q * kk, axis=-1, keepdims=True), 0.0)
        for dlt in range(1, HG_NEAR):
            ok = (rowv & (HG_NEAR - 1)) >= dlt
            diff = jnp.where(ok, b - _group_roll(b, dlt), 0.0)
            a_d = jnp.sum(q * _group_roll(kk, dlt) * jnp.exp(diff), axis=-1, keepdims=True)
            a = a + jnp.where((col == row - dlt) & ((row & (HG_NEAR - 1)) >= dlt), a_d, 0.0)
        a_all.append(a)

    for h in heads:
        o = _dot(a_all[h].astype(BF16), vbs[h]) + o_far[h]
        o = o * lax.rsqrt(jnp.mean(o * o, axis=-1, keepdims=True) + EPS)
        o = o * nw_ref[:, sls[h]]
        g = x_ref[:, 3 * HG_OUT + h * HG_D:3 * HG_OUT + (h + 1) * HG_D]
        o_ref[:, sls[h]] = (g * _sigmoid(g) * o).astype(BF16)


def _outproj_kernel(h_ref, oa_ref, ob_ref, oc_ref, w_ref, o_ref):
    acc = _dot(oa_ref[...], w_ref[:MLA_OUT, :])
    acc += _dot(ob_ref[...], w_ref[MLA_OUT:MLA_OUT + RET_OUT, :])
    acc += _dot(oc_ref[...], w_ref[MLA_OUT + RET_OUT:, :])
    o_ref[...] = h_ref[...] + acc


def _outproj(h, oa, ob, oc, w, layer):
    t = h.shape[0]
    return pl.pallas_call(
        _outproj_kernel,
        grid=(t // OUT_TM,),
        in_specs=[
            pl.BlockSpec((OUT_TM, D_MODEL), lambda i: (i, 0)),
            pl.BlockSpec((OUT_TM, MLA_OUT), lambda i: (i, 0)),
            pl.BlockSpec((OUT_TM, RET_OUT), lambda i: (i, 0)),
            pl.BlockSpec((OUT_TM, HG_OUT), lambda i: (i, 0)),
            pl.BlockSpec((None, D_MODEL, D_MODEL), lambda i: (layer, 0, 0)),
        ],
        out_specs=pl.BlockSpec((OUT_TM, D_MODEL), lambda i: (i, 0)),
        out_shape=jax.ShapeDtypeStruct((t, D_MODEL), F32),
        compiler_params=_cparams(("parallel",)),
        name="outproj",
    )(h, oa, ob, oc, w)


def _rope_tables(seq):
    inv64 = ROPE_BASE ** (-jnp.arange(0, MLA_ROPE, 2, dtype=F32) / MLA_ROPE)
    ang64 = jnp.arange(seq, dtype=F32)[:, None] * inv64[None, :]
    c, s = jnp.cos(ang64), jnp.sin(ang64)
    z = jnp.zeros_like(s)
    c4 = jnp.concatenate([c, c, c, c], axis=-1)
    sa = jnp.concatenate([-s, z, -s, z], axis=-1)
    sb = jnp.concatenate([z, s, z, s], axis=-1)
    inv128 = ROPE_BASE ** (-jnp.arange(0, RET_D, 2, dtype=F32) / RET_D)
    ang128 = jnp.arange(seq, dtype=F32)[:, None] * inv128[None, :]
    cf, sf = jnp.cos(ang128), jnp.sin(ang128)
    return c4, sa, sb, jnp.concatenate([cf, cf], axis=-1), jnp.concatenate([-sf, sf], axis=-1)


def _win_kernel(wt_ref, o_ref):
    hd = MLA_NOPE + MLA_ROPE
    nq = MLA_HEADS * hd
    tk = o_ref.shape[0]

    def put(col, *row_ranges):
        pieces = [wt_ref[r0:r1, :] for r0, r1 in row_ranges]
        piece = pieces[0] if len(pieces) == 1 else jnp.concatenate(pieces, axis=0)
        o_ref[:, col:col + piece.shape[0]] = piece.T.astype(BF16)

    for h in range(MLA_HEADS):
        put(COL_QN + h * MLA_NOPE, (h * hd, h * hd + MLA_NOPE))
    for p in range(MLA_HEADS // 2):
        put(COL_QR + p * LANES, ((2 * p) * hd + MLA_NOPE, (2 * p + 1) * hd),
            ((2 * p + 1) * hd + MLA_NOPE, (2 * p + 2) * hd))
    for blk in range(KV_RANK // LANES):
        put(COL_CKV + blk * LANES, (nq + blk * LANES, nq + (blk + 1) * LANES))
    src = nq + KV_RANK + MLA_ROPE
    for blk in range((COL_KR - COL_RET) // LANES):
        put(COL_RET + blk * LANES, (src + blk * LANES, src + (blk + 1) * LANES))
    kr = jnp.concatenate([wt_ref[nq + KV_RANK:nq + KV_RANK + MLA_ROPE, :],
                          jnp.zeros((LANES - MLA_ROPE, tk), F32)], axis=0)
    o_ref[:, COL_KR:COL_KR + LANES] = kr.T.astype(BF16)
    o_ref[:, COL_KR + LANES:] = jnp.zeros((tk, D_IN_PAD - COL_KR - LANES), BF16)


def _prep_w_in(w_in):
    depth, d, n = w_in.shape
    return pl.pallas_call(
        _win_kernel,
        grid=(depth, d // WIN_TK),
        in_specs=[pl.BlockSpec((None, n, WIN_TK), lambda l, r: (l, 0, r))],
        out_specs=pl.BlockSpec((None, WIN_TK, D_IN_PAD), lambda l, r: (l, r, 0)),
        out_shape=jax.ShapeDtypeStruct((depth, d, D_IN_PAD), BF16),
        compiler_params=_cparams(("parallel", "parallel")),
        name="w_in_relayout",
    )(jnp.swapaxes(w_in, 1, 2))


def _prep_w_kv(w):
    w = w.reshape(DEPTH, KV_RANK, MLA_HEADS, MLA_NOPE + MLA_V)
    kn = w[..., :MLA_NOPE].reshape(DEPTH, KV_RANK, MLA_HEADS * MLA_NOPE)
    v = w[..., MLA_NOPE:].reshape(DEPTH, KV_RANK, MLA_HEADS * MLA_V)
    return jnp.concatenate([kn, v], axis=-1).astype(BF16)


def kernel(x, ffn1_norm, ffn1_w1, ffn1_w3, ffn1_w2, mix_norm, w_in, mla_kv_norm, mla_w_kv_b,
           mla_out_norm, ret_gn, hgrn_lb_logits, hgrn_out_norm, w_o, ffn2_norm, ffn2_w1,
           ffn2_w3, ffn2_w2, final_norm):
    batch, seq, d = x.shape
    assert d == D_MODEL and seq % ATT_TQ == 0 and seq % ATT_TK == 0 and seq % REC_ROWS == 0
    assert REC_ROWS % RET_C == 0 and RET_C % HG_C == 0 and REC_ROWS % ATT_TK == 0
    assert ATT_TK == ATT_TQ
    t = batch * seq
    assert t % PROJ_TM == 0 and t % FFN_TM == 0
    c4, sa, sb, cosf, sinf = _rope_tables(seq)
    w_in_p = _prep_w_in(w_in)
    w_kv_p = _prep_w_kv(mla_w_kv_b)
    row = lambda a: a.reshape(1, -1)

    w_o_b = w_o.astype(BF16)
    ffn_w = (ffn1_w1[0].astype(BF16), ffn1_w3[0].astype(BF16), ffn1_w2[0].astype(BF16))
    h = x.reshape(t, d)
    for l in range(DEPTH):
        h, ffn_w = _ffn(h, row(ffn1_norm[l]), *ffn_w, nxt=(ffn2_w1, ffn2_w3, ffn2_w2, l))
        proj = _inproj(h, row(mix_norm[l]), w_in_p, l)
        ob, oc, kcat, vt = _recurrent(proj, cosf, sinf, row(ret_gn[l]), hgrn_lb_logits,
                                      row(hgrn_out_norm[l]), row(mla_kv_norm[l]), w_kv_p,
                                      c4, sa, sb, l, batch, seq)
        oa = _attention(proj, c4, sa, sb, kcat, vt, mla_out_norm[l].reshape(-1, 1), batch, seq)
        h = _outproj(h, oa, ob, oc, w_o_b, l)
        last = l == DEPTH - 1
        h, ffn_w = _ffn(h, row(ffn2_norm[l]), *ffn_w,
                        nxt=None if last else (ffn1_w1, ffn1_w3, ffn1_w2, l + 1),
                        final_w=row(final_norm) if last else None)
    return h.reshape(batch, seq, d)
```

```python
import functools
import math

import jax
import jax.numpy as jnp
from jax import lax
from jax.experimental import pallas as pl
from jax.experimental.pallas import tpu as pltpu

F32 = jnp.float32
BF16 = jnp.bfloat16

D_MODEL = 2048
DEPTH = 4
MLA_HEADS = 8
MLA_NOPE = 128
MLA_ROPE = 64
MLA_V = 128
KV_RANK = 512
RET_HEADS = 4
RET_D = 128
HG_HEADS = 4
HG_D = 128
D_FF = 5632
ROPE_BASE = 10000.0
EPS = 1e-6
MASK_VALUE = -1e30
MIN_FORGET = 1e-20

LANES = 128
SUBLANES = 8
VMEM_LIMIT = 60 * 1024 * 1024

MLA_QN = MLA_HEADS * MLA_NOPE
MLA_QR = MLA_HEADS * MLA_ROPE
MLA_OUT = MLA_HEADS * MLA_V
KV_W = MLA_HEADS * (MLA_NOPE + MLA_V)
KCAT_HEAD = 2 * LANES
KCAT_W = MLA_HEADS * KCAT_HEAD
RET_OUT = RET_HEADS * RET_D
HG_OUT = HG_HEADS * HG_D
MIX_W = 4 * RET_OUT
assert RET_OUT == HG_OUT

COL_QN = 0
COL_QR = COL_QN + MLA_QN
COL_CKV = COL_QR + MLA_QR
COL_RET = COL_CKV + KV_RANK
COL_HG = COL_RET + MIX_W
COL_KR = COL_HG + MIX_W
IN_TN = 1280
D_IN_PAD = 5 * IN_TN
assert D_IN_PAD >= COL_KR + LANES

FFN_TM = 1024
FFN_TF = 512
PROJ_TM = 1024
ATT_TQ = 512
ATT_TK = 512
ATT_AHEAD = 2
ATT_ONES = 16
RET_C = 256
HG_C = 128
REC_ROWS = 512
HG_NEAR = 4
OUT_TM = 512
WIN_TK = 256


def _cparams(sem):
    return pltpu.CompilerParams(dimension_semantics=sem, vmem_limit_bytes=VMEM_LIMIT)


def _rms(x, w):
    return (x * lax.rsqrt(jnp.mean(x * x, axis=-1, keepdims=True) + EPS)) * w


def _sigmoid(x):
    return 1.0 / (1.0 + jnp.exp(-x))


def _dot(a, b):
    return jnp.dot(a, b, preferred_element_type=F32)


def _dot_nt(a, b):
    return lax.dot_general(a, b, (((1,), (1,)), ((), ())), preferred_element_type=F32)


def _dot_tn(a, b):
    return lax.dot_general(a, b, (((0,), (0,)), ((), ())), preferred_element_type=F32)


def _ffn_kernel(x_ref, nw_ref, w1_ref, w3_ref, w2_ref, *rest, n_f, final, convert):
    rest = list(rest)
    cast_in = [rest.pop(0) for _ in range(3)] if convert else []
    fw_ref = rest.pop(0) if final else None
    o_ref = rest.pop(0)
    cast_out = [rest.pop(0) for _ in range(3)] if convert else []
    (n_ref,) = rest
    f = pl.program_id(1)

    for src, dst in zip(cast_in, cast_out):
        dst[...] = src[...].astype(BF16)

    @pl.when(f == 0)
    def _():
        x = x_ref[...]
        n_ref[...] = _rms(x, nw_ref[...]).astype(BF16)
        o_ref[...] = x

    n = n_ref[...]
    h1 = _dot(n, w1_ref[...])
    h3 = _dot(n, w3_ref[...])
    g = (h1 * _sigmoid(h1) * h3 * 0.5).astype(BF16)
    o_ref[...] += _dot(g, w2_ref[...])

    if final:
        @pl.when(f == n_f - 1)
        def _():
            o_ref[...] = _rms(o_ref[...], fw_ref[...])


def _ffn(h, nw, w1, w3, w2, nxt=None, final_w=None):
    t = h.shape[0]
    n_i = t // FFN_TM
    n_f = D_FF // FFN_TF
    final = final_w is not None
    convert = nxt is not None
    in_specs = [
        pl.BlockSpec((FFN_TM, D_MODEL), lambda i, f: (i, 0)),
        pl.BlockSpec((1, D_MODEL), lambda i, f: (0, 0)),
        pl.BlockSpec((D_MODEL, FFN_TF), lambda i, f: (0, f)),
        pl.BlockSpec((D_MODEL, FFN_TF), lambda i, f: (0, f)),
        pl.BlockSpec((FFN_TF, D_MODEL), lambda i, f: (f, 0)),
    ]
    args = [h, nw, w1, w3, w2]
    out_specs = [pl.BlockSpec((FFN_TM, D_MODEL), lambda i, f: (i, 0))]
    out_shape = [jax.ShapeDtypeStruct((t, D_MODEL), F32)]
    if convert:
        n1, n3, n2, layer = nxt
        dr = D_MODEL // n_i
        in_specs += [
            pl.BlockSpec((None, dr, FFN_TF), lambda i, f: (layer, i, f)),
            pl.BlockSpec((None, dr, FFN_TF), lambda i, f: (layer, i, f)),
            pl.BlockSpec((None, FFN_TF, dr), lambda i, f: (layer, f, i)),
        ]
        args += [n1, n3, n2]
        out_specs += [
            pl.BlockSpec((dr, FFN_TF), lambda i, f: (i, f)),
            pl.BlockSpec((dr, FFN_TF), lambda i, f: (i, f)),
            pl.BlockSpec((FFN_TF, dr), lambda i, f: (f, i)),
        ]
        out_shape += [
            jax.ShapeDtypeStruct((D_MODEL, D_FF), BF16),
            jax.ShapeDtypeStruct((D_MODEL, D_FF), BF16),
            jax.ShapeDtypeStruct((D_FF, D_MODEL), BF16),
        ]
    if final:
        in_specs.append(pl.BlockSpec((1, D_MODEL), lambda i, f: (0, 0)))
        args.append(final_w)
    outs = pl.pallas_call(
        functools.partial(_ffn_kernel, n_f=n_f, final=final, convert=convert),
        grid=(n_i, n_f),
        in_specs=in_specs,
        out_specs=out_specs,
        out_shape=out_shape,
        scratch_shapes=[pltpu.VMEM((FFN_TM, D_MODEL), BF16)],
        compiler_params=_cparams(("parallel", "arbitrary")),
        name="ffn_final" if final else "ffn",
    )(*args)
    return outs[0], tuple(outs[1:])


def _inproj_kernel(x_ref, nw_ref, w_ref, o_ref, n_ref):
    @pl.when(pl.program_id(1) == 0)
    def _():
        n_ref[...] = _rms(x_ref[...], nw_ref[...]).astype(BF16)

    o_ref[...] = _dot(n_ref[...], w_ref[...])


def _inproj(h, nw, w, layer):
    t = h.shape[0]
    return pl.pallas_call(
        _inproj_kernel,
        grid=(t // PROJ_TM, D_IN_PAD // IN_TN),
        in_specs=[
            pl.BlockSpec((PROJ_TM, D_MODEL), lambda i, j: (i, 0)),
            pl.BlockSpec((1, D_MODEL), lambda i, j: (0, 0)),
            pl.BlockSpec((None, D_MODEL, IN_TN), lambda i, j: (layer, 0, j)),
        ],
        out_specs=pl.BlockSpec((PROJ_TM, IN_TN), lambda i, j: (i, j)),
        out_shape=jax.ShapeDtypeStruct((t, D_IN_PAD), F32),
        scratch_shapes=[pltpu.VMEM((PROJ_TM, D_MODEL), BF16)],
        compiler_params=_cparams(("parallel", "arbitrary")),
        name="inproj",
    )(h, nw, w)


def _rope_pair(p, c4, sa, sb):
    return p * c4 + pltpu.roll(p, 96, 1) * sa + pltpu.roll(p, 32, 1) * sb


def _rope_full(x, c, s):
    return x * c + pltpu.roll(x, 64, 1) * s


def _kv_rows(r0, rn, ckv_ref, kr_ref, nw_ref, w_ref, c4_ref, sa_ref, sb_ref, kcat_ref, vt_ref):
    rows = pl.ds(r0, rn)
    n = _rms(ckv_ref[rows, :], nw_ref[...]).astype(BF16)
    kv = _dot(n, w_ref[...])
    kr = _rope_pair(kr_ref[rows, :], c4_ref[rows, :], sa_ref[rows, :], sb_ref[rows, :])
    kr_lo = kr.astype(BF16)
    kr_hi = pltpu.roll(kr, MLA_ROPE, 1).astype(BF16)
    for h in range(MLA_HEADS):
        k0 = h * KCAT_HEAD
        kcat_ref[rows, k0:k0 + MLA_NOPE] = kv[:, h * MLA_NOPE:(h + 1) * MLA_NOPE].astype(BF16)
        kcat_ref[rows, k0 + MLA_NOPE:k0 + KCAT_HEAD] = kr_lo if h % 2 == 0 else kr_hi
    tile, off = divmod(r0, ATT_TK)
    vt_ref[tile, :, off:off + rn] = kv[:, MLA_QN:].T.astype(BF16)


def _attn_kernel(qn_ref, qr_ref, c4_ref, sa_ref, sb_ref, k_ref, vt_ref, nw_ref, o_ref,
                 qt_s, m_s, acc_s):
    i = pl.program_id(1)
    tq, tk = ATT_TQ, ATT_TK
    scale = (MLA_NOPE + MLA_ROPE) ** -0.5 * math.log2(math.e)

    for h in range(MLA_HEADS):
        if h % 2 == 0:
            pair = qr_ref[:, (h // 2) * LANES:(h // 2 + 1) * LANES]
            roped = _rope_pair(pair, c4_ref[...], sa_ref[...], sb_ref[...]) * scale
        qn = qn_ref[:, h * LANES:(h + 1) * LANES] * scale
        qt_s[h] = jnp.concatenate([qn, roped], axis=-1).T.astype(BF16)
    m_s[...] = jnp.full(m_s.shape, MASK_VALUE, F32)
    acc_s[...] = jnp.zeros(acc_s.shape, F32)

    def tile(j, masked):
        ones = jnp.ones((ATT_ONES, tk), BF16)

        def scores(h):
            return _dot(k_ref[j, :, h * KCAT_HEAD:(h + 1) * KCAT_HEAD], qt_s[h])

        pending = [scores(h) for h in range(ATT_AHEAD)]
        for h in range(MLA_HEADS):
            st = pending.pop(0)
            if h + ATT_AHEAD < MLA_HEADS:
                pending.append(scores(h + ATT_AHEAD))
            if masked:
                key = lax.broadcasted_iota(jnp.int32, (tk, tq), 0)
                qry = lax.broadcasted_iota(jnp.int32, (tk, tq), 1)
                st = jnp.where(key <= qry, st, MASK_VALUE)
            m_old = m_s[h]
            m_new = jnp.maximum(m_old, jnp.max(st, axis=0, keepdims=True))
            p = jnp.exp2(st - m_new).astype(BF16)
            alpha = jnp.exp2(m_old - m_new)
            vext = jnp.concatenate([vt_ref[j, h * MLA_V:(h + 1) * MLA_V, :], ones], axis=0)
            acc_s[h] = alpha * acc_s[h] + _dot(vext, p)
            m_s[h] = m_new

    def body(j, carry):
        tile(j, False)
        return carry

    lax.fori_loop(0, i, body, 0)
    tile(i, True)

    for h in range(MLA_HEADS):
        acc = acc_s[h]
        ot = acc[:MLA_V, :] / acc[MLA_V:MLA_V + 1, :]
        ot = ot * lax.rsqrt(jnp.mean(ot * ot, axis=0, keepdims=True) + EPS)
        ot = ot * nw_ref[h * MLA_V:(h + 1) * MLA_V, :]
        o_ref[:, h * MLA_V:(h + 1) * MLA_V] = ot.T.astype(BF16)


def _attention(proj, c4, sa, sb, kcat, vt, nw_col, batch, seq):
    t = proj.shape[0]
    nq = seq // ATT_TQ
    nk = seq // ATT_TK
    tab = pl.BlockSpec((ATT_TQ, LANES), lambda b, i: (i, 0))
    return pl.pallas_call(
        _attn_kernel,
        grid=(batch, nq),
        in_specs=[
            pl.BlockSpec((ATT_TQ, MLA_QN), lambda b, i: (b * nq + i, COL_QN // MLA_QN)),
            pl.BlockSpec((ATT_TQ, MLA_QR), lambda b, i: (b * nq + i, COL_QR // MLA_QR)),
            tab, tab, tab,
            pl.BlockSpec((nk, ATT_TK, KCAT_W), lambda b, i: (b, 0, 0)),
            pl.BlockSpec((nk, MLA_OUT, ATT_TK), lambda b, i: (b, 0, 0)),
            pl.BlockSpec((MLA_OUT, 1), lambda b, i: (0, 0)),
        ],
        out_specs=pl.BlockSpec((ATT_TQ, MLA_OUT), lambda b, i: (b * nq + i, 0)),
        out_shape=jax.ShapeDtypeStruct((t, MLA_OUT), BF16),
        scratch_shapes=[
            pltpu.VMEM((MLA_HEADS, KCAT_HEAD, ATT_TQ), BF16),
            pltpu.VMEM((MLA_HEADS, 1, ATT_TQ), F32),
            pltpu.VMEM((MLA_HEADS, MLA_V + ATT_ONES, ATT_TQ), F32),
        ],
        compiler_params=_cparams(("parallel", "arbitrary")),
        name="mla_attention",
    )(proj, proj, c4, sa, sb, kcat.reshape(t // ATT_TK, ATT_TK, KCAT_W), vt, nw_col)


def _ret_chunk(x_ref, c_ref, s_ref, gw_ref, o_ref, state_ref, decay_ref, lgs):
    c = RET_C
    idx = lax.broadcasted_iota(jnp.int32, (c, 1), 0).astype(F32)
    cosv = c_ref[...]
    sinv = s_ref[...]
    heads = range(RET_HEADS)
    sls = [slice(h * RET_D, (h + 1) * RET_D) for h in heads]

    qbs, vbs, scores, far = [], [], [], []
    for h in heads:
        q = _rope_full(x_ref[:, sls[h]], cosv, sinv)
        k = _rope_full(x_ref[:, RET_OUT + h * RET_D:RET_OUT + (h + 1) * RET_D], cosv, sinv)
        k = k * (RET_D ** -0.5)
        qb = q.astype(BF16)
        vb = x_ref[:, 2 * RET_OUT + h * RET_D:2 * RET_OUT + (h + 1) * RET_D].astype(BF16)
        state = state_ref[h]
        scores.append(_dot_nt(qb, k.astype(BF16)))
        far.append(_dot(qb, state.astype(BF16)))
        zeta = jnp.exp(lgs[h] * (c - 1.0 - idx))
        state_ref[h] = state * math.exp(lgs[h] * c) + _dot_tn((k * zeta).astype(BF16), vb)
        qbs.append(qb)
        vbs.append(vb)

    outs = []
    for h in heads:
        xi = jnp.exp(lgs[h] * (idx + 1.0))
        outs.append(_dot((scores[h] * decay_ref[h]).astype(BF16), vbs[h]) + far[h] * xi)

    for h in heads:
        o = outs[h]
        mu = jnp.mean(o, axis=-1, keepdims=True)
        d = o - mu
        var = jnp.mean(d * d, axis=-1, keepdims=True)
        o = d * lax.rsqrt(var + EPS) * gw_ref[:, sls[h]]
        g = x_ref[:, 3 * RET_OUT + h * RET_D:3 * RET_OUT + (h + 1) * RET_D]
        o_ref[:, sls[h]] = (g * _sigmoid(g) * o).astype(BF16)


def _group_roll(x, shift):
    n, w = x.shape
    return pltpu.roll(x.reshape(n // SUBLANES, SUBLANES, w), shift, 1).reshape(n, w)


def _rec_kernel(xr_ref, xh_ref, c_ref, s_ref, gw_ref, lbl_ref, nw_ref,
                ckv_ref, kr_ref, kvw_ref, wkv_ref, c4_ref, sa_ref, sb_ref,
                ob_ref, oc_ref, kcat_ref, vt_ref,
                rstate_ref, decay_ref, hstate_ref, *, layer):
    lgs = [math.log1p(-(2.0 ** (-5.0 - h))) for h in range(RET_HEADS)]

    @pl.when(pl.program_id(1) == 0)
    def _():
        hstate_ref[...] = jnp.zeros_like(hstate_ref)
        rstate_ref[...] = jnp.zeros_like(rstate_ref)
        row = lax.broadcasted_iota(jnp.int32, (RET_C, RET_C), 0)
        col = lax.broadcasted_iota(jnp.int32, (RET_C, RET_C), 1)
        rel = (row - col).astype(F32)
        for h in range(RET_HEADS):
            decay_ref[h] = jnp.where(rel >= 0, jnp.exp(lgs[h] * jnp.maximum(rel, 0.0)), 0.0)

    logits = lbl_ref[...]
    e = jnp.exp(logits - jnp.max(logits, axis=0, keepdims=True))
    p = e / jnp.sum(e, axis=0, keepdims=True)
    lb_all = jnp.zeros((1, HG_HEADS * HG_D), F32)
    for m in range(layer + 1):
        lb_all = lb_all + p[m:m + 1, :]
    lb_all = lb_all - p[0:1, :]

    hg_per_ret = RET_C // HG_C
    for s in range(REC_ROWS // RET_C):
        rows = pl.ds(s * RET_C, RET_C)
        _ret_chunk(xr_ref.at[rows], c_ref.at[rows], s_ref.at[rows], gw_ref, ob_ref.at[rows],
                   rstate_ref, decay_ref, lgs)
        _kv_rows(s * RET_C, RET_C, ckv_ref, kr_ref, kvw_ref, wkv_ref, c4_ref, sa_ref, sb_ref,
                 kcat_ref, vt_ref)
        for u in range(hg_per_ret):
            rows = pl.ds((s * hg_per_ret + u) * HG_C, HG_C)
            _hgrn_chunk(xh_ref.at[rows], lb_all, nw_ref, oc_ref.at[rows], hstate_ref)


def _recurrent(proj, cosf, sinf, gw, lb_logits, nw, kv_nw, w_kv, c4, sa, sb, layer, batch, seq):
    t = proj.shape[0]
    nc = seq // REC_ROWS
    n_vt = REC_ROWS // ATT_TK
    tab = pl.BlockSpec((REC_ROWS, LANES), lambda b, c: (c, 0))
    out = pl.BlockSpec((REC_ROWS, RET_OUT), lambda b, c: (b * nc + c, 0))
    return pl.pallas_call(
        functools.partial(_rec_kernel, layer=layer),
        grid=(batch, nc),
        in_specs=[
            pl.BlockSpec((REC_ROWS, MIX_W), lambda b, c: (b * nc + c, COL_RET // MIX_W)),
            pl.BlockSpec((REC_ROWS, MIX_W), lambda b, c: (b * nc + c, COL_HG // MIX_W)),
            tab, tab,
            pl.BlockSpec((1, RET_OUT), lambda b, c: (0, 0)),
            pl.BlockSpec((DEPTH, HG_OUT), lambda b, c: (0, 0)),
            pl.BlockSpec((1, HG_OUT), lambda b, c: (0, 0)),
            pl.BlockSpec((REC_ROWS, KV_RANK), lambda b, c: (b * nc + c, COL_CKV // KV_RANK)),
            pl.BlockSpec((REC_ROWS, LANES), lambda b, c: (b * nc + c, COL_KR // LANES)),
            pl.BlockSpec((1, KV_RANK), lambda b, c: (0, 0)),
            pl.BlockSpec((None, KV_RANK, KV_W), lambda b, c: (layer, 0, 0)),
            tab, tab, tab,
        ],
        out_specs=[
            out, out,
            pl.BlockSpec((REC_ROWS, KCAT_W), lambda b, c: (b * nc + c, 0)),
            pl.BlockSpec((n_vt, MLA_OUT, ATT_TK), lambda b, c: (b * nc + c, 0, 0)),
        ],
        out_shape=[
            jax.ShapeDtypeStruct((t, RET_OUT), BF16),
            jax.ShapeDtypeStruct((t, HG_OUT), BF16),
            jax.ShapeDtypeStruct((t, KCAT_W), BF16),
            jax.ShapeDtypeStruct((t // ATT_TK, MLA_OUT, ATT_TK), BF16),
        ],
        scratch_shapes=[pltpu.VMEM((RET_HEADS, RET_D, RET_D), F32),
                        pltpu.VMEM((RET_HEADS, RET_C, RET_C), F32),
                        pltpu.VMEM((HG_HEADS, HG_D, HG_D), F32)],
        compiler_params=_cparams(("parallel", "arbitrary")),
        name="kv_retention_hgrn2",
    )(proj, proj, cosf, sinf, gw, lb_logits, nw, proj, proj, kv_nw, w_kv, c4, sa, sb)


def _hgrn_chunk(x_ref, lb_all, nw_ref, o_ref, state_ref):
    c = HG_C
    rowv = lax.broadcasted_iota(jnp.int32, (c, LANES), 0)
    row = lax.broadcasted_iota(jnp.int32, (c, c), 0)
    col = lax.broadcasted_iota(jnp.int32, (c, c), 1)

    z = x_ref[:, HG_OUT:2 * HG_OUT]
    ez = jnp.exp(-jnp.abs(z))
    r = 1.0 / (1.0 + ez)
    pos = z >= 0
    sig_p = jnp.where(pos, r, ez * r)
    sig_n = jnp.where(pos, ez * r, r)
    f = lb_all + (1.0 - lb_all) * sig_p
    lf = jnp.log2(jnp.maximum(f, MIN_FORGET))
    kk_all = (1.0 - lb_all) * sig_n

    tri = jnp.where(col <= row, 1.0, 0.0).astype(BF16)
    lf_hi = lf.astype(BF16)
    rem = lf - lf_hi.astype(F32)
    lf_mid = rem.astype(BF16)
    lf_lo = (rem - lf_mid.astype(F32)).astype(BF16)
    b_all = _dot(tri, lf_hi) + _dot(tri, lf_mid) + _dot(tri, lf_lo)

    heads = range(HG_HEADS)
    sls = [slice(h * HG_D, (h + 1) * HG_D) for h in heads]
    qs = [x_ref[:, sls[h]] for h in heads]
    kks = [kk_all[:, sls[h]] for h in heads]
    bs = [b_all[:, sls[h]] for h in heads]
    vbs = [x_ref[:, 2 * HG_OUT + h * HG_D:2 * HG_OUT + (h + 1) * HG_D].astype(BF16) for h in heads]


    o_far = []
    for h in heads:
        state = state_ref[h]
        b_last = bs[h][c - 1:c, :]
        o_far.append(_dot_nt((qs[h] * jnp.exp2(bs[h])).astype(BF16), state.astype(BF16)))
        upd = _dot_tn(vbs[h], (kks[h] * jnp.exp2(b_last - bs[h])).astype(BF16))
        state_ref[h] = jnp.exp2(b_last) * state + upd

    a_lvl = []
    for h in heads:
        q, kk, b = qs[h], kks[h], bs[h]
        a = None
        m = HG_NEAR
        while m < c:
            second = (rowv & (2 * m - 1)) >= m
            parts = []
            for blk in range(c // (2 * m)):
                lo = blk * 2 * m
                ref = b[lo + m - 1:lo + m, :]
                if m >= SUBLANES:
                    parts += [ref - b[lo:lo + m, :], b[lo + m:lo + 2 * m, :] - ref]
                else:
                    blk_d = b[lo:lo + 2 * m, :] - ref
                    parts.append(jnp.where(second[lo:lo + 2 * m, :], blk_d, -blk_d))
            efac = jnp.exp2(jnp.concatenate(parts, axis=0))
            ql = jnp.where(second, q * efac, 0.0).astype(BF16)
            kl = jnp.where(second, 0.0, kk * efac).astype(BF16)
            al = _dot_nt(ql, kl)
            if 2 * m < c:
                sft = (2 * m).bit_length() - 1
                al = jnp.where((row >> sft) == (col >> sft), al, 0.0)
            a = al if a is None else a + al
            m *= 2
        a_lvl.append(a)

    a_all = []
    for h in heads:
        q, kk, b = qs[h], kks[h], bs[h]
        a = a_lvl[h] + jnp.where(col == row, jnp.sum(q * kk, axis=-1, keepdims=True), 0.0)
        for dlt in range(1, HG_NEAR):
            ok = (rowv & (HG_NEAR - 1)) >= dlt
            diff = jnp.where(ok, b - _group_roll(b, dlt), 0.0)
            a_d = jnp.sum(q * _group_roll(kk, dlt) * jnp.exp2(diff), axis=-1, keepdims=True)
            a = a + jnp.where((col == row - dlt) & ((row & (HG_NEAR - 1)) >= dlt), a_d, 0.0)
        a_all.append(a)

    for h in heads:
        o = _dot(a_all[h].astype(BF16), vbs[h]) + o_far[h]
        o = o * lax.rsqrt(jnp.mean(o * o, axis=-1, keepdims=True) + EPS)
        o = o * nw_ref[:, sls[h]]
        g = x_ref[:, 3 * HG_OUT + h * HG_D:3 * HG_OUT + (h + 1) * HG_D]
        o_ref[:, sls[h]] = (g * _sigmoid(g) * o).astype(BF16)


def _outproj_kernel(h_ref, oa_ref, ob_ref, oc_ref, w_ref, o_ref):
    acc = _dot(oa_ref[...], w_ref[:MLA_OUT, :])
    acc += _dot(ob_ref[...], w_ref[MLA_OUT:MLA_OUT + RET_OUT, :])
    acc += _dot(oc_ref[...], w_ref[MLA_OUT + RET_OUT:, :])
    o_ref[...] = h_ref[...] + acc


def _outproj(h, oa, ob, oc, w, layer):
    t = h.shape[0]
    return pl.pallas_call(
        _outproj_kernel,
        grid=(t // OUT_TM,),
        in_specs=[
            pl.BlockSpec((OUT_TM, D_MODEL), lambda i: (i, 0)),
            pl.BlockSpec((OUT_TM, MLA_OUT), lambda i: (i, 0)),
            pl.BlockSpec((OUT_TM, RET_OUT), lambda i: (i, 0)),
            pl.BlockSpec((OUT_TM, HG_OUT), lambda i: (i, 0)),
            pl.BlockSpec((None, D_MODEL, D_MODEL), lambda i: (layer, 0, 0)),
        ],
        out_specs=pl.BlockSpec((OUT_TM, D_MODEL), lambda i: (i, 0)),
        out_shape=jax.ShapeDtypeStruct((t, D_MODEL), F32),
        compiler_params=_cparams(("parallel",)),
        name="outproj",
    )(h, oa, ob, oc, w)


def _rope_tables(seq):
    inv64 = ROPE_BASE ** (-jnp.arange(0, MLA_ROPE, 2, dtype=F32) / MLA_ROPE)
    ang64 = jnp.arange(seq, dtype=F32)[:, None] * inv64[None, :]
    c, s = jnp.cos(ang64), jnp.sin(ang64)
    z = jnp.zeros_like(s)
    c4 = jnp.concatenate([c, c, c, c], axis=-1)
    sa = jnp.concatenate([-s, z, -s, z], axis=-1)
    sb = jnp.concatenate([z, s, z, s], axis=-1)
    inv128 = ROPE_BASE ** (-jnp.arange(0, RET_D, 2, dtype=F32) / RET_D)
    ang128 = jnp.arange(seq, dtype=F32)[:, None] * inv128[None, :]
    cf, sf = jnp.cos(ang128), jnp.sin(ang128)
    return c4, sa, sb, jnp.concatenate([cf, cf], axis=-1), jnp.concatenate([-sf, sf], axis=-1)


def _win_kernel(wt_ref, o_ref):
    hd = MLA_NOPE + MLA_ROPE
    nq = MLA_HEADS * hd
    tk = o_ref.shape[0]

    def put(col, *row_ranges):
        pieces = [wt_ref[r0:r1, :] for r0, r1 in row_ranges]
        piece = pieces[0] if len(pieces) == 1 else jnp.concatenate(pieces, axis=0)
        o_ref[:, col:col + piece.shape[0]] = piece.T.astype(BF16)

    for h in range(MLA_HEADS):
        put(COL_QN + h * MLA_NOPE, (h * hd, h * hd + MLA_NOPE))
    for p in range(MLA_HEADS // 2):
        put(COL_QR + p * LANES, ((2 * p) * hd + MLA_NOPE, (2 * p + 1) * hd),
            ((2 * p + 1) * hd + MLA_NOPE, (2 * p + 2) * hd))
    for blk in range(KV_RANK // LANES):
        put(COL_CKV + blk * LANES, (nq + blk * LANES, nq + (blk + 1) * LANES))
    src = nq + KV_RANK + MLA_ROPE
    for blk in range((COL_KR - COL_RET) // LANES):
        put(COL_RET + blk * LANES, (src + blk * LANES, src + (blk + 1) * LANES))
    kr = jnp.concatenate([wt_ref[nq + KV_RANK:nq + KV_RANK + MLA_ROPE, :],
                          jnp.zeros((LANES - MLA_ROPE, tk), F32)], axis=0)
    o_ref[:, COL_KR:COL_KR + LANES] = kr.T.astype(BF16)
    o_ref[:, COL_KR + LANES:] = jnp.zeros((tk, D_IN_PAD - COL_KR - LANES), BF16)


def _prep_w_in(w_in):
    depth, d, n = w_in.shape
    return pl.pallas_call(
        _win_kernel,
        grid=(depth, d // WIN_TK),
        in_specs=[pl.BlockSpec((None, n, WIN_TK), lambda l, r: (l, 0, r))],
        out_specs=pl.BlockSpec((None, WIN_TK, D_IN_PAD), lambda l, r: (l, r, 0)),
        out_shape=jax.ShapeDtypeStruct((depth, d, D_IN_PAD), BF16),
        compiler_params=_cparams(("parallel", "parallel")),
        name="w_in_relayout",
    )(jnp.swapaxes(w_in, 1, 2))


def _prep_w_kv(w):
    w = w.reshape(DEPTH, KV_RANK, MLA_HEADS, MLA_NOPE + MLA_V)
    kn = w[..., :MLA_NOPE].reshape(DEPTH, KV_RANK, MLA_HEADS * MLA_NOPE)
    v = w[..., MLA_NOPE:].reshape(DEPTH, KV_RANK, MLA_HEADS * MLA_V)
    return jnp.concatenate([kn, v], axis=-1).astype(BF16)


def kernel(x, ffn1_norm, ffn1_w1, ffn1_w3, ffn1_w2, mix_norm, w_in, mla_kv_norm, mla_w_kv_b,
           mla_out_norm, ret_gn, hgrn_lb_logits, hgrn_out_norm, w_o, ffn2_norm, ffn2_w1,
           ffn2_w3, ffn2_w2, final_norm):
    batch, seq, d = x.shape
    assert d == D_MODEL and seq % ATT_TQ == 0 and seq % ATT_TK == 0 and seq % REC_ROWS == 0
    assert REC_ROWS % RET_C == 0 and RET_C % HG_C == 0 and REC_ROWS % ATT_TK == 0
    assert ATT_TK == ATT_TQ
    t = batch * seq
    assert t % PROJ_TM == 0 and t % FFN_TM == 0
    c4, sa, sb, cosf, sinf = _rope_tables(seq)
    w_in_p = _prep_w_in(w_in)
    w_kv_p = _prep_w_kv(mla_w_kv_b)
    row = lambda a: a.reshape(1, -1)

    w_o_b = w_o.astype(BF16)
    ffn_w = (ffn1_w1[0].astype(BF16), ffn1_w3[0].astype(BF16), ffn1_w2[0].astype(BF16))
    h = x.reshape(t, d)
    for l in range(DEPTH):
        h, ffn_w = _ffn(h, row(ffn1_norm[l]), *ffn_w, nxt=(ffn2_w1, ffn2_w3, ffn2_w2, l))
        proj = _inproj(h, row(mix_norm[l]), w_in_p, l)
        ob, oc, kcat, vt = _recurrent(proj, cosf, sinf, row(ret_gn[l]), hgrn_lb_logits,
                                      row(hgrn_out_norm[l]), row(mla_kv_norm[l]), w_kv_p,
                                      c4, sa, sb, l, batch, seq)
        oa = _attention(proj, c4, sa, sb, kcat, vt, mla_out_norm[l].reshape(-1, 1), batch, seq)
        h = _outproj(h, oa, ob, oc, w_o_b, l)
        last = l == DEPTH - 1
        h, ffn_w = _ffn(h, row(ffn2_norm[l]), *ffn_w,
                        nxt=None if last else (ffn1_w1, ffn1_w3, ffn1_w2, l + 1),
                        final_w=row(final_norm) if last else None)
    return h.reshape(batch, seq, d)
```

```python
import functools
import math

import jax
import jax.numpy as jnp
from jax import lax
from jax.experimental import pallas as pl
from jax.experimental.pallas import tpu as pltpu

F32 = jnp.float32
BF16 = jnp.bfloat16

D_MODEL = 2048
DEPTH = 4
MLA_HEADS = 8
MLA_NOPE = 128
MLA_ROPE = 64
MLA_V = 128
KV_RANK = 512
RET_HEADS = 4
RET_D = 128
HG_HEADS = 4
HG_D = 128
D_FF = 5632
ROPE_BASE = 10000.0
EPS = 1e-6
MASK_VALUE = -1e30
MIN_FORGET = 1e-20

LANES = 128
SUBLANES = 8
VMEM_LIMIT = 60 * 1024 * 1024

MLA_QN = MLA_HEADS * MLA_NOPE
MLA_QR = MLA_HEADS * MLA_ROPE
MLA_OUT = MLA_HEADS * MLA_V
KV_W = MLA_HEADS * (MLA_NOPE + MLA_V)
KCAT_HEAD = 2 * LANES
KCAT_W = MLA_HEADS * KCAT_HEAD
RET_OUT = RET_HEADS * RET_D
HG_OUT = HG_HEADS * HG_D
MIX_W = 4 * RET_OUT
assert RET_OUT == HG_OUT

COL_QN = 0
COL_QR = COL_QN + MLA_QN
COL_CKV = COL_QR + MLA_QR
COL_RET = COL_CKV + KV_RANK
COL_HG = COL_RET + MIX_W
COL_KR = COL_HG + MIX_W
IN_TN = 1280
D_IN_PAD = 5 * IN_TN
assert D_IN_PAD >= COL_KR + LANES

FFN_TM = 1024
FFN_TF = 512
PROJ_TM = 1024
ATT_TQ = 512
ATT_TK = 512
ATT_AHEAD = 2
ATT_ONES = 16
RET_C = 256
HG_C = 128
REC_ROWS = 512
HG_NEAR = 4
OUT_TM = 512
WIN_TK = 256


def _cparams(sem):
    return pltpu.CompilerParams(dimension_semantics=sem, vmem_limit_bytes=VMEM_LIMIT)


def _rms(x, w):
    return (x * lax.rsqrt(jnp.mean(x * x, axis=-1, keepdims=True) + EPS)) * w


def _sigmoid(x):
    return 1.0 / (1.0 + jnp.exp(-x))


def _dot(a, b):
    return jnp.dot(a, b, preferred_element_type=F32)


def _dot_nt(a, b):
    return lax.dot_general(a, b, (((1,), (1,)), ((), ())), preferred_element_type=F32)


def _dot_tn(a, b):
    return lax.dot_general(a, b, (((0,), (0,)), ((), ())), preferred_element_type=F32)


def _ffn_kernel(x_ref, nw_ref, w1_ref, w3_ref, w2_ref, *rest, n_f, final, convert):
    rest = list(rest)
    cast_in = [rest.pop(0) for _ in range(3)] if convert else []
    fw_ref = rest.pop(0) if final else None
    o_ref = rest.pop(0)
    cast_out = [rest.pop(0) for _ in range(3)] if convert else []
    (n_ref,) = rest
    f = pl.program_id(1)

    for src, dst in zip(cast_in, cast_out):
        dst[...] = src[...].astype(BF16)

    @pl.when(f == 0)
    def _():
        x = x_ref[...]
        n_ref[...] = _rms(x, nw_ref[...]).astype(BF16)
        o_ref[...] = x

    n = n_ref[...]
    h1 = _dot(n, w1_ref[...])
    h3 = _dot(n, w3_ref[...])
    g = (h1 * _sigmoid(h1) * h3 * 0.5).astype(BF16)
    o_ref[...] += _dot(g, w2_ref[...])

    if final:
        @pl.when(f == n_f - 1)
        def _():
            o_ref[...] = _rms(o_ref[...], fw_ref[...])


def _ffn(h, nw, w1, w3, w2, nxt=None, final_w=None):
    t = h.shape[0]
    n_i = t // FFN_TM
    n_f = D_FF // FFN_TF
    final = final_w is not None
    convert = nxt is not None
    in_specs = [
        pl.BlockSpec((FFN_TM, D_MODEL), lambda i, f: (i, 0)),
        pl.BlockSpec((1, D_MODEL), lambda i, f: (0, 0)),
        pl.BlockSpec((D_MODEL, FFN_TF), lambda i, f: (0, f)),
        pl.BlockSpec((D_MODEL, FFN_TF), lambda i, f: (0, f)),
        pl.BlockSpec((FFN_TF, D_MODEL), lambda i, f: (f, 0)),
    ]
    args = [h, nw, w1, w3, w2]
    out_specs = [pl.BlockSpec((FFN_TM, D_MODEL), lambda i, f: (i, 0))]
    out_shape = [jax.ShapeDtypeStruct((t, D_MODEL), F32)]
    if convert:
        n1, n3, n2, layer = nxt
        dr = D_MODEL // n_i
        in_specs += [
            pl.BlockSpec((None, dr, FFN_TF), lambda i, f: (layer, i, f)),
            pl.BlockSpec((None, dr, FFN_TF), lambda i, f: (layer, i, f)),
            pl.BlockSpec((None, FFN_TF, dr), lambda i, f: (layer, f, i)),
        ]
        args += [n1, n3, n2]
        out_specs += [
            pl.BlockSpec((dr, FFN_TF), lambda i, f: (i, f)),
            pl.BlockSpec((dr, FFN_TF), lambda i, f: (i, f)),
            pl.BlockSpec((FFN_TF, dr), lambda i, f: (f, i)),
        ]
        out_shape += [
            jax.ShapeDtypeStruct((D_MODEL, D_FF), BF16),
            jax.ShapeDtypeStruct((D_MODEL, D_FF), BF16),
            jax.ShapeDtypeStruct((D_FF, D_MODEL), BF16),
        ]
    if final:
        in_specs.append(pl.BlockSpec((1, D_MODEL), lambda i, f: (0, 0)))
        args.append(final_w)
    outs = pl.pallas_call(
        functools.partial(_ffn_kernel, n_f=n_f, final=final, convert=convert),
        grid=(n_i, n_f),
        in_specs=in_specs,
        out_specs=out_specs,
        out_shape=out_shape,
        scratch_shapes=[pltpu.VMEM((FFN_TM, D_MODEL), BF16)],
        compiler_params=_cparams(("parallel", "arbitrary")),
        name="ffn_final" if final else "ffn",
    )(*args)
    return outs[0], tuple(outs[1:])


def _inproj_kernel(x_ref, nw_ref, w_ref, o_ref, n_ref):
    @pl.when(pl.program_id(1) == 0)
    def _():
        n_ref[...] = _rms(x_ref[...], nw_ref[...]).astype(BF16)

    o_ref[...] = _dot(n_ref[...], w_ref[...])


def _inproj(h, nw, w, layer):
    t = h.shape[0]
    return pl.pallas_call(
        _inproj_kernel,
        grid=(t // PROJ_TM, D_IN_PAD // IN_TN),
        in_specs=[
            pl.BlockSpec((PROJ_TM, D_MODEL), lambda i, j: (i, 0)),
            pl.BlockSpec((1, D_MODEL), lambda i, j: (0, 0)),
            pl.BlockSpec((None, D_MODEL, IN_TN), lambda i, j: (layer, 0, j)),
        ],
        out_specs=pl.BlockSpec((PROJ_TM, IN_TN), lambda i, j: (i, j)),
        out_shape=jax.ShapeDtypeStruct((t, D_IN_PAD), F32),
        scratch_shapes=[pltpu.VMEM((PROJ_TM, D_MODEL), BF16)],
        compiler_params=_cparams(("parallel", "arbitrary")),
        name="inproj",
    )(h, nw, w)


def _rope_pair(p, c4, sa, sb):
    return p * c4 + pltpu.roll(p, 96, 1) * sa + pltpu.roll(p, 32, 1) * sb


def _rope_full(x, c, s):
    return x * c + pltpu.roll(x, 64, 1) * s


def _kv_rows(r0, rn, ckv_ref, kr_ref, nw_ref, w_ref, c4_ref, sa_ref, sb_ref, kcat_ref, vt_ref):
    rows = pl.ds(r0, rn)
    n = _rms(ckv_ref[rows, :], nw_ref[...]).astype(BF16)
    kv = _dot(n, w_ref[...])
    kr = _rope_pair(kr_ref[rows, :], c4_ref[rows, :], sa_ref[rows, :], sb_ref[rows, :])
    kr_lo = kr.astype(BF16)
    kr_hi = pltpu.roll(kr, MLA_ROPE, 1).astype(BF16)
    for h in range(MLA_HEADS):
        k0 = h * KCAT_HEAD
        kcat_ref[rows, k0:k0 + MLA_NOPE] = kv[:, h * MLA_NOPE:(h + 1) * MLA_NOPE].astype(BF16)
        kcat_ref[rows, k0 + MLA_NOPE:k0 + KCAT_HEAD] = kr_lo if h % 2 == 0 else kr_hi
    tile, off = divmod(r0, ATT_TK)
    vt_ref[tile, :, off:off + rn] = kv[:, MLA_QN:].T.astype(BF16)


def _attn_kernel(qn_ref, qr_ref, c4_ref, sa_ref, sb_ref, k_ref, vt_ref, nw_ref, o_ref,
                 qt_s, m_s, acc_s):
    i = pl.program_id(1)
    tq, tk = ATT_TQ, ATT_TK
    scale = (MLA_NOPE + MLA_ROPE) ** -0.5 * math.log2(math.e)

    for h in range(MLA_HEADS):
        if h % 2 == 0:
            pair = qr_ref[:, (h // 2) * LANES:(h // 2 + 1) * LANES]
            roped = _rope_pair(pair, c4_ref[...], sa_ref[...], sb_ref[...]) * scale
        qn = qn_ref[:, h * LANES:(h + 1) * LANES] * scale
        qt_s[h] = jnp.concatenate([qn, roped], axis=-1).T.astype(BF16)
    m_s[...] = jnp.full(m_s.shape, MASK_VALUE, F32)
    acc_s[...] = jnp.zeros(acc_s.shape, F32)

    def tile(j, masked):
        ones = jnp.ones((ATT_ONES, tk), BF16)

        def scores(h):
            return _dot(k_ref[j, :, h * KCAT_HEAD:(h + 1) * KCAT_HEAD], qt_s[h])

        pending = [scores(h) for h in range(ATT_AHEAD)]
        for h in range(MLA_HEADS):
            st = pending.pop(0)
            if h + ATT_AHEAD < MLA_HEADS:
                pending.append(scores(h + ATT_AHEAD))
            if masked:
                key = lax.broadcasted_iota(jnp.int32, (tk, tq), 0)
                qry = lax.broadcasted_iota(jnp.int32, (tk, tq), 1)
                st = jnp.where(key <= qry, st, MASK_VALUE)
            m_old = m_s[h]
            m_new = jnp.maximum(m_old, jnp.max(st, axis=0, keepdims=True))
            p = jnp.exp2(st - m_new).astype(BF16)
            alpha = jnp.exp2(m_old - m_new)
            vext = jnp.concatenate([vt_ref[j, h * MLA_V:(h + 1) * MLA_V, :], ones], axis=0)
            acc_s[h] = alpha * acc_s[h] + _dot(vext, p)
            m_s[h] = m_new

    def body(j, carry):
        tile(j, False)
        return carry

    lax.fori_loop(0, i, body, 0)
    tile(i, True)

    for h in range(MLA_HEADS):
        acc = acc_s[h]
        ot = acc[:MLA_V, :] / acc[MLA_V:MLA_V + 1, :]
        ot = ot * lax.rsqrt(jnp.mean(ot * ot, axis=0, keepdims=True) + EPS)
        ot = ot * nw_ref[h * MLA_V:(h + 1) * MLA_V, :]
        o_ref[:, h * MLA_V:(h + 1) * MLA_V] = ot.T.astype(BF16)


def _ret_chunk(x_ref, c_ref, s_ref, gw_ref, o_ref, state_ref, decay_ref, lgs):
    c = RET_C
    idx = lax.broadcasted_iota(jnp.int32, (c, 1), 0).astype(F32)
    cosv = c_ref[...]
    sinv = s_ref[...]
    heads = range(RET_HEADS)
    sls = [slice(h * RET_D, (h + 1) * RET_D) for h in heads]

    qbs, vbs, scores, far = [], [], [], []
    for h in heads:
        q = _rope_full(x_ref[:, sls[h]], cosv, sinv)
        k = _rope_full(x_ref[:, RET_OUT + h * RET_D:RET_OUT + (h + 1) * RET_D], cosv, sinv)
        k = k * (RET_D ** -0.5)
        qb = q.astype(BF16)
        vb = x_ref[:, 2 * RET_OUT + h * RET_D:2 * RET_OUT + (h + 1) * RET_D].astype(BF16)
        state = state_ref[h]
        scores.append(_dot_nt(qb, k.astype(BF16)))
        far.append(_dot(qb, state.astype(BF16)))
        zeta = jnp.exp(lgs[h] * (c - 1.0 - idx))
        state_ref[h] = state * math.exp(lgs[h] * c) + _dot_tn((k * zeta).astype(BF16), vb)
        qbs.append(qb)
        vbs.append(vb)

    outs = []
    for h in heads:
        xi = jnp.exp(lgs[h] * (idx + 1.0))
        outs.append(_dot((scores[h] * decay_ref[h]).astype(BF16), vbs[h]) + far[h] * xi)

    for h in heads:
        o = outs[h]
        mu = jnp.mean(o, axis=-1, keepdims=True)
        d = o - mu
        var = jnp.mean(d * d, axis=-1, keepdims=True)
        o = d * lax.rsqrt(var + EPS) * gw_ref[:, sls[h]]
        g = x_ref[:, 3 * RET_OUT + h * RET_D:3 * RET_OUT + (h + 1) * RET_D]
        o_ref[:, sls[h]] = (g * _sigmoid(g) * o).astype(BF16)


def _group_roll(x, shift):
    n, w = x.shape
    return pltpu.roll(x.reshape(n // SUBLANES, SUBLANES, w), shift, 1).reshape(n, w)


def _rec_kernel(xr_ref, xh_ref, c_ref, s_ref, gw_ref, lbl_ref, nw_ref,
                ckv_ref, kr_ref, kvw_ref, wkv_ref, c4_ref, sa_ref, sb_ref,
                qn_ref, qr_ref, onw_ref,
                ob_ref, oc_ref, oa_ref,
                rstate_ref, decay_ref, hstate_ref, kcat_s, vt_s, qt_s, m_s, acc_s, *, layer):
    lgs = [math.log1p(-(2.0 ** (-5.0 - h))) for h in range(RET_HEADS)]

    @pl.when(pl.program_id(1) == 0)
    def _():
        hstate_ref[...] = jnp.zeros_like(hstate_ref)
        rstate_ref[...] = jnp.zeros_like(rstate_ref)
        row = lax.broadcasted_iota(jnp.int32, (RET_C, RET_C), 0)
        col = lax.broadcasted_iota(jnp.int32, (RET_C, RET_C), 1)
        rel = (row - col).astype(F32)
        for h in range(RET_HEADS):
            decay_ref[h] = jnp.where(rel >= 0, jnp.exp(lgs[h] * jnp.maximum(rel, 0.0)), 0.0)

    logits = lbl_ref[...]
    e = jnp.exp(logits - jnp.max(logits, axis=0, keepdims=True))
    p = e / jnp.sum(e, axis=0, keepdims=True)
    lb_all = jnp.zeros((1, HG_HEADS * HG_D), F32)
    for m in range(layer + 1):
        lb_all = lb_all + p[m:m + 1, :]
    lb_all = lb_all - p[0:1, :]

    tile_c = pl.program_id(1)
    kcat_ref = kcat_s.at[tile_c]
    vt_ref = vt_s.at[pl.ds(tile_c, 1)]
    hg_per_ret = RET_C // HG_C
    for s in range(REC_ROWS // RET_C):
        rows = pl.ds(s * RET_C, RET_C)
        _ret_chunk(xr_ref.at[rows], c_ref.at[rows], s_ref.at[rows], gw_ref, ob_ref.at[rows],
                   rstate_ref, decay_ref, lgs)
        _kv_rows(s * RET_C, RET_C, ckv_ref, kr_ref, kvw_ref, wkv_ref, c4_ref, sa_ref, sb_ref,
                 kcat_ref, vt_ref)
        for u in range(hg_per_ret):
            rows = pl.ds((s * hg_per_ret + u) * HG_C, HG_C)
            _hgrn_chunk(xh_ref.at[rows], lb_all, nw_ref, oc_ref.at[rows], hstate_ref)

    _attn_kernel(qn_ref, qr_ref, c4_ref, sa_ref, sb_ref, kcat_s, vt_s, onw_ref, oa_ref,
                 qt_s, m_s, acc_s)


def _recurrent(proj, cosf, sinf, gw, lb_logits, nw, kv_nw, w_kv, c4, sa, sb, onw_col, layer,
               batch, seq):
    t = proj.shape[0]
    nc = seq // REC_ROWS
    nk = seq // ATT_TK
    tab = pl.BlockSpec((REC_ROWS, LANES), lambda b, c: (c, 0))
    out = pl.BlockSpec((REC_ROWS, RET_OUT), lambda b, c: (b * nc + c, 0))
    return pl.pallas_call(
        functools.partial(_rec_kernel, layer=layer),
        grid=(batch, nc),
        in_specs=[
            pl.BlockSpec((REC_ROWS, MIX_W), lambda b, c: (b * nc + c, COL_RET // MIX_W)),
            pl.BlockSpec((REC_ROWS, MIX_W), lambda b, c: (b * nc + c, COL_HG // MIX_W)),
            tab, tab,
            pl.BlockSpec((1, RET_OUT), lambda b, c: (0, 0)),
            pl.BlockSpec((DEPTH, HG_OUT), lambda b, c: (0, 0)),
            pl.BlockSpec((1, HG_OUT), lambda b, c: (0, 0)),
            pl.BlockSpec((REC_ROWS, KV_RANK), lambda b, c: (b * nc + c, COL_CKV // KV_RANK)),
            pl.BlockSpec((REC_ROWS, LANES), lambda b, c: (b * nc + c, COL_KR // LANES)),
            pl.BlockSpec((1, KV_RANK), lambda b, c: (0, 0)),
            pl.BlockSpec((None, KV_RANK, KV_W), lambda b, c: (layer, 0, 0)),
            tab, tab, tab,
            pl.BlockSpec((REC_ROWS, MLA_QN), lambda b, c: (b * nc + c, COL_QN // MLA_QN)),
            pl.BlockSpec((REC_ROWS, MLA_QR), lambda b, c: (b * nc + c, COL_QR // MLA_QR)),
            pl.BlockSpec((MLA_OUT, 1), lambda b, c: (0, 0)),
        ],
        out_specs=[
            out, out,
            pl.BlockSpec((REC_ROWS, MLA_OUT), lambda b, c: (b * nc + c, 0)),
        ],
        out_shape=[
            jax.ShapeDtypeStruct((t, RET_OUT), BF16),
            jax.ShapeDtypeStruct((t, HG_OUT), BF16),
            jax.ShapeDtypeStruct((t, MLA_OUT), BF16),
        ],
        scratch_shapes=[pltpu.VMEM((RET_HEADS, RET_D, RET_D), F32),
                        pltpu.VMEM((RET_HEADS, RET_C, RET_C), F32),
                        pltpu.VMEM((HG_HEADS, HG_D, HG_D), F32),
                        pltpu.VMEM((nk, ATT_TK, KCAT_W), BF16),
                        pltpu.VMEM((nk, MLA_OUT, ATT_TK), BF16),
                        pltpu.VMEM((MLA_HEADS, KCAT_HEAD, ATT_TQ), BF16),
                        pltpu.VMEM((MLA_HEADS, 1, ATT_TQ), F32),
                        pltpu.VMEM((MLA_HEADS, MLA_V + ATT_ONES, ATT_TQ), F32)],
        compiler_params=_cparams(("parallel", "arbitrary")),
        name="mixers",
    )(proj, proj, cosf, sinf, gw, lb_logits, nw, proj, proj, kv_nw, w_kv, c4, sa, sb,
      proj, proj, onw_col)


def _hgrn_chunk(x_ref, lb_all, nw_ref, o_ref, state_ref):
    c = HG_C
    rowv = lax.broadcasted_iota(jnp.int32, (c, LANES), 0)
    row = lax.broadcasted_iota(jnp.int32, (c, c), 0)
    col = lax.broadcasted_iota(jnp.int32, (c, c), 1)

    z = x_ref[:, HG_OUT:2 * HG_OUT]
    ez = jnp.exp(-jnp.abs(z))
    r = 1.0 / (1.0 + ez)
    pos = z >= 0
    sig_p = jnp.where(pos, r, ez * r)
    sig_n = jnp.where(pos, ez * r, r)
    f = lb_all + (1.0 - lb_all) * sig_p
    lf = jnp.log2(jnp.maximum(f, MIN_FORGET))
    kk_all = (1.0 - lb_all) * sig_n

    tri = jnp.where(col <= row, 1.0, 0.0).astype(BF16)
    lf_hi = lf.astype(BF16)
    rem = lf - lf_hi.astype(F32)
    lf_mid = rem.astype(BF16)
    lf_lo = (rem - lf_mid.astype(F32)).astype(BF16)
    b_all = _dot(tri, lf_hi) + _dot(tri, lf_mid) + _dot(tri, lf_lo)

    heads = range(HG_HEADS)
    sls = [slice(h * HG_D, (h + 1) * HG_D) for h in heads]
    qs = [x_ref[:, sls[h]] for h in heads]
    kks = [kk_all[:, sls[h]] for h in heads]
    bs = [b_all[:, sls[h]] for h in heads]
    vbs = [x_ref[:, 2 * HG_OUT + h * HG_D:2 * HG_OUT + (h + 1) * HG_D].astype(BF16) for h in heads]


    o_far = []
    for h in heads:
        state = state_ref[h]
        b_last = bs[h][c - 1:c, :]
        o_far.append(_dot_nt((qs[h] * jnp.exp2(bs[h])).astype(BF16), state.astype(BF16)))
        upd = _dot_tn(vbs[h], (kks[h] * jnp.exp2(b_last - bs[h])).astype(BF16))
        state_ref[h] = jnp.exp2(b_last) * state + upd

    a_lvl = []
    for h in heads:
        q, kk, b = qs[h], kks[h], bs[h]
        a = None
        m = HG_NEAR
        while m < c:
            second = (rowv & (2 * m - 1)) >= m
            parts = []
            for blk in range(c // (2 * m)):
                lo = blk * 2 * m
                ref = b[lo + m - 1:lo + m, :]
                if m >= SUBLANES:
                    parts += [ref - b[lo:lo + m, :], b[lo + m:lo + 2 * m, :] - ref]
                else:
                    blk_d = b[lo:lo + 2 * m, :] - ref
                    parts.append(jnp.where(second[lo:lo + 2 * m, :], blk_d, -blk_d))
            efac = jnp.exp2(jnp.concatenate(parts, axis=0))
            ql = jnp.where(second, q * efac, 0.0).astype(BF16)
            kl = jnp.where(second, 0.0, kk * efac).astype(BF16)
            al = _dot_nt(ql, kl)
            if 2 * m < c:
                sft = (2 * m).bit_length() - 1
                al = jnp.where((row >> sft) == (col >> sft), al, 0.0)
            a = al if a is None else a + al
            m *= 2
        a_lvl.append(a)

    a_all = []
    for h in heads:
        q, kk, b = qs[h], kks[h], bs[h]
        a = a_lvl[h] + jnp.where(col == row, jnp.sum(q * kk, axis=-1, keepdims=True), 0.0)
        for dlt in range(1, HG_NEAR):
            ok = (rowv & (HG_NEAR - 1)) >= dlt
            diff = jnp.where(ok, b - _group_roll(b, dlt), 0.0)
            a_d = jnp.sum(q * _group_roll(kk, dlt) * jnp.exp2(diff), axis=-1, keepdims=True)
            a = a + jnp.where((col == row - dlt) & ((row & (HG_NEAR - 1)) >= dlt), a_d, 0.0)
        a_all.append(a)

    for h in heads:
        o = _dot(a_all[h].astype(BF16), vbs[h]) + o_far[h]
        o = o * lax.rsqrt(jnp.mean(o * o, axis=-1, keepdims=True) + EPS)
        o = o * nw_ref[:, sls[h]]
        g = x_ref[:, 3 * HG_OUT + h * HG_D:3 * HG_OUT + (h + 1) * HG_D]
        o_ref[:, sls[h]] = (g * _sigmoid(g) * o).astype(BF16)


def _outproj_kernel(h_ref, oa_ref, ob_ref, oc_ref, w_ref, o_ref):
    acc = _dot(oa_ref[...], w_ref[:MLA_OUT, :])
    acc += _dot(ob_ref[...], w_ref[MLA_OUT:MLA_OUT + RET_OUT, :])
    acc += _dot(oc_ref[...], w_ref[MLA_OUT + RET_OUT:, :])
    o_ref[...] = h_ref[...] + acc


def _outproj(h, oa, ob, oc, w, layer):
    t = h.shape[0]
    return pl.pallas_call(
        _outproj_kernel,
        grid=(t // OUT_TM,),
        in_specs=[
            pl.BlockSpec((OUT_TM, D_MODEL), lambda i: (i, 0)),
            pl.BlockSpec((OUT_TM, MLA_OUT), lambda i: (i, 0)),
            pl.BlockSpec((OUT_TM, RET_OUT), lambda i: (i, 0)),
            pl.BlockSpec((OUT_TM, HG_OUT), lambda i: (i, 0)),
            pl.BlockSpec((None, D_MODEL, D_MODEL), lambda i: (layer, 0, 0)),
        ],
        out_specs=pl.BlockSpec((OUT_TM, D_MODEL), lambda i: (i, 0)),
        out_shape=jax.ShapeDtypeStruct((t, D_MODEL), F32),
        compiler_params=_cparams(("parallel",)),
        name="outproj",
    )(h, oa, ob, oc, w)


def _rope_tables(seq):
    inv64 = ROPE_BASE ** (-jnp.arange(0, MLA_ROPE, 2, dtype=F32) / MLA_ROPE)
    ang64 = jnp.arange(seq, dtype=F32)[:, None] * inv64[None, :]
    c, s = jnp.cos(ang64), jnp.sin(ang64)
    z = jnp.zeros_like(s)
    c4 = jnp.concatenate([c, c, c, c], axis=-1)
    sa = jnp.concatenate([-s, z, -s, z], axis=-1)
    sb = jnp.concatenate([z, s, z, s], axis=-1)
    inv128 = ROPE_BASE ** (-jnp.arange(0, RET_D, 2, dtype=F32) / RET_D)
    ang128 = jnp.arange(seq, dtype=F32)[:, None] * inv128[None, :]
    cf, sf = jnp.cos(ang128), jnp.sin(ang128)
    return c4, sa, sb, jnp.concatenate([cf, cf], axis=-1), jnp.concatenate([-sf, sf], axis=-1)


def _win_kernel(wt_ref, o_ref):
    hd = MLA_NOPE + MLA_ROPE
    nq = MLA_HEADS * hd
    tk = o_ref.shape[0]

    def put(col, *row_ranges):
        pieces = [wt_ref[r0:r1, :] for r0, r1 in row_ranges]
        piece = pieces[0] if len(pieces) == 1 else jnp.concatenate(pieces, axis=0)
        o_ref[:, col:col + piece.shape[0]] = piece.T.astype(BF16)

    for h in range(MLA_HEADS):
        put(COL_QN + h * MLA_NOPE, (h * hd, h * hd + MLA_NOPE))
    for p in range(MLA_HEADS // 2):
        put(COL_QR + p * LANES, ((2 * p) * hd + MLA_NOPE, (2 * p + 1) * hd),
            ((2 * p + 1) * hd + MLA_NOPE, (2 * p + 2) * hd))
    for blk in range(KV_RANK // LANES):
        put(COL_CKV + blk * LANES, (nq + blk * LANES, nq + (blk + 1) * LANES))
    src = nq + KV_RANK + MLA_ROPE
    for blk in range((COL_KR - COL_RET) // LANES):
        put(COL_RET + blk * LANES, (src + blk * LANES, src + (blk + 1) * LANES))
    kr = jnp.concatenate([wt_ref[nq + KV_RANK:nq + KV_RANK + MLA_ROPE, :],
                          jnp.zeros((LANES - MLA_ROPE, tk), F32)], axis=0)
    o_ref[:, COL_KR:COL_KR + LANES] = kr.T.astype(BF16)
    o_ref[:, COL_KR + LANES:] = jnp.zeros((tk, D_IN_PAD - COL_KR - LANES), BF16)


def _prep_w_in(w_in):
    depth, d, n = w_in.shape
    return pl.pallas_call(
        _win_kernel,
        grid=(depth, d // WIN_TK),
        in_specs=[pl.BlockSpec((None, n, WIN_TK), lambda l, r: (l, 0, r))],
        out_specs=pl.BlockSpec((None, WIN_TK, D_IN_PAD), lambda l, r: (l, r, 0)),
        out_shape=jax.ShapeDtypeStruct((depth, d, D_IN_PAD), BF16),
        compiler_params=_cparams(("parallel", "parallel")),
        name="w_in_relayout",
    )(jnp.swapaxes(w_in, 1, 2))


def _prep_w_kv(w):
    w = w.reshape(DEPTH, KV_RANK, MLA_HEADS, MLA_NOPE + MLA_V)
    kn = w[..., :MLA_NOPE].reshape(DEPTH, KV_RANK, MLA_HEADS * MLA_NOPE)
    v = w[..., MLA_NOPE:].reshape(DEPTH, KV_RANK, MLA_HEADS * MLA_V)
    return jnp.concatenate([kn, v], axis=-1).astype(BF16)


def kernel(x, ffn1_norm, ffn1_w1, ffn1_w3, ffn1_w2, mix_norm, w_in, mla_kv_norm, mla_w_kv_b,
           mla_out_norm, ret_gn, hgrn_lb_logits, hgrn_out_norm, w_o, ffn2_norm, ffn2_w1,
           ffn2_w3, ffn2_w2, final_norm):
    batch, seq, d = x.shape
    assert d == D_MODEL and seq % ATT_TQ == 0 and seq % ATT_TK == 0 and seq % REC_ROWS == 0
    assert REC_ROWS % RET_C == 0 and RET_C % HG_C == 0 and REC_ROWS % ATT_TK == 0
    assert ATT_TK == ATT_TQ == REC_ROWS
    t = batch * seq
    assert t % PROJ_TM == 0 and t % FFN_TM == 0
    c4, sa, sb, cosf, sinf = _rope_tables(seq)
    w_in_p = _prep_w_in(w_in)
    w_kv_p = _prep_w_kv(mla_w_kv_b)
    row = lambda a: a.reshape(1, -1)

    w_o_b = w_o.astype(BF16)
    ffn_w = (ffn1_w1[0].astype(BF16), ffn1_w3[0].astype(BF16), ffn1_w2[0].astype(BF16))
    h = x.reshape(t, d)
    for l in range(DEPTH):
        h, ffn_w = _ffn(h, row(ffn1_norm[l]), *ffn_w, nxt=(ffn2_w1, ffn2_w3, ffn2_w2, l))
        proj = _inproj(h, row(mix_norm[l]), w_in_p, l)
        ob, oc, oa = _recurrent(proj, cosf, sinf, row(ret_gn[l]), hgrn_lb_logits,
                                row(hgrn_out_norm[l]), row(mla_kv_norm[l]), w_kv_p,
                                c4, sa, sb, mla_out_norm[l].reshape(-1, 1), l, batch, seq)
        h = _outproj(h, oa, ob, oc, w_o_b, l)
        last = l == DEPTH - 1
        h, ffn_w = _ffn(h, row(ffn2_norm[l]), *ffn_w,
                        nxt=None if last else (ffn1_w1, ffn1_w3, ffn1_w2, l + 1),
                        final_w=row(final_norm) if last else None)
    return h.reshape(batch, seq, d)
```
